```python
import jax, jax.numpy as jnp
from jax import lax
import numpy as np

D_MODEL = 1024
BATCH = 16
SEQ = 4096
DEPTH = 2
DEC_BATCH = 8
DEC_SEQ = 4096
PAST_LEN = 128

N_META = 16
BLOCK = 128
FRONT_PAD = BLOCK - N_META
D_FF = 2816
EPS = 1e-6
NEG = -1e30

A_HEADS = 8
A_DK = 64
A_DV = 64
A_THETA = 10000.0
B_HEADS = 8
B_Q_LORA = 256
B_KV_LORA = 128
B_NOPE = 64
B_ROPE = 32
B_DV = 64
B_THETA = 10000.0
C_HEADS = 16
C_KV_HEADS = 4
C_GROUP = C_HEADS // C_KV_HEADS
C_DH = 64
C_ROT = C_DH // 4
C_THETA = 500000.0
C_WINDOW = 128

EVEN_SIZES = (A_HEADS * A_DK, A_HEADS * A_DK, A_HEADS * A_DV, A_HEADS * A_DV, B_Q_LORA, B_KV_LORA, B_ROPE)
EVEN_IN = sum(EVEN_SIZES)
EVEN_MIX = A_HEADS * A_DV + B_HEADS * B_DV
ODD_IN = (C_HEADS + 2 * C_KV_HEADS) * C_DH
ODD_MIX = C_HEADS * C_DH
N_EVEN = (DEPTH + 1) // 2
N_ODD = DEPTH // 2

kernel_name = 'hybrid_retention_mla_swa_encoder'


def _rms(x, g):
    x32 = x.astype(jnp.float32)
    y = x32 * lax.rsqrt(jnp.mean(x32 * x32, axis=-1, keepdims=True) + EPS)
    return (y * g.astype(jnp.float32)).astype(x.dtype)


def _rope(x, pos, theta):
    r = x.shape[-1]
    half = r // 2
    inv = theta ** (-jnp.arange(half, dtype=jnp.float32) * 2.0 / r)
    ang = pos[:, None] * inv[None, :]
    cos = jnp.cos(ang)[:, None, :]
    sin = jnp.sin(ang)[:, None, :]
    x32 = x.astype(jnp.float32)
    x1, x2 = x32[..., :half], x32[..., half:]
    return jnp.concatenate([x1 * cos - x2 * sin, x2 * cos + x1 * sin], axis=-1).astype(x.dtype)


def _to_blocks(t):
    tp = jnp.pad(t, [(0, 0), (FRONT_PAD, 0)] + [(0, 0)] * (t.ndim - 2))
    nb = tp.shape[1] // BLOCK
    return tp.reshape((t.shape[0], nb, BLOCK) + t.shape[2:])


def _from_blocks(t):
    return t.reshape((t.shape[0], t.shape[1] * t.shape[2]) + t.shape[3:])[:, FRONT_PAD:]


def _swiglu(x, wg, wu, wd):
    return (jax.nn.silu(x @ wg) * (x @ wu)) @ wd


def _retention(q, k, v, pos, dec_f, dec_b):
    dt = q.dtype
    q = _rope(q, pos, A_THETA)
    k = _rope(k, pos, A_THETA) * (A_DK ** -0.5)
    qb, kb, vb = _to_blocks(q), _to_blocks(k), _to_blocks(v)
    lgf = -jnp.exp(dec_f.astype(jnp.float32))
    lgb = -jnp.exp(dec_b.astype(jnp.float32))
    idx = jnp.arange(BLOCK, dtype=jnp.float32)
    diff = idx[:, None] - idx[None, :]
    dmat = (jnp.where(diff >= 0, jnp.exp(lgf[:, None, None] * jnp.maximum(diff, 0.0)), 0.0)
            + jnp.where(diff < 0, jnp.exp(lgb[:, None, None] * jnp.maximum(-diff, 0.0)), 0.0))
    s = jnp.einsum('bnihd,bnjhd->bnhij', qb, kb) * dmat.astype(dt)
    o = jnp.einsum('bnhij,bnjhe->bnihe', s, vb)
    wkf = jnp.exp(lgf[None, :] * (BLOCK - 1 - idx)[:, None]).astype(dt)
    wkb = jnp.exp(lgb[None, :] * idx[:, None]).astype(dt)
    kvf = jnp.einsum('bnjhd,jh,bnjhe->bnhde', kb, wkf, vb)
    kvb = jnp.einsum('bnjhd,jh,bnjhe->bnhde', kb, wkb, vb)
    cf = jnp.exp(BLOCK * lgf)[:, None, None].astype(dt)
    cb = jnp.exp(BLOCK * lgb)[:, None, None].astype(dt)

    def fwd(state, kv):
        return cf * state + kv, state

    def bwd(state, kv):
        return cb * state + kv, state

    init = jnp.zeros(kvf.shape[:1] + kvf.shape[2:], dt)
    _, sf = lax.scan(fwd, init, jnp.moveaxis(kvf, 1, 0))
    _, sb = lax.scan(bwd, init, jnp.moveaxis(kvb, 1, 0), reverse=True)
    sf = jnp.moveaxis(sf, 0, 1)
    sb = jnp.moveaxis(sb, 0, 1)
    qf = qb * jnp.exp(lgf[None, :] * (idx + 1.0)[:, None]).astype(dt)[:, :, None]
    qbk = qb * jnp.exp(lgb[None, :] * (BLOCK - idx)[:, None]).astype(dt)[:, :, None]
    o = o + jnp.einsum('bnihd,bnhde->bnihe', qf, sf) + jnp.einsum('bnihd,bnhde->bnihe', qbk, sb)
    return _from_blocks(o)


def _mla(cq, ckv, kr, pos, q_norm, w_qb, kv_norm, w_kvb, gq, gk):
    b, l, _ = cq.shape
    q = (_rms(cq, q_norm) @ w_qb).reshape(b, l, B_HEADS, B_NOPE + B_ROPE)
    kv = (_rms(ckv, kv_norm) @ w_kvb).reshape(b, l, B_HEADS, B_NOPE + B_DV)
    k_nope, v = kv[..., :B_NOPE], kv[..., B_NOPE:]
    k = jnp.concatenate([k_nope, jnp.broadcast_to(kr[:, :, None, :], (b, l, B_HEADS, B_ROPE))], axis=-1)
    q = _rms(q, gq)
    k = _rms(k, gk)
    q = jnp.concatenate([q[..., :B_NOPE], _rope(q[..., B_NOPE:], pos, B_THETA)], axis=-1)
    k = jnp.concatenate([k[..., :B_NOPE], _rope(k[..., B_NOPE:], pos, B_THETA)], axis=-1)
    scale = (B_NOPE + B_ROPE) ** -0.5
    dt = q.dtype
    qb = jnp.moveaxis(_to_blocks(q), 1, 0)

    def blk(qn):
        s = jnp.einsum('bqhd,bkhd->bhqk', qn, k).astype(jnp.float32) * scale
        p = jax.nn.softmax(s, axis=-1).astype(dt)
        return jnp.einsum('bhqk,bkhe->bqhe', p, v)

    o = lax.map(blk, qb)
    return _from_blocks(jnp.moveaxis(o, 0, 1)).reshape(b, l, B_HEADS * B_DV)


def _even_mixer(hn, pos, w_in, dec_f, dec_b, ret_norm, q_norm, w_qb, kv_norm, w_kvb, gq, gk, w_out):
    b, l, _ = hn.shape
    z = hn @ w_in
    cuts = [int(c) for c in np.cumsum(EVEN_SIZES)[:-1]]
    qa, ka, va, ga, cq, ckv, kr = jnp.split(z, cuts, axis=-1)
    ret = _retention(qa.reshape(b, l, A_HEADS, A_DK), ka.reshape(b, l, A_HEADS, A_DK),
                     va.reshape(b, l, A_HEADS, A_DV), pos, dec_f, dec_b)
    ret = _rms(ret, ret_norm) * jax.nn.silu(ga.reshape(b, l, A_HEADS, A_DV))
    mla = _mla(cq, ckv, kr, pos, q_norm, w_qb, kv_norm, w_kvb, gq, gk)
    return jnp.concatenate([ret.reshape(b, l, A_HEADS * A_DV), mla], axis=-1) @ w_out


def _band(t, nb):
    b = t.shape[0]
    tp = jnp.pad(t, [(0, 0), (FRONT_PAD + BLOCK, BLOCK), (0, 0), (0, 0)])
    tb = tp.reshape((b, nb + 2, BLOCK) + t.shape[2:])
    nbr = jnp.concatenate([tb[:, :-2], tb[:, 1:-1], tb[:, 2:]], axis=2)
    return jnp.moveaxis(nbr, 1, 0)


def _window_mixer(hn, pos, w_in, gq, gk, sink, w_out):
    b, l, _ = hn.shape
    z = hn @ w_in
    q, k, v = jnp.split(z, [C_HEADS * C_DH, (C_HEADS + C_KV_HEADS) * C_DH], axis=-1)
    q = _rms(q.reshape(b, l, C_HEADS, C_DH), gq)
    k = _rms(k.reshape(b, l, C_KV_HEADS, C_DH), gk)
    v = v.reshape(b, l, C_KV_HEADS, C_DH)
    q = jnp.concatenate([_rope(q[..., :C_ROT], pos, C_THETA), q[..., C_ROT:]], axis=-1)
    k = jnp.concatenate([_rope(k[..., :C_ROT], pos, C_THETA), k[..., C_ROT:]], axis=-1)
    dt = q.dtype
    mk, mv = k[:, :N_META], v[:, :N_META]
    qblk = _to_blocks(q)
    nb = qblk.shape[1]
    qb = jnp.moveaxis(qblk.reshape(b, nb, BLOCK, C_KV_HEADS, C_GROUP, C_DH), 1, 0)
    kn, vn = _band(k, nb), _band(v, nb)
    n = jnp.arange(nb, dtype=jnp.int32)[:, None]
    q_pos = n * BLOCK + jnp.arange(BLOCK, dtype=jnp.int32)[None, :] - FRONT_PAD
    k_pos = (n - 1) * BLOCK + jnp.arange(3 * BLOCK, dtype=jnp.int32)[None, :] - FRONT_PAD
    valid = ((k_pos[:, None, :] >= N_META) & (k_pos[:, None, :] < l)
             & (jnp.abs(q_pos[:, :, None] - k_pos[:, None, :]) <= C_WINDOW))
    mask = jnp.concatenate([jnp.ones((nb, BLOCK, N_META), dtype=bool), valid], axis=-1)
    sink_l = sink.astype(jnp.float32).reshape(C_KV_HEADS, C_GROUP)[None, :, :, None, None]
    scale = C_DH ** -0.5

    def blk(args):
        qn, kb, vb, m = args
        kall = jnp.concatenate([mk, kb], axis=1)
        vall = jnp.concatenate([mv, vb], axis=1)
        s = jnp.einsum('bqkgd,bjkd->bkgqj', qn, kall).astype(jnp.float32) * scale
        s = jnp.where(m[None, None, None], s, NEG)
        s = jnp.concatenate([s, jnp.broadcast_to(sink_l, s.shape[:-1] + (1,))], axis=-1)
        p = jax.nn.softmax(s, axis=-1)[..., :-1].astype(dt)
        return jnp.einsum('bkgqj,bjkd->bqkgd', p, vall)

    o = lax.map(blk, (qb, kn, vn, mask))
    o = _from_blocks(jnp.moveaxis(o, 0, 1)).reshape(b, l, ODD_MIX)
    return o @ w_out


def _trunk(x, meta_tokens, ffn_norm, ffn_w_gate, ffn_w_up, ffn_w_down, mix_norm,
           even_w_in, ret_decay_f, ret_decay_b, ret_out_norm, mla_q_norm, mla_w_qb,
           mla_kv_norm, mla_w_kvb, mla_qk_norm_q, mla_qk_norm_k, even_w_out,
           odd_w_in, swa_q_norm, swa_k_norm, swa_sink, odd_w_out):
    b = x.shape[0]
    h = jnp.concatenate([jnp.broadcast_to(meta_tokens[None].astype(x.dtype), (b, N_META, D_MODEL)), x], axis=1)
    pos = jnp.arange(h.shape[1], dtype=jnp.float32)
    for layer in range(DEPTH):
        h = h + 0.5 * _swiglu(_rms(h, ffn_norm[layer, 0]), ffn_w_gate[layer, 0], ffn_w_up[layer, 0], ffn_w_down[layer, 0])
        hn = _rms(h, mix_norm[layer])
        i = layer // 2
        if layer % 2 == 0:
            h = h + _even_mixer(hn, pos, even_w_in[i], ret_decay_f[i], ret_decay_b[i], ret_out_norm[i],
                                mla_q_norm[i], mla_w_qb[i], mla_kv_norm[i], mla_w_kvb[i],
                                mla_qk_norm_q[i], mla_qk_norm_k[i], even_w_out[i])
        else:
            h = h + _window_mixer(hn, pos, odd_w_in[i], swa_q_norm[i], swa_k_norm[i], swa_sink[i], odd_w_out[i])
        h = h + 0.5 * _swiglu(_rms(h, ffn_norm[layer, 1]), ffn_w_gate[layer, 1], ffn_w_up[layer, 1], ffn_w_down[layer, 1])
    return h[:, N_META:]


def setup_inputs(seed: int = 0) -> dict:
    key = jax.random.key(seed)
    ks = iter(jax.random.split(key, 32))
    f32 = jnp.float32

    def w(shape, fan_in):
        return jax.random.normal(next(ks), shape, f32) * (fan_in ** -0.5)

    def gain(shape):
        return 1.0 + 0.02 * jax.random.normal(next(ks), shape, f32)

    base = jnp.log(-jnp.log1p(-(2.0 ** (-5.0 - jnp.arange(A_HEADS, dtype=f32)))))
    return {
        'x_prompt': jax.random.normal(next(ks), (BATCH, SEQ, D_MODEL), f32),
        'x_sample': jax.random.normal(next(ks), (DEC_BATCH, DEC_SEQ, D_MODEL), f32),
        'meta_tokens': jax.random.normal(next(ks), (N_META, D_MODEL), f32),
        'ffn_norm': gain((DEPTH, 2, D_MODEL)),
        'ffn_w_gate': w((DEPTH, 2, D_MODEL, D_FF), D_MODEL),
        'ffn_w_up': w((DEPTH, 2, D_MODEL, D_FF), D_MODEL),
        'ffn_w_down': w((DEPTH, 2, D_FF, D_MODEL), D_FF),
        'mix_norm': gain((DEPTH, D_MODEL)),
        'even_w_in': w((N_EVEN, D_MODEL, EVEN_IN), D_MODEL),
        'ret_decay_f': base[None] + 0.1 * jax.random.normal(next(ks), (N_EVEN, A_HEADS), f32),
        'ret_decay_b': base[None] + 0.1 * jax.random.normal(next(ks), (N_EVEN, A_HEADS), f32),
        'ret_out_norm': gain((N_EVEN, A_HEADS, A_DV)),
        'mla_q_norm': gain((N_EVEN, B_Q_LORA)),
        'mla_w_qb': w((N_EVEN, B_Q_LORA, B_HEADS * (B_NOPE + B_ROPE)), B_Q_LORA),
        'mla_kv_norm': gain((N_EVEN, B_KV_LORA)),
        'mla_w_kvb': w((N_EVEN, B_KV_LORA, B_HEADS * (B_NOPE + B_DV)), B_KV_LORA),
        'mla_qk_norm_q': gain((N_EVEN, B_NOPE + B_ROPE)),
        'mla_qk_norm_k': gain((N_EVEN, B_NOPE + B_ROPE)),
        'even_w_out': w((N_EVEN, EVEN_MIX, D_MODEL), EVEN_MIX),
        'odd_w_in': w((N_ODD, D_MODEL, ODD_IN), D_MODEL),
        'swa_q_norm': gain((N_ODD, C_DH)),
        'swa_k_norm': gain((N_ODD, C_DH)),
        'swa_sink': 0.5 * jax.random.normal(next(ks), (N_ODD, C_HEADS), f32),
        'odd_w_out': w((N_ODD, ODD_MIX, D_MODEL), ODD_MIX),
    }


def reference(x_prompt, x_sample, meta_tokens, ffn_norm, ffn_w_gate, ffn_w_up, ffn_w_down, mix_norm,
              even_w_in, ret_decay_f, ret_decay_b, ret_out_norm, mla_q_norm, mla_w_qb, mla_kv_norm,
              mla_w_kvb, mla_qk_norm_q, mla_qk_norm_k, even_w_out, odd_w_in, swa_q_norm, swa_k_norm,
              swa_sink, odd_w_out):
    params = (meta_tokens, ffn_norm, ffn_w_gate, ffn_w_up, ffn_w_down, mix_norm,
              even_w_in, ret_decay_f, ret_decay_b, ret_out_norm, mla_q_norm, mla_w_qb,
              mla_kv_norm, mla_w_kvb, mla_qk_norm_q, mla_qk_norm_k, even_w_out,
              odd_w_in, swa_q_norm, swa_k_norm, swa_sink, odd_w_out)
    y_prompt = _trunk(x_prompt, *params)
    y_sample = _trunk(x_sample, *params)
    return (y_prompt, y_sample)
```

```python
import functools

import jax
import jax.numpy as jnp
import numpy as np
from jax import lax
from jax.experimental import pallas as pl
from jax.experimental.pallas import tpu as pltpu

F32 = jnp.float32
BF16 = jnp.bfloat16

D_MODEL = 1024
N_META = 16
BLOCK = 128
FRONT_PAD = BLOCK - N_META
EPS = 1e-6
NEG = -1e30
LANES = 128
HALF = LANES // 2

A_HEADS = 8
A_DK = 64
A_THETA = 10000.0
B_HEADS = 8
B_Q_LORA = 256
B_KV_LORA = 128
B_NOPE = 64
B_ROPE = 32
B_DV = 64
B_QK = B_NOPE + B_ROPE
B_THETA = 10000.0
C_HEADS = 16
C_KV_HEADS = 4
C_DH = 64
C_ROT = 16
C_THETA = 500000.0
C_WINDOW = 128

VMEM_LIMIT = 56 * 1024 * 1024


def _params(*sem):
    return pltpu.CompilerParams(dimension_semantics=sem, vmem_limit_bytes=VMEM_LIMIT)


def _resident(shape):
    zeros = (0,) * len(shape)
    return pl.BlockSpec(shape, lambda *_: zeros, pipeline_mode=pl.Buffered(1))


def _rms_rows(x, gain):
    return x * lax.rsqrt(jnp.mean(x * x, axis=-1, keepdims=True) + EPS) * gain


def _dot(a, b):
    return jnp.dot(a, b, preferred_element_type=F32)


def _dot_nt(a, b):
    return lax.dot_general(a, b, (((1,), (1,)), ((), ())), preferred_element_type=F32)


def _pick(n, options):
    for o in options:
        if n % o == 0:
            return o
    raise ValueError(f"no tile in {options} divides {n}")


def _ffn_core(x, g_ref, wg_ref, wu_ref, wd_ref):
    xn = _rms_rows(x, g_ref[...]).astype(BF16)
    acc = jnp.zeros_like(x)
    for c in range(wg_ref.shape[0]):
        gate = _dot(xn, wg_ref[c])
        up = _dot(xn, wu_ref[c])
        act = (gate * (1.0 / (1.0 + jnp.exp(-gate))) * up).astype(BF16)
        acc = acc + _dot(act, wd_ref[c])
    return x + 0.5 * acc


def _ffn_body(h_ref, g_ref, wg_ref, wu_ref, wd_ref, o_ref):
    o_ref[...] = _ffn_core(h_ref[...], g_ref, wg_ref, wu_ref, wd_ref)


def _proj_ffn_body(*refs, n_in):
    h_ref = refs[0]
    xs = refs[1:1 + n_in]
    ws = refs[1 + n_in:1 + 2 * n_in]
    g_ref, wg_ref, wu_ref, wd_ref, o_ref = refs[1 + 2 * n_in:]
    h2 = h_ref[...]
    for x_ref, w_ref in zip(xs, ws):
        h2 = h2 + _dot(x_ref[...], w_ref[...])
    o_ref[...] = _ffn_core(h2, g_ref, wg_ref, wu_ref, wd_ref)


def _ffn_specs(ffn):
    return [_resident(a.shape) for a in ffn]


def _ffn(h2d, ffn):
    t, d = h2d.shape
    tm = _pick(t, (512, 384, 256, 128))
    row = pl.BlockSpec((tm, d), lambda i: (i, 0))
    return pl.pallas_call(
        _ffn_body,
        grid=(t // tm,),
        in_specs=[row] + _ffn_specs(ffn),
        out_specs=row,
        out_shape=jax.ShapeDtypeStruct((t, d), F32),
        compiler_params=_params("parallel"),
        name="ffn",
    )(h2d, *ffn)


def _proj_ffn(h2d, xs, ws, ffn):
    t, d = h2d.shape
    tm = _pick(t, (512, 384, 256, 128))
    row = pl.BlockSpec((tm, d), lambda i: (i, 0))
    x_specs = [pl.BlockSpec((tm, x.shape[1]), lambda i: (i, 0)) for x in xs]
    w_specs = [_resident(w.shape) for w in ws]
    return pl.pallas_call(
        functools.partial(_proj_ffn_body, n_in=len(xs)),
        grid=(t // tm,),
        in_specs=[row] + x_specs + w_specs + _ffn_specs(ffn),
        out_specs=row,
        out_shape=jax.ShapeDtypeStruct((t, d), F32),
        compiler_params=_params("parallel"),
        name="proj_ffn",
    )(h2d, *xs, *ws, *ffn)


def _rope_pairs(y, cos, sin_lo, sin_hi, half):
    n = y.shape[-1]
    return y * cos + pltpu.roll(y, n - half, 1) * sin_lo + pltpu.roll(y, half, 1) * sin_hi


def _even_in_body(h_ref, g_ref, win_ref, qn_ref, wqb_ref, kvn_ref, wkb_ref, wvb_ref,
                  gq_ref, gk_ref, rtab_ref, mtab_ref,
                  rq_ref, rk_ref, rv_ref, rg_ref, mq_ref, mk_ref, mv_ref, *, tm):
    x = h_ref[0]
    hn = _rms_rows(x, g_ref[...]).astype(BF16)
    z = _dot(hn, win_ref[...])
    hw = A_HEADS * A_DK
    rcos, rlo, rhi = rtab_ref[0], rtab_ref[1], rtab_ref[2]
    for c in range(hw // LANES):
        sl = slice(c * LANES, (c + 1) * LANES)
        q = z[:, sl]
        k = z[:, hw + c * LANES:hw + (c + 1) * LANES]
        rq_ref[0, :, sl] = _rope_pairs(q, rcos, rlo, rhi, A_DK // 2).astype(BF16)
        rk_ref[0, :, sl] = (_rope_pairs(k, rcos, rlo, rhi, A_DK // 2) * (A_DK ** -0.5)).astype(BF16)
    rv_ref[0] = z[:, 2 * hw:3 * hw].astype(BF16)
    ga = z[:, 3 * hw:4 * hw]
    rg_ref[0] = (ga * (1.0 / (1.0 + jnp.exp(-ga)))).astype(BF16)

    o = 4 * hw
    cq = _rms_rows(z[:, o:o + B_Q_LORA], qn_ref[...]).astype(BF16)
    ckv = _rms_rows(z[:, o + B_Q_LORA:o + B_Q_LORA + B_KV_LORA], kvn_ref[...]).astype(BF16)
    kr = z[:, o + B_Q_LORA + B_KV_LORA:]
    q_all = _dot(cq, wqb_ref[...])
    k_all = _dot(ckv, wkb_ref[...])
    mv_ref[0] = _dot(ckv, wvb_ref[...]).astype(BF16)

    mcos, mlo, mhi = mtab_ref[0], mtab_ref[1], mtab_ref[2]
    lane = lax.broadcasted_iota(jnp.int32, (tm, LANES), 1)
    row = lax.broadcasted_iota(jnp.int32, (tm, LANES), 0) + pl.program_id(1) * tm
    bias_lane = lane == B_QK
    q_bias = jnp.where(bias_lane, 1.0, 0.0)
    k_bias = jnp.where(bias_lane & (row < FRONT_PAD), NEG, 0.0)
    inv = 1.0 / B_QK
    for hd in range(B_HEADS):
        sl = slice(hd * LANES, (hd + 1) * LANES)
        qh = q_all[:, sl]
        qh = qh * lax.rsqrt(jnp.sum(qh * qh, axis=-1, keepdims=True) * inv + EPS) * gq_ref[...]
        mq_ref[0, :, sl] = (_rope_pairs(qh, mcos, mlo, mhi, B_ROPE // 2) + q_bias).astype(BF16)
        kh = k_all[:, sl] + kr
        kh = kh * lax.rsqrt(jnp.sum(kh * kh, axis=-1, keepdims=True) * inv + EPS) * gk_ref[...]
        mk_ref[0, :, sl] = (_rope_pairs(kh, mcos, mlo, mhi, B_ROPE // 2) + k_bias).astype(BF16)


def _even_in(h, p):
    b, lp, d = h.shape
    tm = _pick(lp, (384, 128))
    tok = lambda w: pl.BlockSpec((1, tm, w), lambda i, j: (i, j, 0))
    tab = pl.BlockSpec((3, tm, LANES), lambda i, j: (0, j, 0))
    consts = [p["mix_g"], p["w_in"], p["q_norm"], p["w_qb"], p["kv_norm"], p["w_kb"], p["w_vb"],
              p["gq"], p["gk"]]
    hw = A_HEADS * A_DK
    widths = [hw, hw, hw, hw, B_HEADS * LANES, B_HEADS * LANES, B_HEADS * B_DV]
    return pl.pallas_call(
        functools.partial(_even_in_body, tm=tm),
        grid=(b, lp // tm),
        in_specs=[tok(d)] + [_resident(c.shape) for c in consts] + [tab, tab],
        out_specs=[tok(w) for w in widths],
        out_shape=[jax.ShapeDtypeStruct((b, lp, w), BF16) for w in widths],
        compiler_params=_params("parallel", "parallel"),
        name="even_in",
    )(h, *consts, p["ret_tab"], p["mla_tab"])


def _ret_body(q_ref, k_ref, v_ref, g_ref, dmat_ref, vec_ref, gain_ref, o_ref, acc_ref, *, nb):
    lane = lax.broadcasted_iota(jnp.int32, (BLOCK, LANES), 1)
    rowi = lax.broadcasted_iota(jnp.int32, (LANES, LANES), 0)
    coli = lax.broadcasted_iota(jnp.int32, (LANES, LANES), 1)
    first = lane < HALF
    same_head = (rowi < HALF) == (coli < HALF)
    wkf, wkb, qf, qb = vec_ref[0, 0], vec_ref[0, 1], vec_ref[0, 2], vec_ref[0, 3]
    cf, cb = vec_ref[0, 4][:1], vec_ref[0, 5][:1]
    d0, d1 = dmat_ref[0], dmat_ref[1]
    keep0 = jnp.where(first, 1.0, 0.0).astype(BF16)
    keep1 = jnp.where(first, 0.0, 1.0).astype(BF16)

    def kv_state(k, v, wk):
        kw = (k.astype(F32) * wk).T.astype(BF16)
        return jnp.where(same_head, _dot(kw, v), 0.0)

    def fwd(n, state):
        rows = pl.ds(pl.multiple_of(n * BLOCK, BLOCK), BLOCK)
        q, k, v = q_ref[0, rows, :], k_ref[0, rows, :], v_ref[0, rows, :]
        s0 = (_dot_nt(q * keep0, k) * d0).astype(BF16)
        s1 = (_dot_nt(q * keep1, k) * d1).astype(BF16)
        o = jnp.where(first, _dot(s0, v), _dot(s1, v))
        o = o + _dot((q.astype(F32) * qf).astype(BF16), state.astype(BF16))
        acc_ref[rows, :] = o
        return state * cf + kv_state(k, v, wkf)

    lax.fori_loop(0, nb, fwd, jnp.zeros((LANES, LANES), F32))

    def bwd(i, state):
        n = nb - 1 - i
        rows = pl.ds(pl.multiple_of(n * BLOCK, BLOCK), BLOCK)
        q, k, v = q_ref[0, rows, :], k_ref[0, rows, :], v_ref[0, rows, :]
        o = acc_ref[rows, :] + _dot((q.astype(F32) * qb).astype(BF16), state.astype(BF16))
        sq = o * o
        inv = 1.0 / HALF
        r0 = lax.rsqrt(jnp.sum(jnp.where(first, sq, 0.0), axis=-1, keepdims=True) * inv + EPS)
        r1 = lax.rsqrt(jnp.sum(jnp.where(first, 0.0, sq), axis=-1, keepdims=True) * inv + EPS)
        out = o * jnp.where(first, r0, r1) * gain_ref[0] * g_ref[0, rows, :].astype(F32)
        o_ref[0, rows, :] = out.astype(BF16)
        return state * cb + kv_state(k, v, wkb)

    lax.fori_loop(0, nb, bwd, jnp.zeros((LANES, LANES), F32))


def _retention(rq, rk, rv, rg, p):
    b, lp, hw = rq.shape
    pairs = hw // LANES
    seq = pl.BlockSpec((1, lp, LANES), lambda i, j: (i, 0, j))
    return pl.pallas_call(
        functools.partial(_ret_body, nb=lp // BLOCK),
        grid=(b, pairs),
        in_specs=[seq, seq, seq, seq,
                  pl.BlockSpec((2, BLOCK, BLOCK), lambda i, j: (j, 0, 0)),
                  pl.BlockSpec((1, 6, BLOCK, LANES), lambda i, j: (j, 0, 0, 0)),
                  pl.BlockSpec((1, 1, LANES), lambda i, j: (j, 0, 0))],
        out_specs=seq,
        out_shape=jax.ShapeDtypeStruct((b, lp, hw), BF16),
        scratch_shapes=[pltpu.VMEM((lp, LANES), F32)],
        compiler_params=_params("parallel", "parallel"),
        name="retention",
    )(rq, rk, rv, rg, p["ret_dmat"], p["ret_vec"], p["ret_gain"])


def _mla_body(q_ref, k_ref, v_ref, o_ref, *, tq, tk, nk):
    lane = lax.broadcasted_iota(jnp.int32, (tq, LANES), 1)
    row = lax.broadcasted_iota(jnp.int32, (tq, LANES), 0) + pl.program_id(2) * tq
    outs = []
    for hd in range(2):
        q = q_ref[0, :, hd * LANES:(hd + 1) * LANES]

        def step(c, carry, q=q, hd=hd):
            m, l, acc = carry
            rows = pl.ds(pl.multiple_of(c * tk, tk), tk)
            s = _dot_nt(q, k_ref[0, rows, hd * LANES:(hd + 1) * LANES])
            m_new = jnp.maximum(m, jnp.max(s, axis=-1, keepdims=True))
            alpha = jnp.exp(m - m_new)
            pr = jnp.exp(s - m_new)
            l = alpha * l + jnp.sum(pr, axis=-1, keepdims=True)
            acc = alpha * acc + _dot(pr.astype(BF16), v_ref[0, rows, :])
            return m_new, l, acc

        init = (jnp.full((tq, 1), NEG, F32), jnp.zeros((tq, 1), F32), jnp.zeros((tq, LANES), F32))
        _, l, acc = lax.fori_loop(0, nk, step, init)
        outs.append(acc / l)
    out = jnp.where(lane < HALF, outs[0], outs[1])
    o_ref[0] = jnp.where(row >= FRONT_PAD, out, 0.0).astype(BF16)


def _mla(mq, mk, mv):
    b, lp, _ = mq.shape
    pairs = B_HEADS // 2
    tq = _pick(lp, (384, 128))
    tk = tq
    return pl.pallas_call(
        functools.partial(_mla_body, tq=tq, tk=tk, nk=lp // tk),
        grid=(b, pairs, lp // tq),
        in_specs=[pl.BlockSpec((1, tq, 2 * LANES), lambda i, j, t: (i, t, j)),
                  pl.BlockSpec((1, lp, 2 * LANES), lambda i, j, t: (i, 0, j)),
                  pl.BlockSpec((1, lp, LANES), lambda i, j, t: (i, 0, j))],
        out_specs=pl.BlockSpec((1, tq, LANES), lambda i, j, t: (i, t, j)),
        out_shape=jax.ShapeDtypeStruct((b, lp, B_HEADS * B_DV), BF16),
        compiler_params=_params("parallel", "parallel", "parallel"),
        name="mla",
    )(mq, mk, mv)


def _head_rms(x, first, gain):
    sq = x * x
    inv = 1.0 / HALF
    r0 = lax.rsqrt(jnp.sum(jnp.where(first, sq, 0.0), axis=-1, keepdims=True) * inv + EPS)
    r1 = lax.rsqrt(jnp.sum(jnp.where(first, 0.0, sq), axis=-1, keepdims=True) * inv + EPS)
    return x * jnp.where(first, r0, r1) * gain


def _odd_in_body(h_ref, g_ref, win_ref, gq_ref, gk_ref, tab_ref, q_ref, k_ref, v_ref, *, tm):
    x = h_ref[0]
    hn = _rms_rows(x, g_ref[...]).astype(BF16)
    z = _dot(hn, win_ref[...])
    cos, lo, hi = tab_ref[0], tab_ref[1], tab_ref[2]
    first = lax.broadcasted_iota(jnp.int32, (tm, LANES), 1) < HALF
    nq = C_HEADS * C_DH
    nk = 2 * C_KV_HEADS * C_DH
    for c in range(nq // LANES):
        sl = slice(c * LANES, (c + 1) * LANES)
        y = _head_rms(z[:, sl], first, gq_ref[...])
        q_ref[0, :, sl] = _rope_pairs(y, cos, lo, hi, C_ROT // 2).astype(BF16)
    for c in range(nk // LANES):
        sl = slice(c * LANES, (c + 1) * LANES)
        y = _head_rms(z[:, nq + c * LANES:nq + (c + 1) * LANES], first, gk_ref[...])
        k_ref[0, :, sl] = _rope_pairs(y, cos, lo, hi, C_ROT // 2).astype(BF16)
    v_ref[0] = z[:, nq + nk:].astype(BF16)


def _odd_in(h, p):
    b, lp, d = h.shape
    tm = _pick(lp, (384, 128))
    tok = lambda w: pl.BlockSpec((1, tm, w), lambda i, j: (i, j, 0))
    tab = pl.BlockSpec((3, tm, LANES), lambda i, j: (0, j, 0))
    consts = [p["mix_g"], p["w_in"], p["gq"], p["gk"]]
    widths = [C_HEADS * C_DH, 2 * C_KV_HEADS * C_DH, 2 * C_KV_HEADS * C_DH]
    return pl.pallas_call(
        functools.partial(_odd_in_body, tm=tm),
        grid=(b, lp // tm),
        in_specs=[tok(d)] + [_resident(c.shape) for c in consts] + [tab],
        out_specs=[tok(w) for w in widths],
        out_shape=[jax.ShapeDtypeStruct((b, lp, w), BF16) for w in widths],
        compiler_params=_params("parallel", "parallel"),
        name="odd_in",
    )(h, *consts, p["swa_tab"])


def _swa_body(sink_ref, q_ref, km_ref, kp_ref, kc_ref, kn_ref, vm_ref, vp_ref, vc_ref, vn_ref,
              o_ref, *, nb):
    n = pl.program_id(1)
    lane = lax.broadcasted_iota(jnp.int32, (BLOCK, LANES), 1)
    rowi = lax.broadcasted_iota(jnp.int32, (BLOCK, LANES), 0)
    first = lane < HALF
    keep = (jnp.where(first, 1.0, 0.0).astype(BF16), jnp.where(first, 0.0, 1.0).astype(BF16))
    q_pos = n * BLOCK + rowi - FRONT_PAD
    masks = [lane >= FRONT_PAD]
    for off in (-1, 0, 1):
        k_pos = (n + off) * BLOCK + lane - FRONT_PAD
        masks.append((k_pos >= N_META) & (n + off < nb) & (jnp.abs(q_pos - k_pos) <= C_WINDOW))
    row_ok = rowi + n * BLOCK >= FRONT_PAD
    k_refs = (km_ref, kp_ref, kc_ref, kn_ref)
    v_refs = (vm_ref, vp_ref, vc_ref, vn_ref)
    for pair in range(C_HEADS // 2):
        kv = pair // (C_HEADS // C_KV_HEADS // 2)
        kvs = slice(kv * LANES, (kv + 1) * LANES)
        q2 = q_ref[0, :, pair * LANES:(pair + 1) * LANES]
        outs = []
        for which in range(2):
            qm = q2 * keep[which]
            sink = sink_ref[2 * pair + which]
            s = [jnp.where(mk, _dot_nt(qm, kr[0, :, kvs]), NEG) for mk, kr in zip(masks, k_refs)]
            m = jnp.maximum(jnp.maximum(s[0], s[1]), jnp.maximum(s[2], s[3]))
            m = jnp.maximum(jnp.max(m, axis=-1, keepdims=True), sink)
            pr = [jnp.exp(si - m) for si in s]
            l = jnp.sum(pr[0] + pr[1] + pr[2] + pr[3], axis=-1, keepdims=True) + jnp.exp(sink - m)
            o = _dot(pr[0].astype(BF16), v_refs[0][0, :, kvs])
            for j in range(1, 4):
                o = o + _dot(pr[j].astype(BF16), v_refs[j][0, :, kvs])
            outs.append(o / l)
        out = jnp.where(first, outs[0], outs[1])
        o_ref[0, :, pair * LANES:(pair + 1) * LANES] = jnp.where(row_ok, out, 0.0).astype(BF16)


def _swa(q, k, v, sink):
    b, lp, qw = q.shape
    nb = lp // BLOCK
    kw = k.shape[-1]
    qspec = pl.BlockSpec((1, BLOCK, qw), lambda i, n: (i, n, 0))
    kspecs = [pl.BlockSpec((1, BLOCK, kw), lambda i, n: (i, 0, 0)),
              pl.BlockSpec((1, BLOCK, kw), lambda i, n: (i, jnp.maximum(n - 1, 0), 0)),
              pl.BlockSpec((1, BLOCK, kw), lambda i, n: (i, n, 0)),
              pl.BlockSpec((1, BLOCK, kw), lambda i, n: (i, jnp.minimum(n + 1, nb - 1), 0))]
    return pl.pallas_call(
        functools.partial(_swa_body, nb=nb),
        grid=(b, nb),
        in_specs=[pl.BlockSpec(memory_space=pltpu.SMEM), qspec] + kspecs + kspecs,
        out_specs=qspec,
        out_shape=jax.ShapeDtypeStruct((b, lp, qw), BF16),
        compiler_params=_params("parallel", "parallel"),
        name="swa",
    )(sink, q, k, k, k, k, v, v, v, v)


def _rope_table(lp, theta, rot, group, offset, scale=1.0):
    half = rot // 2
    pos = jnp.arange(lp, dtype=F32) - FRONT_PAD
    lane = np.arange(LANES) % group - offset
    in_lo = (lane >= 0) & (lane < half)
    in_hi = (lane >= half) & (lane < rot)
    idx = np.where(in_lo | in_hi, lane % half, 0)
    inv = theta ** (-jnp.asarray(idx, F32) * 2.0 / rot)
    ang = pos[:, None] * inv[None, :]
    cos = jnp.where(in_lo | in_hi, jnp.cos(ang), 1.0)
    sin = jnp.sin(ang)
    lo = jnp.where(in_lo, -sin, 0.0)
    hi = jnp.where(in_hi, sin, 0.0)
    return jnp.stack([cos, lo, hi]) * scale


def _pair_lanes(x):
    h, n = x.shape
    return jnp.repeat(x.reshape(h // 2, 2, n).transpose(0, 2, 1), HALF, axis=2)


def _prep_ffn(gain, wg, wu, wd, fc=256):
    d, f = wg.shape
    nc = f // fc
    return (gain.reshape(1, d),
            wg.reshape(d, nc, fc).transpose(1, 0, 2).astype(BF16),
            wu.reshape(d, nc, fc).transpose(1, 0, 2).astype(BF16),
            wd.reshape(nc, fc, d).astype(BF16))


def _prep_even(lp, mix_g, w_in, dec_f, dec_b, ret_norm, q_norm, w_qb, kv_norm, w_kvb, gq, gk, w_out):
    d = w_in.shape[0]
    cut = 4 * A_HEADS * A_DK + B_Q_LORA + B_KV_LORA
    kr_cols = jnp.concatenate([jnp.zeros((d, B_NOPE), F32), w_in[:, cut:],
                               jnp.zeros((d, LANES - B_QK), F32)], axis=1)
    w_in2 = jnp.concatenate([w_in[:, :cut], kr_cols], axis=1).astype(BF16)
    w_qb2 = jnp.pad(w_qb.reshape(B_Q_LORA, B_HEADS, B_QK), ((0, 0), (0, 0), (0, LANES - B_QK)))
    w_kv3 = w_kvb.reshape(B_KV_LORA, B_HEADS, B_NOPE + B_DV)
    w_kb = jnp.pad(w_kv3[:, :, :B_NOPE], ((0, 0), (0, 0), (0, LANES - B_NOPE)))
    w_vb = w_kv3[:, :, B_NOPE:]
    pad96 = lambda g: jnp.pad(g, (0, LANES - B_QK)).reshape(1, LANES)

    lgf = -jnp.exp(dec_f.astype(F32))
    lgb = -jnp.exp(dec_b.astype(F32))
    idx = jnp.arange(BLOCK, dtype=F32)
    diff = idx[:, None] - idx[None, :]
    dmat = (jnp.where(diff >= 0, jnp.exp(lgf[:, None, None] * jnp.maximum(diff, 0.0)), 0.0)
            + jnp.where(diff < 0, jnp.exp(lgb[:, None, None] * jnp.maximum(-diff, 0.0)), 0.0))
    ones = jnp.ones((BLOCK,), F32)
    vec = jnp.stack([
        _pair_lanes(jnp.exp(lgf[:, None] * (BLOCK - 1 - idx)[None, :])),
        _pair_lanes(jnp.exp(lgb[:, None] * idx[None, :])),
        _pair_lanes(jnp.exp(lgf[:, None] * (idx + 1.0)[None, :])),
        _pair_lanes(jnp.exp(lgb[:, None] * (BLOCK - idx)[None, :])),
        _pair_lanes(jnp.exp(BLOCK * lgf)[:, None] * ones[None, :]),
        _pair_lanes(jnp.exp(BLOCK * lgb)[:, None] * ones[None, :]),
    ], axis=1)
    half = A_HEADS * A_DK
    return dict(
        mix_g=mix_g.reshape(1, d), w_in=w_in2,
        q_norm=q_norm.reshape(1, -1), w_qb=w_qb2.reshape(B_Q_LORA, -1).astype(BF16),
        kv_norm=kv_norm.reshape(1, -1), w_kb=w_kb.reshape(B_KV_LORA, -1).astype(BF16),
        w_vb=w_vb.reshape(B_KV_LORA, -1).astype(BF16),
        gq=pad96(gq) * (B_QK ** -0.5), gk=pad96(gk),
        ret_tab=_rope_table(lp, A_THETA, A_DK, A_DK, 0),
        mla_tab=_rope_table(lp, B_THETA, B_ROPE, LANES, B_NOPE),
        ret_dmat=dmat, ret_vec=vec,
        ret_gain=ret_norm.reshape(A_HEADS // 2, 1, LANES),
        w_out=(w_out[:half].astype(BF16), w_out[half:].astype(BF16)),
    )


def _prep_odd(lp, mix_g, w_in, gq, gk, sink, w_out):
    d = w_in.shape[0]
    nq = C_HEADS * C_DH
    nkv = C_KV_HEADS * C_DH
    dup = lambda w: jnp.repeat(w.reshape(d, C_KV_HEADS, 1, C_DH), 2, axis=2).reshape(d, 2 * nkv)
    w_in2 = jnp.concatenate([w_in[:, :nq], dup(w_in[:, nq:nq + nkv]), dup(w_in[:, nq + nkv:])], axis=1)
    two = lambda g: jnp.concatenate([g, g]).reshape(1, LANES)
    return dict(
        mix_g=mix_g.reshape(1, d), w_in=w_in2.astype(BF16),
        gq=two(gq) * (C_DH ** -0.5), gk=two(gk),
        swa_tab=_rope_table(lp, C_THETA, C_ROT, C_DH, 0),
        sink=sink.astype(F32), w_out=(w_out.astype(BF16),),
    )


def _trunk(x, meta, layers):
    b, seq, d = x.shape
    lp = FRONT_PAD + N_META + seq
    h = jnp.concatenate([jnp.zeros((b, FRONT_PAD, d), x.dtype),
                         jnp.broadcast_to(meta[None].astype(x.dtype), (b, N_META, d)), x], axis=1)
    h = _ffn(h.reshape(b * lp, d), layers[0]["ffn0"])
    for i, layer in enumerate(layers):
        h3 = h.reshape(b, lp, d)
        mp = layer["mix"]
        if layer["even"]:
            rq, rk, rv, rg, mq, mk, mv = _even_in(h3, mp)
            xs = [_retention(rq, rk, rv, rg, mp), _mla(mq, mk, mv)]
        else:
            q, k, v = _odd_in(h3, mp)
            xs = [_swa(q, k, v, mp["sink"])]
        xs = [a.reshape(b * lp, a.shape[-1]) for a in xs]
        h = _proj_ffn(h, xs, mp["w_out"], layer["ffn1"])
        if i + 1 < len(layers):
            h = _ffn(h, layers[i + 1]["ffn0"])
    return h.reshape(b, lp, d)[:, BLOCK:]


def kernel(x_prompt, x_sample, meta_tokens, ffn_norm, ffn_w_gate, ffn_w_up, ffn_w_down, mix_norm, even_w_in, ret_decay_f, ret_decay_b, ret_out_norm, mla_q_norm, mla_w_qb, mla_kv_norm, mla_w_kvb, mla_qk_norm_q, mla_qk_norm_k, even_w_out, odd_w_in, swa_q_norm, swa_k_norm, swa_sink, odd_w_out):
    depth = ffn_norm.shape[0]
    assert x_prompt.shape[1] == x_sample.shape[1]
    lp = FRONT_PAD + N_META + x_prompt.shape[1]
    layers = []
    for layer in range(depth):
        i = layer // 2
        ffn = [_prep_ffn(ffn_norm[layer, s], ffn_w_gate[layer, s], ffn_w_up[layer, s], ffn_w_down[layer, s])
               for s in range(2)]
        if layer % 2 == 0:
            mix = _prep_even(lp, mix_norm[layer], even_w_in[i], ret_decay_f[i], ret_decay_b[i],
                             ret_out_norm[i], mla_q_norm[i], mla_w_qb[i], mla_kv_norm[i], mla_w_kvb[i],
                             mla_qk_norm_q[i], mla_qk_norm_k[i], even_w_out[i])
        else:
            mix = _prep_odd(lp, mix_norm[layer], odd_w_in[i], swa_q_norm[i], swa_k_norm[i],
                            swa_sink[i], odd_w_out[i])
        layers.append(dict(even=layer % 2 == 0, ffn0=ffn[0], ffn1=ffn[1], mix=mix))
    return (_trunk(x_prompt, meta_tokens, layers), _trunk(x_sample, meta_tokens, layers))
```

```python
import functools

import jax
import jax.numpy as jnp
import numpy as np
from jax import lax
from jax.experimental import pallas as pl
from jax.experimental.pallas import tpu as pltpu

F32 = jnp.float32
BF16 = jnp.bfloat16

D_MODEL = 1024
N_META = 16
BLOCK = 128
FRONT_PAD = BLOCK - N_META
EPS = 1e-6
NEG = -1e30
LOG2E = 1.4426950408889634
LANES = 128
HALF = LANES // 2

A_HEADS = 8
A_DK = 64
A_THETA = 10000.0
B_HEADS = 8
B_Q_LORA = 256
B_KV_LORA = 128
B_NOPE = 64
B_ROPE = 32
B_DV = 64
B_QK = B_NOPE + B_ROPE
B_THETA = 10000.0
C_HEADS = 16
C_KV_HEADS = 4
C_DH = 64
C_ROT = 16
C_THETA = 500000.0
C_WINDOW = 128

VMEM_LIMIT = 56 * 1024 * 1024


def _params(*sem):
    return pltpu.CompilerParams(dimension_semantics=sem, vmem_limit_bytes=VMEM_LIMIT)


def _resident(shape):
    zeros = (0,) * len(shape)
    return pl.BlockSpec(shape, lambda *_: zeros, pipeline_mode=pl.Buffered(1))


def _rms_rows(x, gain):
    return x * lax.rsqrt(jnp.mean(x * x, axis=-1, keepdims=True) + EPS) * gain


def _dot(a, b):
    return jnp.dot(a, b, preferred_element_type=F32)


def _dot_nt(a, b):
    return lax.dot_general(a, b, (((1,), (1,)), ((), ())), preferred_element_type=F32)


def _pick(n, options):
    for o in options:
        if n % o == 0:
            return o
    raise ValueError(f"no tile in {options} divides {n}")


def _ffn_core(x, g_ref, wg_ref, wu_ref, wd_ref):
    xn = _rms_rows(x, g_ref[...]).astype(BF16)
    acc = jnp.zeros_like(x)
    for c in range(wg_ref.shape[0]):
        gate = _dot(xn, wg_ref[c])
        up = _dot(xn, wu_ref[c])
        act = (gate * (1.0 / (1.0 + jnp.exp(-gate))) * up).astype(BF16)
        acc = acc + _dot(act, wd_ref[c])
    return x + 0.5 * acc


def _ffn_body(h_ref, g_ref, wg_ref, wu_ref, wd_ref, o_ref):
    o_ref[...] = _ffn_core(h_ref[...], g_ref, wg_ref, wu_ref, wd_ref)


def _proj_ffn_body(*refs, n_in):
    h_ref = refs[0]
    xs = refs[1:1 + n_in]
    ws = refs[1 + n_in:1 + 2 * n_in]
    g_ref, wg_ref, wu_ref, wd_ref, o_ref = refs[1 + 2 * n_in:]
    h2 = h_ref[...]
    for x_ref, w_ref in zip(xs, ws):
        h2 = h2 + _dot(x_ref[...], w_ref[...])
    o_ref[...] = _ffn_core(h2, g_ref, wg_ref, wu_ref, wd_ref)


def _ffn_specs(ffn):
    return [_resident(a.shape) for a in ffn]


def _ffn(h2d, ffn):
    t, d = h2d.shape
    tm = _pick(t, (512, 384, 256, 128))
    row = pl.BlockSpec((tm, d), lambda i: (i, 0))
    return pl.pallas_call(
        _ffn_body,
        grid=(t // tm,),
        in_specs=[row] + _ffn_specs(ffn),
        out_specs=row,
        out_shape=jax.ShapeDtypeStruct((t, d), F32),
        compiler_params=_params("parallel"),
        name="ffn",
    )(h2d, *ffn)


def _proj_ffn(h2d, xs, ws, ffn):
    t, d = h2d.shape
    tm = _pick(t, (512, 384, 256, 128))
    row = pl.BlockSpec((tm, d), lambda i: (i, 0))
    x_specs = [pl.BlockSpec((tm, x.shape[1]), lambda i: (i, 0)) for x in xs]
    w_specs = [_resident(w.shape) for w in ws]
    return pl.pallas_call(
        functools.partial(_proj_ffn_body, n_in=len(xs)),
        grid=(t // tm,),
        in_specs=[row] + x_specs + w_specs + _ffn_specs(ffn),
        out_specs=row,
        out_shape=jax.ShapeDtypeStruct((t, d), F32),
        compiler_params=_params("parallel"),
        name="proj_ffn",
    )(h2d, *xs, *ws, *ffn)


def _rope_pairs(y, cos, sin_lo, sin_hi, half):
    n = y.shape[-1]
    return y * cos + pltpu.roll(y, n - half, 1) * sin_lo + pltpu.roll(y, half, 1) * sin_hi


def _even_in_body(h_ref, g_ref, win_ref, qn_ref, wqb_ref, kvn_ref, wkb_ref, wvb_ref,
                  gq_ref, gk_ref, rtab_ref, mtab_ref,
                  rq_ref, rk_ref, rv_ref, rg_ref, mq_ref, mk_ref, mv_ref, *, tm):
    x = h_ref[0]
    hn = _rms_rows(x, g_ref[...]).astype(BF16)
    z = _dot(hn, win_ref[...])
    hw = A_HEADS * A_DK
    rcos, rlo, rhi = rtab_ref[0], rtab_ref[1], rtab_ref[2]
    for c in range(hw // LANES):
        sl = slice(c * LANES, (c + 1) * LANES)
        q = z[:, sl]
        k = z[:, hw + c * LANES:hw + (c + 1) * LANES]
        rq_ref[0, :, sl] = _rope_pairs(q, rcos, rlo, rhi, A_DK // 2).astype(BF16)
        rk_ref[0, :, sl] = (_rope_pairs(k, rcos, rlo, rhi, A_DK // 2) * (A_DK ** -0.5)).astype(BF16)
    rv_ref[0] = z[:, 2 * hw:3 * hw].astype(BF16)
    ga = z[:, 3 * hw:4 * hw]
    rg_ref[0] = (ga * (1.0 / (1.0 + jnp.exp(-ga)))).astype(BF16)

    o = 4 * hw
    cq = _rms_rows(z[:, o:o + B_Q_LORA], qn_ref[...]).astype(BF16)
    ckv = _rms_rows(z[:, o + B_Q_LORA:o + B_Q_LORA + B_KV_LORA], kvn_ref[...]).astype(BF16)
    kr = z[:, o + B_Q_LORA + B_KV_LORA:]
    q_all = _dot(cq, wqb_ref[...])
    k_all = _dot(ckv, wkb_ref[...])
    v_all = _dot(ckv, wvb_ref[...])

    mcos, mlo, mhi = mtab_ref[0], mtab_ref[1], mtab_ref[2]
    lane = lax.broadcasted_iota(jnp.int32, (tm, LANES), 1)
    v_ones = jnp.where(lane >= HALF, 1.0, 0.0)
    row = lax.broadcasted_iota(jnp.int32, (tm, LANES), 0) + pl.program_id(1) * tm
    bias_lane = lane == B_QK
    q_bias = jnp.where(bias_lane, 1.0, 0.0)
    k_bias = jnp.where(bias_lane & (row < FRONT_PAD), NEG, 0.0)
    inv = 1.0 / B_QK
    for hd in range(B_HEADS):
        sl = slice(hd * LANES, (hd + 1) * LANES)
        qh = q_all[:, sl]
        qh = qh * lax.rsqrt(jnp.sum(qh * qh, axis=-1, keepdims=True) * inv + EPS) * gq_ref[...]
        mq_ref[0, :, sl] = (_rope_pairs(qh, mcos, mlo, mhi, B_ROPE // 2) + q_bias).astype(BF16)
        kh = k_all[:, sl] + kr
        kh = kh * lax.rsqrt(jnp.sum(kh * kh, axis=-1, keepdims=True) * inv + EPS) * gk_ref[...]
        mk_ref[0, :, sl] = (_rope_pairs(kh, mcos, mlo, mhi, B_ROPE // 2) + k_bias).astype(BF16)
        mv_ref[0, :, sl] = (v_all[:, sl] + v_ones).astype(BF16)


def _even_in(h, p):
    b, lp, d = h.shape
    tm = _pick(lp, (384, 128))
    tok = lambda w: pl.BlockSpec((1, tm, w), lambda i, j: (i, j, 0))
    tab = pl.BlockSpec((3, tm, LANES), lambda i, j: (0, j, 0))
    consts = [p["mix_g"], p["w_in"], p["q_norm"], p["w_qb"], p["kv_norm"], p["w_kb"], p["w_vb"],
              p["gq"], p["gk"]]
    hw = A_HEADS * A_DK
    widths = [hw, hw, hw, hw, B_HEADS * LANES, B_HEADS * LANES, B_HEADS * LANES]
    return pl.pallas_call(
        functools.partial(_even_in_body, tm=tm),
        grid=(b, lp // tm),
        in_specs=[tok(d)] + [_resident(c.shape) for c in consts] + [tab, tab],
        out_specs=[tok(w) for w in widths],
        out_shape=[jax.ShapeDtypeStruct((b, lp, w), BF16) for w in widths],
        compiler_params=_params("parallel", "parallel"),
        name="even_in",
    )(h, *consts, p["ret_tab"], p["mla_tab"])


def _ret_body(q_ref, k_ref, v_ref, g_ref, dmat_ref, vec_ref, gain_ref, o_ref, acc_ref, *, nb):
    lane = lax.broadcasted_iota(jnp.int32, (BLOCK, LANES), 1)
    rowi = lax.broadcasted_iota(jnp.int32, (LANES, LANES), 0)
    coli = lax.broadcasted_iota(jnp.int32, (LANES, LANES), 1)
    first = lane < HALF
    same_head = (rowi < HALF) == (coli < HALF)
    wkf, wkb, qf, qb = vec_ref[0, 0], vec_ref[0, 1], vec_ref[0, 2], vec_ref[0, 3]
    cf, cb = vec_ref[0, 4][:1], vec_ref[0, 5][:1]
    d0, d1 = dmat_ref[0], dmat_ref[1]
    keep0 = jnp.where(first, 1.0, 0.0).astype(BF16)
    keep1 = jnp.where(first, 0.0, 1.0).astype(BF16)

    def kv_state(k, v, wk):
        kw = (k.astype(F32) * wk).T.astype(BF16)
        return jnp.where(same_head, _dot(kw, v), 0.0)

    def fwd(n, state):
        rows = pl.ds(pl.multiple_of(n * BLOCK, BLOCK), BLOCK)
        q, k, v = q_ref[0, rows, :], k_ref[0, rows, :], v_ref[0, rows, :]
        s0 = (_dot_nt(q * keep0, k) * d0).astype(BF16)
        s1 = (_dot_nt(q * keep1, k) * d1).astype(BF16)
        o = jnp.where(first, _dot(s0, v), _dot(s1, v))
        o = o + _dot((q.astype(F32) * qf).astype(BF16), state.astype(BF16))
        acc_ref[rows, :] = o
        return state * cf + kv_state(k, v, wkf)

    lax.fori_loop(0, nb, fwd, jnp.zeros((LANES, LANES), F32))

    def bwd(i, state):
        n = nb - 1 - i
        rows = pl.ds(pl.multiple_of(n * BLOCK, BLOCK), BLOCK)
        q, k, v = q_ref[0, rows, :], k_ref[0, rows, :], v_ref[0, rows, :]
        o = acc_ref[rows, :] + _dot((q.astype(F32) * qb).astype(BF16), state.astype(BF16))
        sq = o * o
        inv = 1.0 / HALF
        r0 = lax.rsqrt(jnp.sum(jnp.where(first, sq, 0.0), axis=-1, keepdims=True) * inv + EPS)
        r1 = lax.rsqrt(jnp.sum(jnp.where(first, 0.0, sq), axis=-1, keepdims=True) * inv + EPS)
        out = o * jnp.where(first, r0, r1) * gain_ref[0] * g_ref[0, rows, :].astype(F32)
        o_ref[0, rows, :] = out.astype(BF16)
        return state * cb + kv_state(k, v, wkb)

    lax.fori_loop(0, nb, bwd, jnp.zeros((LANES, LANES), F32))


def _retention(rq, rk, rv, rg, p):
    b, lp, hw = rq.shape
    pairs = hw // LANES
    seq = pl.BlockSpec((1, lp, LANES), lambda i, j: (i, 0, j))
    return pl.pallas_call(
        functools.partial(_ret_body, nb=lp // BLOCK),
        grid=(b, pairs),
        in_specs=[seq, seq, seq, seq,
                  pl.BlockSpec((2, BLOCK, BLOCK), lambda i, j: (j, 0, 0)),
                  pl.BlockSpec((1, 6, BLOCK, LANES), lambda i, j: (j, 0, 0, 0)),
                  pl.BlockSpec((1, 1, LANES), lambda i, j: (j, 0, 0))],
        out_specs=seq,
        out_shape=jax.ShapeDtypeStruct((b, lp, hw), BF16),
        scratch_shapes=[pltpu.VMEM((lp, LANES), F32)],
        compiler_params=_params("parallel", "parallel"),
        name="retention",
    )(rq, rk, rv, rg, p["ret_dmat"], p["ret_vec"], p["ret_gain"])


MXU_TILE = 256
MLA_ROWS = 16


def _mla_body(q_ref, k_ref, v_ref, o_ref, s_scr, p_scr, o_scr, *, tq):
    lp = k_ref.shape[1]
    chunks = [(c, min(MXU_TILE, lp - c)) for c in range(0, lp, MXU_TILE)]
    hk = lp // 2 // MXU_TILE * MXU_TILE

    def scores(hd):
        q = q_ref[0, :, hd * LANES:(hd + 1) * LANES]
        for c, w in chunks:
            s_scr[hd, :, c:c + w] = _dot_nt(q, k_ref[0, c:c + w, hd * LANES:(hd + 1) * LANES])

    def probs(hd):
        for r in range(0, tq, MLA_ROWS):
            sb = s_scr[hd, r:r + MLA_ROWS, :]
            m = jnp.max(sb, axis=-1, keepdims=True)
            p_scr[hd, r:r + MLA_ROWS, :] = jnp.exp2(sb - m).astype(BF16)

    def weighted(hd):
        v = v_ref[0, :, hd * LANES:(hd + 1) * LANES]
        if hk == 0:
            return _dot(p_scr[hd], v)
        return _dot(p_scr[hd, :, :hk], v[:hk]) + _dot(p_scr[hd, :, hk:], v[hk:])

    scores(0)
    probs(0)
    scores(1)
    o_scr[...] = weighted(0)
    probs(1)
    oa = o_scr[...]
    ob = weighted(1)
    lane = lax.broadcasted_iota(jnp.int32, (tq, LANES), 1)
    row = lax.broadcasted_iota(jnp.int32, (tq, LANES), 0) + pl.program_id(2) * tq
    out = jnp.where(lane < HALF, oa * pltpu.roll(1.0 / oa, HALF, 1),
                    pltpu.roll(ob, HALF, 1) * (1.0 / ob))
    o_ref[0] = jnp.where(row >= FRONT_PAD, out, 0.0).astype(BF16)


def _mla(mq, mk, mv):
    b, lp, _ = mq.shape
    pairs = B_HEADS // 2
    tq = _pick(lp, (384, 128))
    return pl.pallas_call(
        functools.partial(_mla_body, tq=tq),
        grid=(b, pairs, lp // tq),
        in_specs=[pl.BlockSpec((1, tq, 2 * LANES), lambda i, j, t: (i, t, j)),
                  pl.BlockSpec((1, lp, 2 * LANES), lambda i, j, t: (i, 0, j)),
                  pl.BlockSpec((1, lp, 2 * LANES), lambda i, j, t: (i, 0, j))],
        out_specs=pl.BlockSpec((1, tq, LANES), lambda i, j, t: (i, t, j)),
        out_shape=jax.ShapeDtypeStruct((b, lp, B_HEADS * B_DV), BF16),
        scratch_shapes=[pltpu.VMEM((2, tq, lp), F32), pltpu.VMEM((2, tq, lp), BF16),
                        pltpu.VMEM((tq, LANES), F32)],
        compiler_params=_params("parallel", "parallel", "parallel"),
        name="mla",
    )(mq, mk, mv)


def _head_rms(x, first, gain):
    sq = x * x
    inv = 1.0 / HALF
    r0 = lax.rsqrt(jnp.sum(jnp.where(first, sq, 0.0), axis=-1, keepdims=True) * inv + EPS)
    r1 = lax.rsqrt(jnp.sum(jnp.where(first, 0.0, sq), axis=-1, keepdims=True) * inv + EPS)
    return x * jnp.where(first, r0, r1) * gain


def _odd_in_body(h_ref, g_ref, win_ref, gq_ref, gk_ref, tab_ref, q_ref, k_ref, v_ref, *, tm):
    x = h_ref[0]
    hn = _rms_rows(x, g_ref[...]).astype(BF16)
    z = _dot(hn, win_ref[...])
    cos, lo, hi = tab_ref[0], tab_ref[1], tab_ref[2]
    first = lax.broadcasted_iota(jnp.int32, (tm, LANES), 1) < HALF
    nq = C_HEADS * C_DH
    nk = 2 * C_KV_HEADS * C_DH
    for c in range(nq // LANES):
        sl = slice(c * LANES, (c + 1) * LANES)
        y = _head_rms(z[:, sl], first, gq_ref[...])
        q_ref[0, :, sl] = _rope_pairs(y, cos, lo, hi, C_ROT // 2).astype(BF16)
    for c in range(nk // LANES):
        sl = slice(c * LANES, (c + 1) * LANES)
        y = _head_rms(z[:, nq + c * LANES:nq + (c + 1) * LANES], first, gk_ref[...])
        k_ref[0, :, sl] = _rope_pairs(y, cos, lo, hi, C_ROT // 2).astype(BF16)
    v_ref[0] = z[:, nq + nk:].astype(BF16)


def _odd_in(h, p):
    b, lp, d = h.shape
    tm = _pick(lp, (384, 128))
    tok = lambda w: pl.BlockSpec((1, tm, w), lambda i, j: (i, j, 0))
    tab = pl.BlockSpec((3, tm, LANES), lambda i, j: (0, j, 0))
    consts = [p["mix_g"], p["w_in"], p["gq"], p["gk"]]
    widths = [C_HEADS * C_DH, 2 * C_KV_HEADS * C_DH, 2 * C_KV_HEADS * C_DH]
    return pl.pallas_call(
        functools.partial(_odd_in_body, tm=tm),
        grid=(b, lp // tm),
        in_specs=[tok(d)] + [_resident(c.shape) for c in consts] + [tab],
        out_specs=[tok(w) for w in widths],
        out_shape=[jax.ShapeDtypeStruct((b, lp, w), BF16) for w in widths],
        compiler_params=_params("parallel", "parallel"),
        name="odd_in",
    )(h, *consts, p["swa_tab"])


def _swa_body(sink_ref, q_ref, km_ref, kp_ref, kc_ref, kn_ref, vm_ref, vp_ref, vc_ref, vn_ref,
              o_ref, *, nb):
    n = pl.program_id(1)
    lane = lax.broadcasted_iota(jnp.int32, (BLOCK, LANES), 1)
    rowi = lax.broadcasted_iota(jnp.int32, (BLOCK, LANES), 0)
    first = lane < HALF
    keep = (jnp.where(first, 1.0, 0.0).astype(BF16), jnp.where(first, 0.0, 1.0).astype(BF16))
    q_pos = n * BLOCK + rowi - FRONT_PAD
    masks = [lane >= FRONT_PAD]
    for off in (-1, 0, 1):
        k_pos = (n + off) * BLOCK + lane - FRONT_PAD
        masks.append((k_pos >= N_META) & (n + off < nb) & (jnp.abs(q_pos - k_pos) <= C_WINDOW))
    row_ok = rowi + n * BLOCK >= FRONT_PAD
    k_refs = (km_ref, kp_ref, kc_ref, kn_ref)
    v_refs = (vm_ref, vp_ref, vc_ref, vn_ref)
    for pair in range(C_HEADS // 2):
        kv = pair // (C_HEADS // C_KV_HEADS // 2)
        kvs = slice(kv * LANES, (kv + 1) * LANES)
        q2 = q_ref[0, :, pair * LANES:(pair + 1) * LANES]
        outs = []
        for which in range(2):
            qm = q2 * keep[which]
            sink = sink_ref[2 * pair + which]
            s = [jnp.where(mk, _dot_nt(qm, kr[0, :, kvs]), NEG) for mk, kr in zip(masks, k_refs)]
            m = jnp.maximum(jnp.maximum(s[0], s[1]), jnp.maximum(s[2], s[3]))
            m = jnp.maximum(jnp.max(m, axis=-1, keepdims=True), sink)
            pr = [jnp.exp(si - m) for si in s]
            l = jnp.sum(pr[0] + pr[1] + pr[2] + pr[3], axis=-1, keepdims=True) + jnp.exp(sink - m)
            o = _dot(pr[0].astype(BF16), v_refs[0][0, :, kvs])
            for j in range(1, 4):
                o = o + _dot(pr[j].astype(BF16), v_refs[j][0, :, kvs])
            outs.append(o / l)
        out = jnp.where(first, outs[0], outs[1])
        o_ref[0, :, pair * LANES:(pair + 1) * LANES] = jnp.where(row_ok, out, 0.0).astype(BF16)


def _swa(q, k, v, sink):
    b, lp, qw = q.shape
    nb = lp // BLOCK
    kw = k.shape[-1]
    qspec = pl.BlockSpec((1, BLOCK, qw), lambda i, n: (i, n, 0))
    kspecs = [pl.BlockSpec((1, BLOCK, kw), lambda i, n: (i, 0, 0)),
              pl.BlockSpec((1, BLOCK, kw), lambda i, n: (i, jnp.maximum(n - 1, 0), 0)),
              pl.BlockSpec((1, BLOCK, kw), lambda i, n: (i, n, 0)),
              pl.BlockSpec((1, BLOCK, kw), lambda i, n: (i, jnp.minimum(n + 1, nb - 1), 0))]
    return pl.pallas_call(
        functools.partial(_swa_body, nb=nb),
        grid=(b, nb),
        in_specs=[pl.BlockSpec(memory_space=pltpu.SMEM), qspec] + kspecs + kspecs,
        out_specs=qspec,
        out_shape=jax.ShapeDtypeStruct((b, lp, qw), BF16),
        compiler_params=_params("parallel", "parallel"),
        name="swa",
    )(sink, q, k, k, k, k, v, v, v, v)


def _rope_table(lp, theta, rot, group, offset, scale=1.0):
    half = rot // 2
    pos = jnp.arange(lp, dtype=F32) - FRONT_PAD
    lane = np.arange(LANES) % group - offset
    in_lo = (lane >= 0) & (lane < half)
    in_hi = (lane >= half) & (lane < rot)
    idx = np.where(in_lo | in_hi, lane % half, 0)
    inv = theta ** (-jnp.asarray(idx, F32) * 2.0 / rot)
    ang = pos[:, None] * inv[None, :]
    cos = jnp.where(in_lo | in_hi, jnp.cos(ang), 1.0)
    sin = jnp.sin(ang)
    lo = jnp.where(in_lo, -sin, 0.0)
    hi = jnp.where(in_hi, sin, 0.0)
    return jnp.stack([cos, lo, hi]) * scale


def _pair_lanes(x):
    h, n = x.shape
    return jnp.repeat(x.reshape(h // 2, 2, n).transpose(0, 2, 1), HALF, axis=2)


def _prep_ffn(gain, wg, wu, wd, fc=256):
    d, f = wg.shape
    nc = f // fc
    return (gain.reshape(1, d),
            wg.reshape(d, nc, fc).transpose(1, 0, 2).astype(BF16),
            wu.reshape(d, nc, fc).transpose(1, 0, 2).astype(BF16),
            wd.reshape(nc, fc, d).astype(BF16))


def _prep_even(lp, mix_g, w_in, dec_f, dec_b, ret_norm, q_norm, w_qb, kv_norm, w_kvb, gq, gk, w_out):
    d = w_in.shape[0]
    cut = 4 * A_HEADS * A_DK + B_Q_LORA + B_KV_LORA
    kr_cols = jnp.concatenate([jnp.zeros((d, B_NOPE), F32), w_in[:, cut:],
                               jnp.zeros((d, LANES - B_QK), F32)], axis=1)
    w_in2 = jnp.concatenate([w_in[:, :cut], kr_cols], axis=1).astype(BF16)
    w_qb2 = jnp.pad(w_qb.reshape(B_Q_LORA, B_HEADS, B_QK), ((0, 0), (0, 0), (0, LANES - B_QK)))
    w_kv3 = w_kvb.reshape(B_KV_LORA, B_HEADS, B_NOPE + B_DV)
    w_kb = jnp.pad(w_kv3[:, :, :B_NOPE], ((0, 0), (0, 0), (0, LANES - B_NOPE)))
    w_vb = jnp.pad(w_kv3[:, :, B_NOPE:], ((0, 0), (0, 0), (0, LANES - B_DV)))
    pad96 = lambda g: jnp.pad(g, (0, LANES - B_QK)).reshape(1, LANES)

    lgf = -jnp.exp(dec_f.astype(F32))
    lgb = -jnp.exp(dec_b.astype(F32))
    idx = jnp.arange(BLOCK, dtype=F32)
    diff = idx[:, None] - idx[None, :]
    dmat = (jnp.where(diff >= 0, jnp.exp(lgf[:, None, None] * jnp.maximum(diff, 0.0)), 0.0)
            + jnp.where(diff < 0, jnp.exp(lgb[:, None, None] * jnp.maximum(-diff, 0.0)), 0.0))
    ones = jnp.ones((BLOCK,), F32)
    vec = jnp.stack([
        _pair_lanes(jnp.exp(lgf[:, None] * (BLOCK - 1 - idx)[None, :])),
        _pair_lanes(jnp.exp(lgb[:, None] * idx[None, :])),
        _pair_lanes(jnp.exp(lgf[:, None] * (idx + 1.0)[None, :])),
        _pair_lanes(jnp.exp(lgb[:, None] * (BLOCK - idx)[None, :])),
        _pair_lanes(jnp.exp(BLOCK * lgf)[:, None] * ones[None, :]),
        _pair_lanes(jnp.exp(BLOCK * lgb)[:, None] * ones[None, :]),
    ], axis=1)
    half = A_HEADS * A_DK
    return dict(
        mix_g=mix_g.reshape(1, d), w_in=w_in2,
        q_norm=q_norm.reshape(1, -1), w_qb=w_qb2.reshape(B_Q_LORA, -1).astype(BF16),
        kv_norm=kv_norm.reshape(1, -1), w_kb=w_kb.reshape(B_KV_LORA, -1).astype(BF16),
        w_vb=w_vb.reshape(B_KV_LORA, -1).astype(BF16),
        gq=pad96(gq) * (B_QK ** -0.5 * LOG2E), gk=pad96(gk),
        ret_tab=_rope_table(lp, A_THETA, A_DK, A_DK, 0),
        mla_tab=_rope_table(lp, B_THETA, B_ROPE, LANES, B_NOPE),
        ret_dmat=dmat, ret_vec=vec,
        ret_gain=ret_norm.reshape(A_HEADS // 2, 1, LANES),
        w_out=(w_out[:half].astype(BF16), w_out[half:].astype(BF16)),
    )


def _prep_odd(lp, mix_g, w_in, gq, gk, sink, w_out):
    d = w_in.shape[0]
    nq = C_HEADS * C_DH
    nkv = C_KV_HEADS * C_DH
    dup = lambda w: jnp.repeat(w.reshape(d, C_KV_HEADS, 1, C_DH), 2, axis=2).reshape(d, 2 * nkv)
    w_in2 = jnp.concatenate([w_in[:, :nq], dup(w_in[:, nq:nq + nkv]), dup(w_in[:, nq + nkv:])], axis=1)
    two = lambda g: jnp.concatenate([g, g]).reshape(1, LANES)
    return dict(
        mix_g=mix_g.reshape(1, d), w_in=w_in2.astype(BF16),
        gq=two(gq) * (C_DH ** -0.5), gk=two(gk),
        swa_tab=_rope_table(lp, C_THETA, C_ROT, C_DH, 0),
        sink=sink.astype(F32), w_out=(w_out.astype(BF16),),
    )


def _trunk(x, meta, layers):
    b, seq, d = x.shape
    lp = FRONT_PAD + N_META + seq
    h = jnp.concatenate([jnp.zeros((b, FRONT_PAD, d), x.dtype),
                         jnp.broadcast_to(meta[None].astype(x.dtype), (b, N_META, d)), x], axis=1)
    h = _ffn(h.reshape(b * lp, d), layers[0]["ffn0"])
    for i, layer in enumerate(layers):
        h3 = h.reshape(b, lp, d)
        mp = layer["mix"]
        if layer["even"]:
            rq, rk, rv, rg, mq, mk, mv = _even_in(h3, mp)
            xs = [_retention(rq, rk, rv, rg, mp), _mla(mq, mk, mv)]
        else:
            q, k, v = _odd_in(h3, mp)
            xs = [_swa(q, k, v, mp["sink"])]
        xs = [a.reshape(b * lp, a.shape[-1]) for a in xs]
        h = _proj_ffn(h, xs, mp["w_out"], layer["ffn1"])
        if i + 1 < len(layers):
            h = _ffn(h, layers[i + 1]["ffn0"])
    return h.reshape(b, lp, d)[:, BLOCK:]


def kernel(x_prompt, x_sample, meta_tokens, ffn_norm, ffn_w_gate, ffn_w_up, ffn_w_down, mix_norm, even_w_in, ret_decay_f, ret_decay_b, ret_out_norm, mla_q_norm, mla_w_qb, mla_kv_norm, mla_w_kvb, mla_qk_norm_q, mla_qk_norm_k, even_w_out, odd_w_in, swa_q_norm, swa_k_norm, swa_sink, odd_w_out):
    depth = ffn_norm.shape[0]
    assert x_prompt.shape[1] == x_sample.shape[1]
    lp = FRONT_PAD + N_META + x_prompt.shape[1]
    layers = []
    for layer in range(depth):
        i = layer // 2
        ffn = [_prep_ffn(ffn_norm[layer, s], ffn_w_gate[layer, s], ffn_w_up[layer, s], ffn_w_down[layer, s])
               for s in range(2)]
        if layer % 2 == 0:
            mix = _prep_even(lp, mix_norm[layer], even_w_in[i], ret_decay_f[i], ret_decay_b[i],
                             ret_out_norm[i], mla_q_norm[i], mla_w_qb[i], mla_kv_norm[i], mla_w_kvb[i],
                             mla_qk_norm_q[i], mla_qk_norm_k[i], even_w_out[i])
        else:
            mix = _prep_odd(lp, mix_norm[layer], odd_w_in[i], swa_q_norm[i], swa_k_norm[i],
                            swa_sink[i], odd_w_out[i])
        layers.append(dict(even=layer % 2 == 0, ffn0=ffn[0], ffn1=ffn[1], mix=mix))
    return (_trunk(x_prompt, meta_tokens, layers), _trunk(x_sample, meta_tokens, layers))
```

```python
import functools

import jax
import jax.numpy as jnp
import numpy as np
from jax import lax
from jax.experimental import pallas as pl
from jax.experimental.pallas import tpu as pltpu

F32 = jnp.float32
BF16 = jnp.bfloat16

D_MODEL = 1024
N_META = 16
BLOCK = 128
FRONT_PAD = BLOCK - N_META
EPS = 1e-6
NEG = -1e30
LOG2E = 1.4426950408889634
LANES = 128
HALF = LANES // 2

A_HEADS = 8
A_DK = 64
A_THETA = 10000.0
B_HEADS = 8
B_Q_LORA = 256
B_KV_LORA = 128
B_NOPE = 64
B_ROPE = 32
B_DV = 64
B_QK = B_NOPE + B_ROPE
B_THETA = 10000.0
C_HEADS = 16
C_KV_HEADS = 4
C_DH = 64
C_ROT = 16
C_THETA = 500000.0
C_WINDOW = 128

VMEM_LIMIT = 56 * 1024 * 1024


def _params(*sem):
    return pltpu.CompilerParams(dimension_semantics=sem, vmem_limit_bytes=VMEM_LIMIT)


def _resident(shape):
    zeros = (0,) * len(shape)
    return pl.BlockSpec(shape, lambda *_: zeros, pipeline_mode=pl.Buffered(1))


def _rms_rows(x, gain):
    return x * lax.rsqrt(jnp.mean(x * x, axis=-1, keepdims=True) + EPS) * gain


def _dot(a, b):
    return jnp.dot(a, b, preferred_element_type=F32)


def _dot_nt(a, b):
    return lax.dot_general(a, b, (((1,), (1,)), ((), ())), preferred_element_type=F32)


def _pick(n, options):
    for o in options:
        if n % o == 0:
            return o
    raise ValueError(f"no tile in {options} divides {n}")


def _ffn_core(x, g_ref, wg_ref, wu_ref, wd_ref):
    xn = _rms_rows(x, g_ref[...]).astype(BF16)
    acc = jnp.zeros_like(x)
    for c in range(wg_ref.shape[0]):
        gate = _dot(xn, wg_ref[c])
        up = _dot(xn, wu_ref[c])
        act = (gate * (1.0 / (1.0 + jnp.exp(-gate))) * up).astype(BF16)
        acc = acc + _dot(act, wd_ref[c])
    return x + 0.5 * acc


def _ffn_body(h_ref, g_ref, wg_ref, wu_ref, wd_ref, o_ref):
    o_ref[...] = _ffn_core(h_ref[...], g_ref, wg_ref, wu_ref, wd_ref)


def _proj_ffn_body(*refs, n_in):
    h_ref = refs[0]
    xs = refs[1:1 + n_in]
    ws = refs[1 + n_in:1 + 2 * n_in]
    g_ref, wg_ref, wu_ref, wd_ref, o_ref = refs[1 + 2 * n_in:]
    h2 = h_ref[...]
    for x_ref, w_ref in zip(xs, ws):
        h2 = h2 + _dot(x_ref[...], w_ref[...])
    o_ref[...] = _ffn_core(h2, g_ref, wg_ref, wu_ref, wd_ref)


def _ffn_specs(ffn):
    return [_resident(a.shape) for a in ffn]


def _ffn(h2d, ffn):
    t, d = h2d.shape
    tm = _pick(t, (512, 384, 256, 128))
    row = pl.BlockSpec((tm, d), lambda i: (i, 0))
    return pl.pallas_call(
        _ffn_body,
        grid=(t // tm,),
        in_specs=[row] + _ffn_specs(ffn),
        out_specs=row,
        out_shape=jax.ShapeDtypeStruct((t, d), F32),
        compiler_params=_params("parallel"),
        name="ffn",
    )(h2d, *ffn)


def _proj_ffn(h2d, xs, ws, ffn):
    t, d = h2d.shape
    tm = _pick(t, (512, 384, 256, 128))
    row = pl.BlockSpec((tm, d), lambda i: (i, 0))
    x_specs = [pl.BlockSpec((tm, x.shape[1]), lambda i: (i, 0)) for x in xs]
    w_specs = [_resident(w.shape) for w in ws]
    return pl.pallas_call(
        functools.partial(_proj_ffn_body, n_in=len(xs)),
        grid=(t // tm,),
        in_specs=[row] + x_specs + w_specs + _ffn_specs(ffn),
        out_specs=row,
        out_shape=jax.ShapeDtypeStruct((t, d), F32),
        compiler_params=_params("parallel"),
        name="proj_ffn",
    )(h2d, *xs, *ws, *ffn)


def _rope_pairs(y, cos, sin_lo, sin_hi, half):
    n = y.shape[-1]
    return y * cos + pltpu.roll(y, n - half, 1) * sin_lo + pltpu.roll(y, half, 1) * sin_hi


def _even_in_body(h_ref, g_ref, win_ref, qn_ref, wqb_ref, kvn_ref, wkb_ref, wvb_ref,
                  gq_ref, gk_ref, rtab_ref, mtab_ref,
                  rq_ref, rk_ref, rv_ref, rg_ref, mq_ref, mk_ref, mv_ref, *, tm):
    x = h_ref[0]
    hn = _rms_rows(x, g_ref[...]).astype(BF16)
    z = _dot(hn, win_ref[...])
    hw = A_HEADS * A_DK
    rcos, rlo, rhi = rtab_ref[0], rtab_ref[1], rtab_ref[2]
    for c in range(hw // LANES):
        sl = slice(c * LANES, (c + 1) * LANES)
        q = z[:, sl]
        k = z[:, hw + c * LANES:hw + (c + 1) * LANES]
        rq_ref[0, :, sl] = _rope_pairs(q, rcos, rlo, rhi, A_DK // 2).astype(BF16)
        rk_ref[0, :, sl] = (_rope_pairs(k, rcos, rlo, rhi, A_DK // 2) * (A_DK ** -0.5)).astype(BF16)
    rv_ref[0] = z[:, 2 * hw:3 * hw].astype(BF16)
    ga = z[:, 3 * hw:4 * hw]
    rg_ref[0] = (ga * (1.0 / (1.0 + jnp.exp(-ga)))).astype(BF16)

    o = 4 * hw
    cq = _rms_rows(z[:, o:o + B_Q_LORA], qn_ref[...]).astype(BF16)
    ckv = _rms_rows(z[:, o + B_Q_LORA:o + B_Q_LORA + B_KV_LORA], kvn_ref[...]).astype(BF16)
    kr = z[:, o + B_Q_LORA + B_KV_LORA:]
    q_all = _dot(cq, wqb_ref[...])
    k_all = _dot(ckv, wkb_ref[...])
    v_all = _dot(ckv, wvb_ref[...])

    mcos, mlo, mhi = mtab_ref[0], mtab_ref[1], mtab_ref[2]
    lane = lax.broadcasted_iota(jnp.int32, (tm, LANES), 1)
    v_ones = jnp.where(lane >= HALF, 1.0, 0.0)
    row = lax.broadcasted_iota(jnp.int32, (tm, LANES), 0) + pl.program_id(1) * tm
    bias_lane = lane == B_QK
    q_bias = jnp.where(bias_lane, 1.0, 0.0)
    k_bias = jnp.where(bias_lane & (row < FRONT_PAD), NEG, 0.0)
    inv = 1.0 / B_QK
    for hd in range(B_HEADS):
        sl = slice(hd * LANES, (hd + 1) * LANES)
        qh = q_all[:, sl]
        qh = qh * lax.rsqrt(jnp.sum(qh * qh, axis=-1, keepdims=True) * inv + EPS) * gq_ref[...]
        mq_ref[0, :, sl] = (_rope_pairs(qh, mcos, mlo, mhi, B_ROPE // 2) + q_bias).astype(BF16)
        kh = k_all[:, sl] + kr
        kh = kh * lax.rsqrt(jnp.sum(kh * kh, axis=-1, keepdims=True) * inv + EPS) * gk_ref[...]
        mk_ref[0, :, sl] = (_rope_pairs(kh, mcos, mlo, mhi, B_ROPE // 2) + k_bias).astype(BF16)
        mv_ref[0, :, sl] = (v_all[:, sl] + v_ones).astype(BF16)


def _even_in(h, p):
    b, lp, d = h.shape
    tm = _pick(lp, (384, 128))
    tok = lambda w: pl.BlockSpec((1, tm, w), lambda i, j: (i, j, 0))
    tab = pl.BlockSpec((3, tm, LANES), lambda i, j: (0, j, 0))
    consts = [p["mix_g"], p["w_in"], p["q_norm"], p["w_qb"], p["kv_norm"], p["w_kb"], p["w_vb"],
              p["gq"], p["gk"]]
    hw = A_HEADS * A_DK
    widths = [hw, hw, hw, hw, B_HEADS * LANES, B_HEADS * LANES, B_HEADS * LANES]
    return pl.pallas_call(
        functools.partial(_even_in_body, tm=tm),
        grid=(b, lp // tm),
        in_specs=[tok(d)] + [_resident(c.shape) for c in consts] + [tab, tab],
        out_specs=[tok(w) for w in widths],
        out_shape=[jax.ShapeDtypeStruct((b, lp, w), BF16) for w in widths],
        compiler_params=_params("parallel", "parallel"),
        name="even_in",
    )(h, *consts, p["ret_tab"], p["mla_tab"])


def _ret_body(q_ref, k_ref, v_ref, g_ref, dmat_ref, vec_ref, gain_ref, o_ref, acc_ref, kv_ref, st_ref,
              *, nb):
    lane = lax.broadcasted_iota(jnp.int32, (BLOCK, LANES), 1)
    first = lane < HALF
    wkf, wkb, qf, qb = vec_ref[0, 0], vec_ref[0, 1], vec_ref[0, 2], vec_ref[0, 3]
    cf, cb = vec_ref[0, 4][:1], vec_ref[0, 5][:1]
    dcat = jnp.concatenate([dmat_ref[0], dmat_ref[1]], axis=0)
    keep0 = jnp.where(first, 1.0, 0.0).astype(BF16)
    keep1 = jnp.where(first, 0.0, 1.0).astype(BF16)
    row2 = lax.broadcasted_iota(jnp.int32, (2 * LANES, LANES), 0) % LANES
    col2 = lax.broadcasted_iota(jnp.int32, (2 * LANES, LANES), 1)
    same2 = (row2 < HALF) == (col2 < HALF)
    unroll = next(u for u in (11, 3, 1) if nb % u == 0)

    def intra(n, carry):
        rows = pl.ds(pl.multiple_of(n * BLOCK, BLOCK), BLOCK)
        q, k, v = q_ref[0, rows, :], k_ref[0, rows, :], v_ref[0, rows, :]
        qs = jnp.concatenate([q * keep0, q * keep1], axis=0)
        s = (_dot_nt(qs, k) * dcat).astype(BF16)
        o2 = _dot(s, v)
        acc_ref[rows, :] = jnp.where(first, o2[:BLOCK], o2[BLOCK:])
        kf = k.astype(F32)
        kw = jnp.concatenate([kf * wkf, kf * wkb], axis=1).T.astype(BF16)
        kv_ref[n] = jnp.where(same2, _dot(kw, v), 0.0)
        return carry

    lax.fori_loop(0, nb, intra, 0, unroll=unroll)

    def scan(i, carry):
        sf, sb = carry
        j = nb - 1 - i
        st_ref[i, :LANES, :] = sf.astype(BF16)
        st_ref[j, LANES:, :] = sb.astype(BF16)
        return sf * cf + kv_ref[i, :LANES, :], sb * cb + kv_ref[j, LANES:, :]

    zero = jnp.zeros((LANES, LANES), F32)
    lax.fori_loop(0, nb, scan, (zero, zero))

    def inter(n, carry):
        rows = pl.ds(pl.multiple_of(n * BLOCK, BLOCK), BLOCK)
        qf32 = q_ref[0, rows, :].astype(F32)
        qq = jnp.concatenate([qf32 * qf, qf32 * qb], axis=1).astype(BF16)
        o = acc_ref[rows, :] + _dot(qq, st_ref[n])
        sq = o * o
        inv = 1.0 / HALF
        r0 = lax.rsqrt(jnp.sum(jnp.where(first, sq, 0.0), axis=-1, keepdims=True) * inv + EPS)
        r1 = lax.rsqrt(jnp.sum(jnp.where(first, 0.0, sq), axis=-1, keepdims=True) * inv + EPS)
        out = o * jnp.where(first, r0, r1) * gain_ref[0] * g_ref[0, rows, :].astype(F32)
        o_ref[0, rows, :] = out.astype(BF16)
        return carry

    lax.fori_loop(0, nb, inter, 0, unroll=unroll)


def _retention(rq, rk, rv, rg, p):
    b, lp, hw = rq.shape
    pairs = hw // LANES
    nb = lp // BLOCK
    seq = pl.BlockSpec((1, lp, LANES), lambda i, j: (i, 0, j))
    return pl.pallas_call(
        functools.partial(_ret_body, nb=nb),
        grid=(b, pairs),
        in_specs=[seq, seq, seq, seq,
                  pl.BlockSpec((2, BLOCK, BLOCK), lambda i, j: (j, 0, 0)),
                  pl.BlockSpec((1, 6, BLOCK, LANES), lambda i, j: (j, 0, 0, 0)),
                  pl.BlockSpec((1, 1, LANES), lambda i, j: (j, 0, 0))],
        out_specs=seq,
        out_shape=jax.ShapeDtypeStruct((b, lp, hw), BF16),
        scratch_shapes=[pltpu.VMEM((lp, LANES), F32), pltpu.VMEM((nb, 2 * LANES, LANES), F32),
                        pltpu.VMEM((nb, 2 * LANES, LANES), BF16)],
        compiler_params=_params("parallel", "parallel"),
        name="retention",
    )(rq, rk, rv, rg, p["ret_dmat"], p["ret_vec"], p["ret_gain"])


MXU_TILE = 256
MLA_ROWS = 16


def _mla_body(q_ref, k_ref, v_ref, o_ref, s_scr, p_scr, o_scr, *, tq):
    lp = k_ref.shape[1]
    chunks = [(c, min(MXU_TILE, lp - c)) for c in range(0, lp, MXU_TILE)]
    hk = lp // 2 // MXU_TILE * MXU_TILE

    def scores(hd):
        q = q_ref[0, :, hd * LANES:(hd + 1) * LANES]
        for c, w in chunks:
            s_scr[hd, :, c:c + w] = _dot_nt(q, k_ref[0, c:c + w, hd * LANES:(hd + 1) * LANES])

    def probs(hd):
        for r in range(0, tq, MLA_ROWS):
            sb = s_scr[hd, r:r + MLA_ROWS, :]
            m = jnp.max(sb, axis=-1, keepdims=True)
            p_scr[hd, r:r + MLA_ROWS, :] = jnp.exp2(sb - m).astype(BF16)

    def weighted(hd):
        v = v_ref[0, :, hd * LANES:(hd + 1) * LANES]
        if hk == 0:
            return _dot(p_scr[hd], v)
        return _dot(p_scr[hd, :, :hk], v[:hk]) + _dot(p_scr[hd, :, hk:], v[hk:])

    scores(0)
    probs(0)
    scores(1)
    o_scr[...] = weighted(0)
    probs(1)
    oa = o_scr[...]
    ob = weighted(1)
    lane = lax.broadcasted_iota(jnp.int32, (tq, LANES), 1)
    row = lax.broadcasted_iota(jnp.int32, (tq, LANES), 0) + pl.program_id(2) * tq
    out = jnp.where(lane < HALF, oa * pltpu.roll(1.0 / oa, HALF, 1),
                    pltpu.roll(ob, HALF, 1) * (1.0 / ob))
    o_ref[0] = jnp.where(row >= FRONT_PAD, out, 0.0).astype(BF16)


def _mla(mq, mk, mv):
    b, lp, _ = mq.shape
    pairs = B_HEADS // 2
    tq = _pick(lp, (384, 128))
    return pl.pallas_call(
        functools.partial(_mla_body, tq=tq),
        grid=(b, pairs, lp // tq),
        in_specs=[pl.BlockSpec((1, tq, 2 * LANES), lambda i, j, t: (i, t, j)),
                  pl.BlockSpec((1, lp, 2 * LANES), lambda i, j, t: (i, 0, j)),
                  pl.BlockSpec((1, lp, 2 * LANES), lambda i, j, t: (i, 0, j))],
        out_specs=pl.BlockSpec((1, tq, LANES), lambda i, j, t: (i, t, j)),
        out_shape=jax.ShapeDtypeStruct((b, lp, B_HEADS * B_DV), BF16),
        scratch_shapes=[pltpu.VMEM((2, tq, lp), F32), pltpu.VMEM((2, tq, lp), BF16),
                        pltpu.VMEM((tq, LANES), F32)],
        compiler_params=_params("parallel", "parallel", "parallel"),
        name="mla",
    )(mq, mk, mv)


def _head_rms(x, first, gain):
    sq = x * x
    inv = 1.0 / HALF
    r0 = lax.rsqrt(jnp.sum(jnp.where(first, sq, 0.0), axis=-1, keepdims=True) * inv + EPS)
    r1 = lax.rsqrt(jnp.sum(jnp.where(first, 0.0, sq), axis=-1, keepdims=True) * inv + EPS)
    return x * jnp.where(first, r0, r1) * gain


def _odd_in_body(h_ref, g_ref, win_ref, gq_ref, gk_ref, tab_ref, q_ref, k_ref, vt_ref, *, tm):
    x = h_ref[0]
    hn = _rms_rows(x, g_ref[...]).astype(BF16)
    z = _dot(hn, win_ref[...])
    cos, lo, hi = tab_ref[0], tab_ref[1], tab_ref[2]
    first = lax.broadcasted_iota(jnp.int32, (tm, LANES), 1) < HALF
    nq = C_HEADS * C_DH
    nk = 2 * C_KV_HEADS * C_DH
    for c in range(nq // LANES):
        sl = slice(c * LANES, (c + 1) * LANES)
        y = _head_rms(z[:, sl], first, gq_ref[...])
        q_ref[0, :, sl] = _rope_pairs(y, cos, lo, hi, C_ROT // 2).astype(BF16)
    for c in range(nk // LANES):
        sl = slice(c * LANES, (c + 1) * LANES)
        y = _head_rms(z[:, nq + c * LANES:nq + (c + 1) * LANES], first, gk_ref[...])
        k_ref[0, :, sl] = _rope_pairs(y, cos, lo, hi, C_ROT // 2).astype(BF16)
    vt_ref[0] = z[:, nq + nk:].T.astype(BF16)


def _odd_in(h, p):
    b, lp, d = h.shape
    tm = _pick(lp, (384, 128))
    tok = lambda w: pl.BlockSpec((1, tm, w), lambda i, j: (i, j, 0))
    tab = pl.BlockSpec((3, tm, LANES), lambda i, j: (0, j, 0))
    consts = [p["mix_g"], p["w_in"], p["gq"], p["gk"]]
    nq, nk, nv = C_HEADS * C_DH, 2 * C_KV_HEADS * C_DH, C_KV_HEADS * C_DH
    return pl.pallas_call(
        functools.partial(_odd_in_body, tm=tm),
        grid=(b, lp // tm),
        in_specs=[tok(d)] + [_resident(c.shape) for c in consts] + [tab],
        out_specs=[tok(nq), tok(nk), pl.BlockSpec((1, nv, tm), lambda i, j: (i, 0, j))],
        out_shape=[jax.ShapeDtypeStruct((b, lp, nq), BF16), jax.ShapeDtypeStruct((b, lp, nk), BF16),
                   jax.ShapeDtypeStruct((b, nv, lp), BF16)],
        compiler_params=_params("parallel", "parallel"),
        name="odd_in",
    )(h, *consts, p["swa_tab"])


def _swa_body(sink_ref, q_ref, km_ref, kp_ref, kc_ref, kn_ref, vm_ref, vp_ref, vc_ref, vn_ref,
              o_ref, *, nb):
    n = pl.program_id(1)
    nkeys = 4 * BLOCK
    group = C_HEADS // C_KV_HEADS
    kk = lax.broadcasted_iota(jnp.int32, (nkeys, BLOCK), 0)
    qq = lax.broadcasted_iota(jnp.int32, (nkeys, BLOCK), 1)
    q_pos = n * BLOCK + qq - FRONT_PAD
    k_pos = (n - 2) * BLOCK + kk - FRONT_PAD
    seq_len = nb * BLOCK - FRONT_PAD
    band = (kk >= BLOCK) & (k_pos >= N_META) & (k_pos < seq_len) & (jnp.abs(q_pos - k_pos) <= C_WINDOW)
    mask = band | ((kk < BLOCK) & (kk >= FRONT_PAD))
    lane = lax.broadcasted_iota(jnp.int32, (BLOCK, LANES), 1)
    rowi = lax.broadcasted_iota(jnp.int32, (BLOCK, LANES), 0)
    first = lane < HALF
    keep = (jnp.where(first, 1.0, 0.0).astype(BF16), jnp.where(first, 0.0, 1.0).astype(BF16))
    row_ok = rowi + n * BLOCK >= FRONT_PAD
    ones = jnp.ones((C_DH, nkeys), BF16)
    for kv in range(C_KV_HEADS):
        kvs = slice(kv * LANES, (kv + 1) * LANES)
        kcat = jnp.concatenate([r[0, :, kvs] for r in (km_ref, kp_ref, kc_ref, kn_ref)], axis=0)
        vt = jnp.concatenate([r[0, kv * C_DH:(kv + 1) * C_DH, :] for r in (vm_ref, vp_ref, vc_ref, vn_ref)],
                             axis=1)
        vt = jnp.concatenate([vt, ones], axis=0)
        pairs = [q_ref[0, :, (kv * group // 2 + i) * LANES:(kv * group // 2 + i + 1) * LANES]
                 for i in range(group // 2)]
        qs = jnp.concatenate([p2 * keep[w] for p2 in pairs for w in range(2)], axis=0)
        st = _dot_nt(kcat, qs)
        normed = []
        for g in range(group):
            sink = sink_ref[kv * group + g] * LOG2E
            s = jnp.where(mask, st[:, g * BLOCK:(g + 1) * BLOCK], NEG)
            m = jnp.maximum(jnp.max(s, axis=0, keepdims=True), sink)
            ot = _dot(vt, jnp.exp2(s - m).astype(BF16))
            den = ot[C_DH:C_DH + 1, :] + jnp.exp2(sink - m)
            normed.append(ot[:C_DH, :] * (1.0 / den))
        for i in range(group // 2):
            slab = jnp.concatenate([normed[2 * i], normed[2 * i + 1]], axis=0).T
            sl = slice((kv * group // 2 + i) * LANES, (kv * group // 2 + i + 1) * LANES)
            o_ref[0, :, sl] = jnp.where(row_ok, slab, 0.0).astype(BF16)


def _swa(q, k, vt, sink):
    b, lp, qw = q.shape
    nb = lp // BLOCK
    kw = k.shape[-1]
    vw = vt.shape[1]
    prev = lambda n: jnp.maximum(n - 1, 0)
    nxt = lambda n: jnp.minimum(n + 1, nb - 1)
    qspec = pl.BlockSpec((1, BLOCK, qw), lambda i, n: (i, n, 0))
    kspecs = [pl.BlockSpec((1, BLOCK, kw), lambda i, n: (i, 0, 0)),
              pl.BlockSpec((1, BLOCK, kw), lambda i, n: (i, prev(n), 0)),
              pl.BlockSpec((1, BLOCK, kw), lambda i, n: (i, n, 0)),
              pl.BlockSpec((1, BLOCK, kw), lambda i, n: (i, nxt(n), 0))]
    vspecs = [pl.BlockSpec((1, vw, BLOCK), lambda i, n: (i, 0, 0)),
              pl.BlockSpec((1, vw, BLOCK), lambda i, n: (i, 0, prev(n))),
              pl.BlockSpec((1, vw, BLOCK), lambda i, n: (i, 0, n)),
              pl.BlockSpec((1, vw, BLOCK), lambda i, n: (i, 0, nxt(n)))]
    return pl.pallas_call(
        functools.partial(_swa_body, nb=nb),
        grid=(b, nb),
        in_specs=[pl.BlockSpec(memory_space=pltpu.SMEM), qspec] + kspecs + vspecs,
        out_specs=qspec,
        out_shape=jax.ShapeDtypeStruct((b, lp, qw), BF16),
        compiler_params=_params("parallel", "parallel"),
        name="swa",
    )(sink, q, k, k, k, k, vt, vt, vt, vt)


def _rope_table(lp, theta, rot, group, offset, scale=1.0):
    half = rot // 2
    pos = jnp.arange(lp, dtype=F32) - FRONT_PAD
    lane = np.arange(LANES) % group - offset
    in_lo = (lane >= 0) & (lane < half)
    in_hi = (lane >= half) & (lane < rot)
    idx = np.where(in_lo | in_hi, lane % half, 0)
    inv = theta ** (-jnp.asarray(idx, F32) * 2.0 / rot)
    ang = pos[:, None] * inv[None, :]
    cos = jnp.where(in_lo | in_hi, jnp.cos(ang), 1.0)
    sin = jnp.sin(ang)
    lo = jnp.where(in_lo, -sin, 0.0)
    hi = jnp.where(in_hi, sin, 0.0)
    return jnp.stack([cos, lo, hi]) * scale


def _pair_lanes(x):
    h, n = x.shape
    return jnp.repeat(x.reshape(h // 2, 2, n).transpose(0, 2, 1), HALF, axis=2)


def _prep_ffn(gain, wg, wu, wd, fc=256):
    d, f = wg.shape
    nc = f // fc
    return (gain.reshape(1, d),
            wg.reshape(d, nc, fc).transpose(1, 0, 2).astype(BF16),
            wu.reshape(d, nc, fc).transpose(1, 0, 2).astype(BF16),
            wd.reshape(nc, fc, d).astype(BF16))


def _prep_even(lp, mix_g, w_in, dec_f, dec_b, ret_norm, q_norm, w_qb, kv_norm, w_kvb, gq, gk, w_out):
    d = w_in.shape[0]
    cut = 4 * A_HEADS * A_DK + B_Q_LORA + B_KV_LORA
    kr_cols = jnp.concatenate([jnp.zeros((d, B_NOPE), F32), w_in[:, cut:],
                               jnp.zeros((d, LANES - B_QK), F32)], axis=1)
    w_in2 = jnp.concatenate([w_in[:, :cut], kr_cols], axis=1).astype(BF16)
    w_qb2 = jnp.pad(w_qb.reshape(B_Q_LORA, B_HEADS, B_QK), ((0, 0), (0, 0), (0, LANES - B_QK)))
    w_kv3 = w_kvb.reshape(B_KV_LORA, B_HEADS, B_NOPE + B_DV)
    w_kb = jnp.pad(w_kv3[:, :, :B_NOPE], ((0, 0), (0, 0), (0, LANES - B_NOPE)))
    w_vb = jnp.pad(w_kv3[:, :, B_NOPE:], ((0, 0), (0, 0), (0, LANES - B_DV)))
    pad96 = lambda g: jnp.pad(g, (0, LANES - B_QK)).reshape(1, LANES)

    lgf = -jnp.exp(dec_f.astype(F32))
    lgb = -jnp.exp(dec_b.astype(F32))
    idx = jnp.arange(BLOCK, dtype=F32)
    diff = idx[:, None] - idx[None, :]
    dmat = (jnp.where(diff >= 0, jnp.exp(lgf[:, None, None] * jnp.maximum(diff, 0.0)), 0.0)
            + jnp.where(diff < 0, jnp.exp(lgb[:, None, None] * jnp.maximum(-diff, 0.0)), 0.0))
    ones = jnp.ones((BLOCK,), F32)
    vec = jnp.stack([
        _pair_lanes(jnp.exp(lgf[:, None] * (BLOCK - 1 - idx)[None, :])),
        _pair_lanes(jnp.exp(lgb[:, None] * idx[None, :])),
        _pair_lanes(jnp.exp(lgf[:, None] * (idx + 1.0)[None, :])),
        _pair_lanes(jnp.exp(lgb[:, None] * (BLOCK - idx)[None, :])),
        _pair_lanes(jnp.exp(BLOCK * lgf)[:, None] * ones[None, :]),
        _pair_lanes(jnp.exp(BLOCK * lgb)[:, None] * ones[None, :]),
    ], axis=1)
    half = A_HEADS * A_DK
    return dict(
        mix_g=mix_g.reshape(1, d), w_in=w_in2,
        q_norm=q_norm.reshape(1, -1), w_qb=w_qb2.reshape(B_Q_LORA, -1).astype(BF16),
        kv_norm=kv_norm.reshape(1, -1), w_kb=w_kb.reshape(B_KV_LORA, -1).astype(BF16),
        w_vb=w_vb.reshape(B_KV_LORA, -1).astype(BF16),
        gq=pad96(gq) * (B_QK ** -0.5 * LOG2E), gk=pad96(gk),
        ret_tab=_rope_table(lp, A_THETA, A_DK, A_DK, 0),
        mla_tab=_rope_table(lp, B_THETA, B_ROPE, LANES, B_NOPE),
        ret_dmat=dmat, ret_vec=vec,
        ret_gain=ret_norm.reshape(A_HEADS // 2, 1, LANES),
        w_out=(w_out[:half].astype(BF16), w_out[half:].astype(BF16)),
    )


def _prep_odd(lp, mix_g, w_in, gq, gk, sink, w_out):
    d = w_in.shape[0]
    nq = C_HEADS * C_DH
    nkv = C_KV_HEADS * C_DH
    dup = lambda w: jnp.repeat(w.reshape(d, C_KV_HEADS, 1, C_DH), 2, axis=2).reshape(d, 2 * nkv)
    w_in2 = jnp.concatenate([w_in[:, :nq], dup(w_in[:, nq:nq + nkv]), w_in[:, nq + nkv:]], axis=1)
    two = lambda g: jnp.concatenate([g, g]).reshape(1, LANES)
    return dict(
        mix_g=mix_g.reshape(1, d), w_in=w_in2.astype(BF16),
        gq=two(gq) * (C_DH ** -0.5 * LOG2E), gk=two(gk),
        swa_tab=_rope_table(lp, C_THETA, C_ROT, C_DH, 0),
        sink=sink.astype(F32), w_out=(w_out.astype(BF16),),
    )


def _trunk(x, meta, layers):
    b, seq, d = x.shape
    lp = FRONT_PAD + N_META + seq
    h = jnp.concatenate([jnp.zeros((b, FRONT_PAD, d), x.dtype),
                         jnp.broadcast_to(meta[None].astype(x.dtype), (b, N_META, d)), x], axis=1)
    h = _ffn(h.reshape(b * lp, d), layers[0]["ffn0"])
    for i, layer in enumerate(layers):
        h3 = h.reshape(b, lp, d)
        mp = layer["mix"]
        if layer["even"]:
            rq, rk, rv, rg, mq, mk, mv = _even_in(h3, mp)
            xs = [_retention(rq, rk, rv, rg, mp), _mla(mq, mk, mv)]
        else:
            q, k, v = _odd_in(h3, mp)
            xs = [_swa(q, k, v, mp["sink"])]
        xs = [a.reshape(b * lp, a.shape[-1]) for a in xs]
        h = _proj_ffn(h, xs, mp["w_out"], layer["ffn1"])
        if i + 1 < len(layers):
            h = _ffn(h, layers[i + 1]["ffn0"])
    return h.reshape(b, lp, d)[:, BLOCK:]


def kernel(x_prompt, x_sample, meta_tokens, ffn_norm, ffn_w_gate, ffn_w_up, ffn_w_down, mix_norm, even_w_in, ret_decay_f, ret_decay_b, ret_out_norm, mla_q_norm, mla_w_qb, mla_kv_norm, mla_w_kvb, mla_qk_norm_q, mla_qk_norm_k, even_w_out, odd_w_in, swa_q_norm, swa_k_norm, swa_sink, odd_w_out):
    depth = ffn_norm.shape[0]
    assert x_prompt.shape[1] == x_sample.shape[1]
    lp = FRONT_PAD + N_META + x_prompt.shape[1]
    layers = []
    for layer in range(depth):
        i = layer // 2
        ffn = [_prep_ffn(ffn_norm[layer, s], ffn_w_gate[layer, s], ffn_w_up[layer, s], ffn_w_down[layer, s])
               for s in range(2)]
        if layer % 2 == 0:
            mix = _prep_even(lp, mix_norm[layer], even_w_in[i], ret_decay_f[i], ret_decay_b[i],
                             ret_out_norm[i], mla_q_norm[i], mla_w_qb[i], mla_kv_norm[i], mla_w_kvb[i],
                             mla_qk_norm_q[i], mla_qk_norm_k[i], even_w_out[i])
        else:
            mix = _prep_odd(lp, mix_norm[layer], odd_w_in[i], swa_q_norm[i], swa_k_norm[i],
                            swa_sink[i], odd_w_out[i])
        layers.append(dict(even=layer % 2 == 0, ffn0=ffn[0], ffn1=ffn[1], mix=mix))
    return (_trunk(x_prompt, meta_tokens, layers), _trunk(x_sample, meta_tokens, layers))
```

```python
import functools

import jax
import jax.numpy as jnp
import numpy as np
from jax import lax
from jax.experimental import pallas as pl
from jax.experimental.pallas import tpu as pltpu

F32 = jnp.float32
BF16 = jnp.bfloat16

D_MODEL = 1024
N_META = 16
BLOCK = 128
FRONT_PAD = BLOCK - N_META
EPS = 1e-6
NEG = -1e30
LOG2E = 1.4426950408889634
LANES = 128
HALF = LANES // 2

A_HEADS = 8
A_DK = 64
A_THETA = 10000.0
B_HEADS = 8
B_Q_LORA = 256
B_KV_LORA = 128
B_NOPE = 64
B_ROPE = 32
B_DV = 64
B_QK = B_NOPE + B_ROPE
B_THETA = 10000.0
C_HEADS = 16
C_KV_HEADS = 4
C_DH = 64
C_ROT = 16
C_THETA = 500000.0
C_WINDOW = 128

VMEM_LIMIT = 56 * 1024 * 1024


def _params(*sem):
    return pltpu.CompilerParams(dimension_semantics=sem, vmem_limit_bytes=VMEM_LIMIT)


def _resident(shape):
    zeros = (0,) * len(shape)
    return pl.BlockSpec(shape, lambda *_: zeros, pipeline_mode=pl.Buffered(1))


def _rms_rows(x, gain):
    return x * lax.rsqrt(jnp.mean(x * x, axis=-1, keepdims=True) + EPS) * gain


def _dot(a, b):
    return jnp.dot(a, b, preferred_element_type=F32)


def _dot_nt(a, b):
    return lax.dot_general(a, b, (((1,), (1,)), ((), ())), preferred_element_type=F32)


def _pick(n, options):
    for o in options:
        if n % o == 0:
            return o
    raise ValueError(f"no tile in {options} divides {n}")


def _ffn_core(x, g_ref, wg_ref, wu_ref, wd_ref):
    xn = _rms_rows(x, g_ref[...]).astype(BF16)
    acc = jnp.zeros_like(x)
    for c in range(wg_ref.shape[0]):
        gate = _dot(xn, wg_ref[c])
        up = _dot(xn, wu_ref[c])
        act = (gate * (1.0 / (1.0 + jnp.exp(-gate))) * up).astype(BF16)
        acc = acc + _dot(act, wd_ref[c])
    return x + 0.5 * acc


def _ffn_body(h_ref, g_ref, wg_ref, wu_ref, wd_ref, o_ref):
    o_ref[...] = _ffn_core(h_ref[...], g_ref, wg_ref, wu_ref, wd_ref)


def _proj_ffn_body(*refs, n_in):
    h_ref = refs[0]
    xs = refs[1:1 + n_in]
    ws = refs[1 + n_in:1 + 2 * n_in]
    g_ref, wg_ref, wu_ref, wd_ref, o_ref = refs[1 + 2 * n_in:]
    h2 = h_ref[...]
    for x_ref, w_ref in zip(xs, ws):
        h2 = h2 + _dot(x_ref[...], w_ref[...])
    o_ref[...] = _ffn_core(h2, g_ref, wg_ref, wu_ref, wd_ref)


def _ffn_specs(ffn):
    return [_resident(a.shape) for a in ffn]


def _ffn(h2d, ffn):
    t, d = h2d.shape
    tm = _pick(t, (512, 384, 256, 128))
    row = pl.BlockSpec((tm, d), lambda i: (i, 0))
    return pl.pallas_call(
        _ffn_body,
        grid=(t // tm,),
        in_specs=[row] + _ffn_specs(ffn),
        out_specs=row,
        out_shape=jax.ShapeDtypeStruct((t, d), F32),
        compiler_params=_params("parallel"),
        name="ffn",
    )(h2d, *ffn)


def _proj_ffn(h2d, xs, ws, ffn):
    t, d = h2d.shape
    tm = _pick(t, (512, 384, 256, 128))
    row = pl.BlockSpec((tm, d), lambda i: (i, 0))
    x_specs = [pl.BlockSpec((tm, x.shape[1]), lambda i: (i, 0)) for x in xs]
    w_specs = [_resident(w.shape) for w in ws]
    return pl.pallas_call(
        functools.partial(_proj_ffn_body, n_in=len(xs)),
        grid=(t // tm,),
        in_specs=[row] + x_specs + w_specs + _ffn_specs(ffn),
        out_specs=row,
        out_shape=jax.ShapeDtypeStruct((t, d), F32),
        compiler_params=_params("parallel"),
        name="proj_ffn",
    )(h2d, *xs, *ws, *ffn)


def _rope_pairs(y, cos, sin_lo, sin_hi, half):
    n = y.shape[-1]
    return y * cos + pltpu.roll(y, n - half, 1) * sin_lo + pltpu.roll(y, half, 1) * sin_hi


def _even_in_body(h_ref, g_ref, win_ref, qn_ref, wqb_ref, wqr_ref, kvn_ref, wkb_ref, wvb_ref,
                  gq_ref, gk_ref, rtab_ref, mtab_ref,
                  rq_ref, rk_ref, rv_ref, rg_ref, mq_ref, mk_ref, mv_ref, *, tm):
    x = h_ref[0]
    hn = _rms_rows(x, g_ref[...]).astype(BF16)
    z = _dot(hn, win_ref[...])
    o = B_Q_LORA + B_KV_LORA
    cq = _rms_rows(z[:, :B_Q_LORA], qn_ref[...]).astype(BF16)
    ckv = _rms_rows(z[:, B_Q_LORA:o], kvn_ref[...]).astype(BF16)
    kr = z[:, o:o + LANES]
    kr_rot = z[:, o + LANES:o + 2 * LANES]
    q_all = _dot(cq, wqb_ref[...])
    q_rot = _dot(cq, wqr_ref[...])
    k_all = _dot(ckv, wkb_ref[...])
    v_all = _dot(ckv, wvb_ref[...])

    mcos, msin = mtab_ref[0], mtab_ref[1]
    lane = lax.broadcasted_iota(jnp.int32, (tm, LANES), 1)
    v_ones = jnp.where(lane >= HALF, 1.0, 0.0)
    row = lax.broadcasted_iota(jnp.int32, (tm, LANES), 0) + pl.program_id(1) * tm
    bias_lane = lane == B_QK
    q_bias = jnp.where(bias_lane, 1.0, 0.0)
    k_bias = jnp.where(bias_lane & (row < FRONT_PAD), NEG, 0.0)
    inv = 1.0 / B_QK
    gq, gq_rot, gk, gk_rot = gq_ref[0:1], gq_ref[1:2], gk_ref[0:1], gk_ref[1:2]
    for hd in range(B_HEADS):
        sl = slice(hd * LANES, (hd + 1) * LANES)
        qh = q_all[:, sl]
        r = lax.rsqrt(jnp.sum(qh * qh, axis=-1, keepdims=True) * inv + EPS)
        qh = (qh * r * gq) * mcos + (q_rot[:, sl] * r * gq_rot) * msin
        mq_ref[0, :, sl] = (qh + q_bias).astype(BF16)
        kh = k_all[:, sl] + kr
        r = lax.rsqrt(jnp.sum(kh * kh, axis=-1, keepdims=True) * inv + EPS)
        kh = (kh * r * gk) * mcos + (kr_rot * r * gk_rot) * msin
        mk_ref[0, :, sl] = (kh + k_bias).astype(BF16)
        mv_ref[0, :, sl] = (v_all[:, sl] + v_ones).astype(BF16)

    hw = A_HEADS * A_DK
    o += 2 * LANES
    rcos, rsin = rtab_ref[0], rtab_ref[1]
    for c in range(hw // LANES):
        sl = slice(c * LANES, (c + 1) * LANES)
        q = z[:, o + c * LANES:o + (c + 1) * LANES]
        k = z[:, o + hw + c * LANES:o + hw + (c + 1) * LANES]
        rq_ref[0, :, sl] = (q * rcos + pltpu.roll(q, HALF, 1) * rsin).astype(BF16)
        rk_ref[0, :, sl] = ((k * rcos + pltpu.roll(k, HALF, 1) * rsin) * (A_DK ** -0.5)).astype(BF16)
    rv_ref[0] = z[:, o + 2 * hw:o + 3 * hw].astype(BF16)
    ga = z[:, o + 3 * hw:o + 4 * hw]
    rg_ref[0] = (ga * (1.0 / (1.0 + jnp.exp(-ga)))).astype(BF16)


def _even_in(h, p):
    b, lp, d = h.shape
    tm = _pick(lp, (384, 128))
    tok = lambda w: pl.BlockSpec((1, tm, w), lambda i, j: (i, j, 0))
    tab = pl.BlockSpec((2, tm, LANES), lambda i, j: (0, j, 0))
    consts = [p["mix_g"], p["w_in"], p["q_norm"], p["w_qb"], p["w_qr"], p["kv_norm"], p["w_kb"], p["w_vb"],
              p["gq"], p["gk"]]
    hw = A_HEADS * A_DK
    widths = [hw, hw, hw, hw, B_HEADS * LANES, B_HEADS * LANES, B_HEADS * LANES]
    return pl.pallas_call(
        functools.partial(_even_in_body, tm=tm),
        grid=(b, lp // tm),
        in_specs=[tok(d)] + [_resident(c.shape) for c in consts] + [tab, tab],
        out_specs=[tok(w) for w in widths],
        out_shape=[jax.ShapeDtypeStruct((b, lp, w), BF16) for w in widths],
        compiler_params=_params("parallel", "parallel"),
        name="even_in",
    )(h, *consts, p["ret_tab"], p["mla_tab"])


def _ret_body(q_ref, k_ref, v_ref, g_ref, dmat_ref, vec_ref, gain_ref, o_ref, acc_ref, kv_ref, st_ref,
              *, nb):
    lane = lax.broadcasted_iota(jnp.int32, (BLOCK, LANES), 1)
    first = lane < HALF
    wkf, wkb, qf, qb = vec_ref[0, 0], vec_ref[0, 1], vec_ref[0, 2], vec_ref[0, 3]
    cf, cb = vec_ref[0, 4][:1], vec_ref[0, 5][:1]
    dcat = jnp.concatenate([dmat_ref[0], dmat_ref[1]], axis=0)
    qk_first = lane % HALF < HALF // 2
    keep0 = jnp.where(qk_first, 1.0, 0.0).astype(BF16)
    keep1 = jnp.where(qk_first, 0.0, 1.0).astype(BF16)
    row2 = lax.broadcasted_iota(jnp.int32, (2 * LANES, LANES), 0) % HALF
    col2 = lax.broadcasted_iota(jnp.int32, (2 * LANES, LANES), 1)
    same2 = (row2 < HALF // 2) == (col2 < HALF)
    unroll = next(u for u in (11, 3, 1) if nb % u == 0)

    def intra(n, carry):
        rows = pl.ds(pl.multiple_of(n * BLOCK, BLOCK), BLOCK)
        q, k, v = q_ref[0, rows, :], k_ref[0, rows, :], v_ref[0, rows, :]
        qs = jnp.concatenate([q * keep0, q * keep1], axis=0)
        s = (_dot_nt(qs, k) * dcat).astype(BF16)
        o2 = _dot(s, v)
        acc_ref[rows, :] = jnp.where(first, o2[:BLOCK], o2[BLOCK:])
        kf = k.astype(F32)
        kw = jnp.concatenate([kf * wkf, kf * wkb], axis=1).T.astype(BF16)
        kv_ref[n] = jnp.where(same2, _dot(kw, v), 0.0)
        return carry

    lax.fori_loop(0, nb, intra, 0, unroll=unroll)

    def scan(i, carry):
        sf, sb = carry
        j = nb - 1 - i
        st_ref[i, :LANES, :] = sf.astype(BF16)
        st_ref[j, LANES:, :] = sb.astype(BF16)
        return sf * cf + kv_ref[i, :LANES, :], sb * cb + kv_ref[j, LANES:, :]

    zero = jnp.zeros((LANES, LANES), F32)
    lax.fori_loop(0, nb, scan, (zero, zero))

    def inter(n, carry):
        rows = pl.ds(pl.multiple_of(n * BLOCK, BLOCK), BLOCK)
        qf32 = q_ref[0, rows, :].astype(F32)
        qq = jnp.concatenate([qf32 * qf, qf32 * qb], axis=1).astype(BF16)
        o = acc_ref[rows, :] + _dot(qq, st_ref[n])
        sq = o * o
        inv = 1.0 / HALF
        r0 = lax.rsqrt(jnp.sum(jnp.where(first, sq, 0.0), axis=-1, keepdims=True) * inv + EPS)
        r1 = lax.rsqrt(jnp.sum(jnp.where(first, 0.0, sq), axis=-1, keepdims=True) * inv + EPS)
        out = o * jnp.where(first, r0, r1) * gain_ref[0] * g_ref[0, rows, :].astype(F32)
        o_ref[0, rows, :] = out.astype(BF16)
        return carry

    lax.fori_loop(0, nb, inter, 0, unroll=unroll)


def _retention(rq, rk, rv, rg, p):
    b, lp, hw = rq.shape
    pairs = hw // LANES
    nb = lp // BLOCK
    seq = pl.BlockSpec((1, lp, LANES), lambda i, j: (i, 0, j))
    return pl.pallas_call(
        functools.partial(_ret_body, nb=nb),
        grid=(b, pairs),
        in_specs=[seq, seq, seq, seq,
                  pl.BlockSpec((2, BLOCK, BLOCK), lambda i, j: (j, 0, 0)),
                  pl.BlockSpec((1, 6, BLOCK, LANES), lambda i, j: (j, 0, 0, 0)),
                  pl.BlockSpec((1, 1, LANES), lambda i, j: (j, 0, 0))],
        out_specs=seq,
        out_shape=jax.ShapeDtypeStruct((b, lp, hw), BF16),
        scratch_shapes=[pltpu.VMEM((lp, LANES), F32), pltpu.VMEM((nb, 2 * LANES, LANES), F32),
                        pltpu.VMEM((nb, 2 * LANES, LANES), BF16)],
        compiler_params=_params("parallel", "parallel"),
        name="retention",
    )(rq, rk, rv, rg, p["ret_dmat"], p["ret_vec"], p["ret_gain"])


MXU_TILE = 256
MLA_ROWS = 16


def _mla_body(q_ref, k_ref, v_ref, o_ref, s_scr, p_scr, o_scr, *, tq):
    lp = k_ref.shape[1]
    chunks = [(c, min(MXU_TILE, lp - c)) for c in range(0, lp, MXU_TILE)]
    hk = lp // 2 // MXU_TILE * MXU_TILE

    def scores(hd):
        q = q_ref[0, :, hd * LANES:(hd + 1) * LANES]
        for c, w in chunks:
            s_scr[hd, :, c:c + w] = _dot_nt(q, k_ref[0, c:c + w, hd * LANES:(hd + 1) * LANES])

    def probs(hd):
        for r in range(0, tq, MLA_ROWS):
            sb = s_scr[hd, r:r + MLA_ROWS, :]
            m = jnp.max(sb, axis=-1, keepdims=True)
            p_scr[hd, r:r + MLA_ROWS, :] = jnp.exp2(sb - m).astype(BF16)

    def weighted(hd):
        v = v_ref[0, :, hd * LANES:(hd + 1) * LANES]
        if hk == 0:
            return _dot(p_scr[hd], v)
        return _dot(p_scr[hd, :, :hk], v[:hk]) + _dot(p_scr[hd, :, hk:], v[hk:])

    scores(0)
    probs(0)
    scores(1)
    o_scr[...] = weighted(0)
    probs(1)
    oa = o_scr[...]
    ob = weighted(1)
    lane = lax.broadcasted_iota(jnp.int32, (tq, LANES), 1)
    row = lax.broadcasted_iota(jnp.int32, (tq, LANES), 0) + pl.program_id(2) * tq
    out = jnp.where(lane < HALF, oa * pltpu.roll(1.0 / oa, HALF, 1),
                    pltpu.roll(ob, HALF, 1) * (1.0 / ob))
    o_ref[0] = jnp.where(row >= FRONT_PAD, out, 0.0).astype(BF16)


def _mla(mq, mk, mv):
    b, lp, _ = mq.shape
    pairs = B_HEADS // 2
    tq = _pick(lp, (384, 128))
    return pl.pallas_call(
        functools.partial(_mla_body, tq=tq),
        grid=(b, pairs, lp // tq),
        in_specs=[pl.BlockSpec((1, tq, 2 * LANES), lambda i, j, t: (i, t, j)),
                  pl.BlockSpec((1, lp, 2 * LANES), lambda i, j, t: (i, 0, j)),
                  pl.BlockSpec((1, lp, 2 * LANES), lambda i, j, t: (i, 0, j))],
        out_specs=pl.BlockSpec((1, tq, LANES), lambda i, j, t: (i, t, j)),
        out_shape=jax.ShapeDtypeStruct((b, lp, B_HEADS * B_DV), BF16),
        scratch_shapes=[pltpu.VMEM((2, tq, lp), F32), pltpu.VMEM((2, tq, lp), BF16),
                        pltpu.VMEM((tq, LANES), F32)],
        compiler_params=_params("parallel", "parallel", "parallel"),
        name="mla",
    )(mq, mk, mv)


def _head_rms(x, first, gain):
    sq = x * x
    inv = 1.0 / HALF
    r0 = lax.rsqrt(jnp.sum(jnp.where(first, sq, 0.0), axis=-1, keepdims=True) * inv + EPS)
    r1 = lax.rsqrt(jnp.sum(jnp.where(first, 0.0, sq), axis=-1, keepdims=True) * inv + EPS)
    return x * jnp.where(first, r0, r1) * gain


def _odd_in_body(h_ref, g_ref, win_ref, gq_ref, gk_ref, tab_ref, q_ref, k_ref, vt_ref, *, tm):
    x = h_ref[0]
    hn = _rms_rows(x, g_ref[...]).astype(BF16)
    z = _dot(hn, win_ref[...])
    cos, lo, hi = tab_ref[0], tab_ref[1], tab_ref[2]
    first = lax.broadcasted_iota(jnp.int32, (tm, LANES), 1) < HALF
    nq = C_HEADS * C_DH
    nk = 2 * C_KV_HEADS * C_DH
    for c in range(nq // LANES):
        sl = slice(c * LANES, (c + 1) * LANES)
        y = _head_rms(z[:, sl], first, gq_ref[...])
        q_ref[0, :, sl] = _rope_pairs(y, cos, lo, hi, C_ROT // 2).astype(BF16)
    for c in range(nk // LANES):
        sl = slice(c * LANES, (c + 1) * LANES)
        y = _head_rms(z[:, nq + c * LANES:nq + (c + 1) * LANES], first, gk_ref[...])
        k_ref[0, :, sl] = _rope_pairs(y, cos, lo, hi, C_ROT // 2).astype(BF16)
    vt_ref[0] = z[:, nq + nk:].T.astype(BF16)


def _odd_in(h, p):
    b, lp, d = h.shape
    tm = _pick(lp, (384, 128))
    tok = lambda w: pl.BlockSpec((1, tm, w), lambda i, j: (i, j, 0))
    tab = pl.BlockSpec((3, tm, LANES), lambda i, j: (0, j, 0))
    consts = [p["mix_g"], p["w_in"], p["gq"], p["gk"]]
    nq, nk, nv = C_HEADS * C_DH, 2 * C_KV_HEADS * C_DH, C_KV_HEADS * C_DH
    return pl.pallas_call(
        functools.partial(_odd_in_body, tm=tm),
        grid=(b, lp // tm),
        in_specs=[tok(d)] + [_resident(c.shape) for c in consts] + [tab],
        out_specs=[tok(nq), tok(nk), pl.BlockSpec((1, nv, tm), lambda i, j: (i, 0, j))],
        out_shape=[jax.ShapeDtypeStruct((b, lp, nq), BF16), jax.ShapeDtypeStruct((b, lp, nk), BF16),
                   jax.ShapeDtypeStruct((b, nv, lp), BF16)],
        compiler_params=_params("parallel", "parallel"),
        name="odd_in",
    )(h, *consts, p["swa_tab"])


def _swa_body(sink_ref, q_ref, km_ref, kp_ref, kc_ref, kn_ref, vm_ref, vp_ref, vc_ref, vn_ref,
              o_ref, *, nb):
    n = pl.program_id(1)
    nkeys = 4 * BLOCK
    group = C_HEADS // C_KV_HEADS
    kk = lax.broadcasted_iota(jnp.int32, (nkeys, BLOCK), 0)
    qq = lax.broadcasted_iota(jnp.int32, (nkeys, BLOCK), 1)
    q_pos = n * BLOCK + qq - FRONT_PAD
    k_pos = (n - 2) * BLOCK + kk - FRONT_PAD
    seq_len = nb * BLOCK - FRONT_PAD
    band = (kk >= BLOCK) & (k_pos >= N_META) & (k_pos < seq_len) & (jnp.abs(q_pos - k_pos) <= C_WINDOW)
    mask = band | ((kk < BLOCK) & (kk >= FRONT_PAD))
    lane = lax.broadcasted_iota(jnp.int32, (BLOCK, LANES), 1)
    rowi = lax.broadcasted_iota(jnp.int32, (BLOCK, LANES), 0)
    first = lane < HALF
    keep = (jnp.where(first, 1.0, 0.0).astype(BF16), jnp.where(first, 0.0, 1.0).astype(BF16))
    row_ok = rowi + n * BLOCK >= FRONT_PAD
    ones = jnp.ones((C_DH, nkeys), BF16)
    for kv in range(C_KV_HEADS):
        kvs = slice(kv * LANES, (kv + 1) * LANES)
        kcat = jnp.concatenate([r[0, :, kvs] for r in (km_ref, kp_ref, kc_ref, kn_ref)], axis=0)
        vt = jnp.concatenate([r[0, kv * C_DH:(kv + 1) * C_DH, :] for r in (vm_ref, vp_ref, vc_ref, vn_ref)],
                             axis=1)
        vt = jnp.concatenate([vt, ones], axis=0)
        pairs = [q_ref[0, :, (kv * group // 2 + i) * LANES:(kv * group // 2 + i + 1) * LANES]
                 for i in range(group // 2)]
        qs = jnp.concatenate([p2 * keep[w] for p2 in pairs for w in range(2)], axis=0)
        st = _dot_nt(kcat, qs)
        normed = []
        for g in range(group):
            sink = sink_ref[kv * group + g] * LOG2E
            s = jnp.where(mask, st[:, g * BLOCK:(g + 1) * BLOCK], NEG)
            m = jnp.maximum(jnp.max(s, axis=0, keepdims=True), sink)
            ot = _dot(vt, jnp.exp2(s - m).astype(BF16))
            den = ot[C_DH:C_DH + 1, :] + jnp.exp2(sink - m)
            normed.append(ot[:C_DH, :] * (1.0 / den))
        for i in range(group // 2):
            slab = jnp.concatenate([normed[2 * i], normed[2 * i + 1]], axis=0).T
            sl = slice((kv * group // 2 + i) * LANES, (kv * group // 2 + i + 1) * LANES)
            o_ref[0, :, sl] = jnp.where(row_ok, slab, 0.0).astype(BF16)


def _swa(q, k, vt, sink):
    b, lp, qw = q.shape
    nb = lp // BLOCK
    kw = k.shape[-1]
    vw = vt.shape[1]
    prev = lambda n: jnp.maximum(n - 1, 0)
    nxt = lambda n: jnp.minimum(n + 1, nb - 1)
    qspec = pl.BlockSpec((1, BLOCK, qw), lambda i, n: (i, n, 0))
    kspecs = [pl.BlockSpec((1, BLOCK, kw), lambda i, n: (i, 0, 0)),
              pl.BlockSpec((1, BLOCK, kw), lambda i, n: (i, prev(n), 0)),
              pl.BlockSpec((1, BLOCK, kw), lambda i, n: (i, n, 0)),
              pl.BlockSpec((1, BLOCK, kw), lambda i, n: (i, nxt(n), 0))]
    vspecs = [pl.BlockSpec((1, vw, BLOCK), lambda i, n: (i, 0, 0)),
              pl.BlockSpec((1, vw, BLOCK), lambda i, n: (i, 0, prev(n))),
              pl.BlockSpec((1, vw, BLOCK), lambda i, n: (i, 0, n)),
              pl.BlockSpec((1, vw, BLOCK), lambda i, n: (i, 0, nxt(n)))]
    return pl.pallas_call(
        functools.partial(_swa_body, nb=nb),
        grid=(b, nb),
        in_specs=[pl.BlockSpec(memory_space=pltpu.SMEM), qspec] + kspecs + vspecs,
        out_specs=qspec,
        out_shape=jax.ShapeDtypeStruct((b, lp, qw), BF16),
        compiler_params=_params("parallel", "parallel"),
        name="swa",
    )(sink, q, k, k, k, k, vt, vt, vt, vt)


def _rope_table(lp, theta, rot, group, offset, scale=1.0):
    half = rot // 2
    pos = jnp.arange(lp, dtype=F32) - FRONT_PAD
    lane = np.arange(LANES) % group - offset
    in_lo = (lane >= 0) & (lane < half)
    in_hi = (lane >= half) & (lane < rot)
    idx = np.where(in_lo | in_hi, lane % half, 0)
    inv = theta ** (-jnp.asarray(idx, F32) * 2.0 / rot)
    ang = pos[:, None] * inv[None, :]
    cos = jnp.where(in_lo | in_hi, jnp.cos(ang), 1.0)
    sin = jnp.sin(ang)
    lo = jnp.where(in_lo, -sin, 0.0)
    hi = jnp.where(in_hi, sin, 0.0)
    return jnp.stack([cos, lo, hi]) * scale


def _pair_lanes(x):
    h, n = x.shape
    return jnp.repeat(x.reshape(h // 2, 2, n).transpose(0, 2, 1), HALF, axis=2)


def _prep_ffn(gain, wg, wu, wd, fc=256):
    d, f = wg.shape
    nc = f // fc
    return (gain.reshape(1, d),
            wg.reshape(d, nc, fc).transpose(1, 0, 2).astype(BF16),
            wu.reshape(d, nc, fc).transpose(1, 0, 2).astype(BF16),
            wd.reshape(nc, fc, d).astype(BF16))


def _prep_even(lp, mix_g, w_in, dec_f, dec_b, ret_norm, q_norm, w_qb, kv_norm, w_kvb, gq, gk, w_out):
    d = w_in.shape[0]
    hw = A_HEADS * A_DK
    cut = 4 * hw + B_Q_LORA + B_KV_LORA
    lanes = np.arange(LANES)
    partner = lanes.copy()
    partner[B_NOPE:B_NOPE + B_ROPE // 2] += B_ROPE // 2
    partner[B_NOPE + B_ROPE // 2:B_QK] -= B_ROPE // 2
    is_rope = jnp.asarray((lanes >= B_NOPE) & (lanes < B_QK))
    rot = lambda w: jnp.where(is_rope, w[..., partner], 0.0)
    kr_cols = jnp.concatenate([jnp.zeros((d, B_NOPE), F32), w_in[:, cut:],
                               jnp.zeros((d, LANES - B_QK), F32)], axis=1)
    qk_perm = lambda w: w.reshape(d, A_HEADS // 2, 2, 2, A_DK // 2).transpose(0, 1, 3, 2, 4).reshape(d, hw)
    w_in2 = jnp.concatenate([w_in[:, 4 * hw:cut], kr_cols, rot(kr_cols),
                             qk_perm(w_in[:, :hw]), qk_perm(w_in[:, hw:2 * hw]), w_in[:, 2 * hw:4 * hw]],
                            axis=1).astype(BF16)
    w_qb2 = jnp.pad(w_qb.reshape(B_Q_LORA, B_HEADS, B_QK), ((0, 0), (0, 0), (0, LANES - B_QK)))
    w_kv3 = w_kvb.reshape(B_KV_LORA, B_HEADS, B_NOPE + B_DV)
    w_kb = jnp.pad(w_kv3[:, :, :B_NOPE], ((0, 0), (0, 0), (0, LANES - B_NOPE)))
    w_vb = jnp.pad(w_kv3[:, :, B_NOPE:], ((0, 0), (0, 0), (0, LANES - B_DV)))
    pad96 = lambda g: jnp.pad(g, (0, LANES - B_QK)).reshape(1, LANES)
    with_rot = lambda g: jnp.concatenate([g, rot(g)], axis=0)
    mla_tab = _rope_table(lp, B_THETA, B_ROPE, LANES, B_NOPE)
    pos = jnp.arange(lp, dtype=F32) - FRONT_PAD
    ang = pos[:, None] * (A_THETA ** (-jnp.asarray(lanes % (A_DK // 2), F32) * 2.0 / A_DK))[None, :]
    ret_tab = jnp.stack([jnp.cos(ang), jnp.where(jnp.asarray(lanes < HALF), -1.0, 1.0) * jnp.sin(ang)])
    qk_lanes = lambda x: _pair_lanes(x).reshape(x.shape[0] // 2, x.shape[1], 2, 2, A_DK // 2
                                                ).transpose(0, 1, 3, 2, 4).reshape(-1, x.shape[1], LANES)

    lgf = -jnp.exp(dec_f.astype(F32))
    lgb = -jnp.exp(dec_b.astype(F32))
    idx = jnp.arange(BLOCK, dtype=F32)
    diff = idx[:, None] - idx[None, :]
    dmat = (jnp.where(diff >= 0, jnp.exp(lgf[:, None, None] * jnp.maximum(diff, 0.0)), 0.0)
            + jnp.where(diff < 0, jnp.exp(lgb[:, None, None] * jnp.maximum(-diff, 0.0)), 0.0))
    ones = jnp.ones((BLOCK,), F32)
    vec = jnp.stack([
        qk_lanes(jnp.exp(lgf[:, None] * (BLOCK - 1 - idx)[None, :])),
        qk_lanes(jnp.exp(lgb[:, None] * idx[None, :])),
        qk_lanes(jnp.exp(lgf[:, None] * (idx + 1.0)[None, :])),
        qk_lanes(jnp.exp(lgb[:, None] * (BLOCK - idx)[None, :])),
        _pair_lanes(jnp.exp(BLOCK * lgf)[:, None] * ones[None, :]),
        _pair_lanes(jnp.exp(BLOCK * lgb)[:, None] * ones[None, :]),
    ], axis=1)
    half = A_HEADS * A_DK
    return dict(
        mix_g=mix_g.reshape(1, d), w_in=w_in2,
        q_norm=q_norm.reshape(1, -1), w_qb=w_qb2.reshape(B_Q_LORA, -1).astype(BF16),
        w_qr=rot(w_qb2).reshape(B_Q_LORA, -1).astype(BF16),
        kv_norm=kv_norm.reshape(1, -1), w_kb=w_kb.reshape(B_KV_LORA, -1).astype(BF16),
        w_vb=w_vb.reshape(B_KV_LORA, -1).astype(BF16),
        gq=with_rot(pad96(gq) * (B_QK ** -0.5 * LOG2E)), gk=with_rot(pad96(gk)),
        ret_tab=ret_tab,
        mla_tab=jnp.stack([mla_tab[0], mla_tab[1] + mla_tab[2]]),
        ret_dmat=dmat, ret_vec=vec,
        ret_gain=ret_norm.reshape(A_HEADS // 2, 1, LANES),
        w_out=(w_out[:half].astype(BF16), w_out[half:].astype(BF16)),
    )


def _prep_odd(lp, mix_g, w_in, gq, gk, sink, w_out):
    d = w_in.shape[0]
    nq = C_HEADS * C_DH
    nkv = C_KV_HEADS * C_DH
    dup = lambda w: jnp.repeat(w.reshape(d, C_KV_HEADS, 1, C_DH), 2, axis=2).reshape(d, 2 * nkv)
    w_in2 = jnp.concatenate([w_in[:, :nq], dup(w_in[:, nq:nq + nkv]), w_in[:, nq + nkv:]], axis=1)
    two = lambda g: jnp.concatenate([g, g]).reshape(1, LANES)
    return dict(
        mix_g=mix_g.reshape(1, d), w_in=w_in2.astype(BF16),
        gq=two(gq) * (C_DH ** -0.5 * LOG2E), gk=two(gk),
        swa_tab=_rope_table(lp, C_THETA, C_ROT, C_DH, 0),
        sink=sink.astype(F32), w_out=(w_out.astype(BF16),),
    )


def _trunk(x, meta, layers):
    b, seq, d = x.shape
    lp = FRONT_PAD + N_META + seq
    h = jnp.concatenate([jnp.zeros((b, FRONT_PAD, d), x.dtype),
                         jnp.broadcast_to(meta[None].astype(x.dtype), (b, N_META, d)), x], axis=1)
    h = _ffn(h.reshape(b * lp, d), layers[0]["ffn0"])
    for i, layer in enumerate(layers):
        h3 = h.reshape(b, lp, d)
        mp = layer["mix"]
        if layer["even"]:
            rq, rk, rv, rg, mq, mk, mv = _even_in(h3, mp)
            xs = [_retention(rq, rk, rv, rg, mp), _mla(mq, mk, mv)]
        else:
            q, k, v = _odd_in(h3, mp)
            xs = [_swa(q, k, v, mp["sink"])]
        xs = [a.reshape(b * lp, a.shape[-1]) for a in xs]
        h = _proj_ffn(h, xs, mp["w_out"], layer["ffn1"])
        if i + 1 < len(layers):
            h = _ffn(h, layers[i + 1]["ffn0"])
    return h.reshape(b, lp, d)[:, BLOCK:]


def kernel(x_prompt, x_sample, meta_tokens, ffn_norm, ffn_w_gate, ffn_w_up, ffn_w_down, mix_norm, even_w_in, ret_decay_f, ret_decay_b, ret_out_norm, mla_q_norm, mla_w_qb, mla_kv_norm, mla_w_kvb, mla_qk_norm_q, mla_qk_norm_k, even_w_out, odd_w_in, swa_q_norm, swa_k_norm, swa_sink, odd_w_out):
    depth = ffn_norm.shape[0]
    assert x_prompt.shape[1] == x_sample.shape[1]
    lp = FRONT_PAD + N_META + x_prompt.shape[1]
    layers = []
    for layer in range(depth):
        i = layer // 2
        ffn = [_prep_ffn(ffn_norm[layer, s], ffn_w_gate[layer, s], ffn_w_up[layer, s], ffn_w_down[layer, s])
               for s in range(2)]
        if layer % 2 == 0:
            mix = _prep_even(lp, mix_norm[layer], even_w_in[i], ret_decay_f[i], ret_decay_b[i],
                             ret_out_norm[i], mla_q_norm[i], mla_w_qb[i], mla_kv_norm[i], mla_w_kvb[i],
                             mla_qk_norm_q[i], mla_qk_norm_k[i], even_w_out[i])
        else:
            mix = _prep_odd(lp, mix_norm[layer], odd_w_in[i], swa_q_norm[i], swa_k_norm[i],
                            swa_sink[i], odd_w_out[i])
        layers.append(dict(even=layer % 2 == 0, ffn0=ffn[0], ffn1=ffn[1], mix=mix))
    return (_trunk(x_prompt, meta_tokens, layers), _trunk(x_sample, meta_tokens, layers))
```

```python
import functools

import jax
import jax.numpy as jnp
import numpy as np
from jax import lax
from jax.experimental import pallas as pl
from jax.experimental.pallas import tpu as pltpu

F32 = jnp.float32
BF16 = jnp.bfloat16

D_MODEL = 1024
N_META = 16
BLOCK = 128
FRONT_PAD = BLOCK - N_META
EPS = 1e-6
NEG = -1e30
LOG2E = 1.4426950408889634
LANES = 128
HALF = LANES // 2

A_HEADS = 8
A_DK = 64
A_THETA = 10000.0
B_HEADS = 8
B_Q_LORA = 256
B_KV_LORA = 128
B_NOPE = 64
B_ROPE = 32
B_DV = 64
B_QK = B_NOPE + B_ROPE
B_THETA = 10000.0
C_HEADS = 16
C_KV_HEADS = 4
C_DH = 64
C_ROT = 16
C_THETA = 500000.0
C_WINDOW = 128

VMEM_LIMIT = 56 * 1024 * 1024


def _params(*sem):
    return pltpu.CompilerParams(dimension_semantics=sem, vmem_limit_bytes=VMEM_LIMIT)


def _resident(shape):
    zeros = (0,) * len(shape)
    return pl.BlockSpec(shape, lambda *_: zeros, pipeline_mode=pl.Buffered(1))


def _rms_rows(x, gain):
    return x * lax.rsqrt(jnp.mean(x * x, axis=-1, keepdims=True) + EPS) * gain


def _dot(a, b):
    return jnp.dot(a, b, preferred_element_type=F32)


def _dot_nt(a, b):
    return lax.dot_general(a, b, (((1,), (1,)), ((), ())), preferred_element_type=F32)


def _is_pad_row(row, seq):
    return (row >= seq) & (row < seq + FRONT_PAD)


def _positions(lp):
    row = jnp.arange(lp, dtype=F32)
    seq = lp - BLOCK
    return jnp.where(row < seq, row + N_META, row - (seq + FRONT_PAD))


def _pick(n, options):
    for o in options:
        if n % o == 0:
            return o
    raise ValueError(f"no tile in {options} divides {n}")


def _ffn_core(x, g_ref, wg_ref, wu_ref, wd_ref):
    xn = _rms_rows(x, g_ref[...]).astype(BF16)
    acc = jnp.zeros_like(x)
    for c in range(wg_ref.shape[0]):
        gate = _dot(xn, wg_ref[c])
        up = _dot(xn, wu_ref[c])
        act = (gate * (1.0 / (1.0 + jnp.exp(-gate))) * up).astype(BF16)
        acc = acc + _dot(act, wd_ref[c])
    return x + 0.5 * acc


def _ffn_body(h_ref, g_ref, wg_ref, wu_ref, wd_ref, o_ref):
    o_ref[...] = _ffn_core(h_ref[...], g_ref, wg_ref, wu_ref, wd_ref)


def _ffn_head_body(x_ref, tail_ref, g_ref, wg_ref, wu_ref, wd_ref, o_ref, *, nt):
    t = pl.program_id(1)

    @pl.when(t < nt)
    def _():
        o_ref[0] = _ffn_core(x_ref[0], g_ref, wg_ref, wu_ref, wd_ref)

    @pl.when(t == nt)
    def _():
        o_ref[0, :BLOCK, :] = tail_ref[...]


def _proj_ffn_body(*refs, n_in):
    tile = lambda r: r[...] if len(r.shape) == 2 else r[0]
    h_ref = refs[0]
    xs = refs[1:1 + n_in]
    ws = refs[1 + n_in:1 + 2 * n_in]
    g_ref, wg_ref, wu_ref, wd_ref, o_ref = refs[1 + 2 * n_in:]
    h2 = tile(h_ref)
    for x_ref, w_ref in zip(xs, ws):
        h2 = h2 + _dot(tile(x_ref), w_ref[...])
    out = _ffn_core(h2, g_ref, wg_ref, wu_ref, wd_ref)
    if len(o_ref.shape) == 2:
        o_ref[...] = out
    else:
        o_ref[0] = out


def _ffn_specs(ffn):
    return [_resident(a.shape) for a in ffn]


def _ffn(h2d, ffn):
    t, d = h2d.shape
    tm = _pick(t, (512, 384, 256, 128))
    row = pl.BlockSpec((tm, d), lambda i: (i, 0))
    return pl.pallas_call(
        _ffn_body,
        grid=(t // tm,),
        in_specs=[row] + _ffn_specs(ffn),
        out_specs=row,
        out_shape=jax.ShapeDtypeStruct((t, d), F32),
        compiler_params=_params("parallel"),
        name="ffn",
    )(h2d, *ffn)


def _ffn_head(x, tail, ffn):
    b, seq, d = x.shape
    tm = _pick(seq, (512, 256, 128))
    nt = seq // tm
    return pl.pallas_call(
        functools.partial(_ffn_head_body, nt=nt),
        grid=(b, nt + 1),
        in_specs=[pl.BlockSpec((1, tm, d), lambda i, t: (i, jnp.minimum(t, nt - 1), 0)),
                  pl.BlockSpec((BLOCK, d), lambda i, t: (0, 0))] + _ffn_specs(ffn),
        out_specs=pl.BlockSpec((1, tm, d), lambda i, t: (i, t, 0)),
        out_shape=jax.ShapeDtypeStruct((b, seq + BLOCK, d), F32),
        compiler_params=_params("parallel", "arbitrary"),
        name="ffn_head",
    )(x, tail, *ffn)


def _proj_ffn_tail(h, xs, ws, ffn):
    b, lp, d = h.shape
    seq = lp - BLOCK
    tm = _pick(seq, (512, 256, 128))
    tok = lambda w: pl.BlockSpec((1, tm, w), lambda i, t: (i, t, 0))
    return pl.pallas_call(
        functools.partial(_proj_ffn_body, n_in=len(xs)),
        grid=(b, seq // tm),
        in_specs=[tok(d)] + [tok(x.shape[-1]) for x in xs] + [_resident(w.shape) for w in ws]
        + _ffn_specs(ffn),
        out_specs=tok(d),
        out_shape=jax.ShapeDtypeStruct((b, seq, d), F32),
        compiler_params=_params("parallel", "parallel"),
        name="proj_ffn_tail",
    )(h, *xs, *ws, *ffn)


def _proj_ffn(h2d, xs, ws, ffn):
    t, d = h2d.shape
    tm = _pick(t, (512, 384, 256, 128))
    row = pl.BlockSpec((tm, d), lambda i: (i, 0))
    x_specs = [pl.BlockSpec((tm, x.shape[1]), lambda i: (i, 0)) for x in xs]
    w_specs = [_resident(w.shape) for w in ws]
    return pl.pallas_call(
        functools.partial(_proj_ffn_body, n_in=len(xs)),
        grid=(t // tm,),
        in_specs=[row] + x_specs + w_specs + _ffn_specs(ffn),
        out_specs=row,
        out_shape=jax.ShapeDtypeStruct((t, d), F32),
        compiler_params=_params("parallel"),
        name="proj_ffn",
    )(h2d, *xs, *ws, *ffn)


def _rope_pairs(y, cos, sin_lo, sin_hi, half):
    n = y.shape[-1]
    return y * cos + pltpu.roll(y, n - half, 1) * sin_lo + pltpu.roll(y, half, 1) * sin_hi


def _even_in_body(h_ref, g_ref, win_ref, qn_ref, wqb_ref, wqr_ref, kvn_ref, wkb_ref, wvb_ref,
                  gq_ref, gk_ref, rtab_ref, mtab_ref,
                  rq_ref, rk_ref, rv_ref, rg_ref, mq_ref, mk_ref, mv_ref, *, tm, seq):
    x = h_ref[0]
    hn = _rms_rows(x, g_ref[...]).astype(BF16)
    z = _dot(hn, win_ref[...])
    o = B_Q_LORA + B_KV_LORA
    cq = _rms_rows(z[:, :B_Q_LORA], qn_ref[...]).astype(BF16)
    ckv = _rms_rows(z[:, B_Q_LORA:o], kvn_ref[...]).astype(BF16)
    kr = z[:, o:o + LANES]
    kr_rot = z[:, o + LANES:o + 2 * LANES]
    q_all = _dot(cq, wqb_ref[...])
    q_rot = _dot(cq, wqr_ref[...])
    k_all = _dot(ckv, wkb_ref[...])
    v_all = _dot(ckv, wvb_ref[...])

    mcos, msin = mtab_ref[0], mtab_ref[1]
    lane = lax.broadcasted_iota(jnp.int32, (tm, LANES), 1)
    v_ones = jnp.where(lane >= HALF, 1.0, 0.0)
    row = lax.broadcasted_iota(jnp.int32, (tm, LANES), 0) + pl.program_id(1) * tm
    bias_lane = lane == B_QK
    q_bias = jnp.where(bias_lane, 1.0, 0.0)
    k_bias = jnp.where(bias_lane & _is_pad_row(row, seq), NEG, 0.0)
    inv = 1.0 / B_QK
    gq, gq_rot, gk, gk_rot = gq_ref[0:1], gq_ref[1:2], gk_ref[0:1], gk_ref[1:2]
    for hd in range(B_HEADS):
        sl = slice(hd * LANES, (hd + 1) * LANES)
        qh = q_all[:, sl]
        r = lax.rsqrt(jnp.sum(qh * qh, axis=-1, keepdims=True) * inv + EPS)
        qh = (qh * r * gq) * mcos + (q_rot[:, sl] * r * gq_rot) * msin
        mq_ref[0, :, sl] = (qh + q_bias).astype(BF16)
        kh = k_all[:, sl] + kr
        r = lax.rsqrt(jnp.sum(kh * kh, axis=-1, keepdims=True) * inv + EPS)
        kh = (kh * r * gk) * mcos + (kr_rot * r * gk_rot) * msin
        mk_ref[0, :, sl] = (kh + k_bias).astype(BF16)
        mv_ref[0, :, sl] = (v_all[:, sl] + v_ones).astype(BF16)

    hw = A_HEADS * A_DK
    o += 2 * LANES
    rcos, rsin = rtab_ref[0], rtab_ref[1]
    for c in range(hw // LANES):
        sl = slice(c * LANES, (c + 1) * LANES)
        q = z[:, o + c * LANES:o + (c + 1) * LANES]
        k = z[:, o + hw + c * LANES:o + hw + (c + 1) * LANES]
        rq_ref[0, :, sl] = (q * rcos + pltpu.roll(q, HALF, 1) * rsin).astype(BF16)
        rk_ref[0, :, sl] = ((k * rcos + pltpu.roll(k, HALF, 1) * rsin) * (A_DK ** -0.5)).astype(BF16)
    rv_ref[0] = z[:, o + 2 * hw:o + 3 * hw].astype(BF16)
    ga = z[:, o + 3 * hw:o + 4 * hw]
    rg_ref[0] = (ga * (1.0 / (1.0 + jnp.exp(-ga)))).astype(BF16)


def _even_in(h, p):
    b, lp, d = h.shape
    tm = _pick(lp, (384, 128))
    tok = lambda w: pl.BlockSpec((1, tm, w), lambda i, j: (i, j, 0))
    tab = pl.BlockSpec((2, tm, LANES), lambda i, j: (0, j, 0))
    consts = [p["mix_g"], p["w_in"], p["q_norm"], p["w_qb"], p["w_qr"], p["kv_norm"], p["w_kb"], p["w_vb"],
              p["gq"], p["gk"]]
    hw = A_HEADS * A_DK
    widths = [hw, hw, hw, hw, B_HEADS * LANES, B_HEADS * LANES, B_HEADS * LANES]
    return pl.pallas_call(
        functools.partial(_even_in_body, tm=tm, seq=lp - BLOCK),
        grid=(b, lp // tm),
        in_specs=[tok(d)] + [_resident(c.shape) for c in consts] + [tab, tab],
        out_specs=[tok(w) for w in widths],
        out_shape=[jax.ShapeDtypeStruct((b, lp, w), BF16) for w in widths],
        compiler_params=_params("parallel", "parallel"),
        name="even_in",
    )(h, *consts, p["ret_tab"], p["mla_tab"])


def _ret_body(q_ref, k_ref, v_ref, g_ref, dmat_ref, vec_ref, gain_ref, o_ref, acc_ref, kv_ref, st_ref,
              *, nb):
    lane = lax.broadcasted_iota(jnp.int32, (BLOCK, LANES), 1)
    first = lane < HALF
    wkf, wkb, qf, qb = vec_ref[0, 0], vec_ref[0, 1], vec_ref[0, 2], vec_ref[0, 3]
    cf, cb = vec_ref[0, 4][:1], vec_ref[0, 5][:1]
    dcat = jnp.concatenate([dmat_ref[0], dmat_ref[1]], axis=0)
    qk_first = lane % HALF < HALF // 2
    keep0 = jnp.where(qk_first, 1.0, 0.0).astype(BF16)
    keep1 = jnp.where(qk_first, 0.0, 1.0).astype(BF16)
    row2 = lax.broadcasted_iota(jnp.int32, (2 * LANES, LANES), 0) % HALF
    col2 = lax.broadcasted_iota(jnp.int32, (2 * LANES, LANES), 1)
    same2 = (row2 < HALF // 2) == (col2 < HALF)
    unroll = next(u for u in (11, 3, 1) if nb % u == 0)

    def intra(n, carry):
        rows = pl.ds(pl.multiple_of(n * BLOCK, BLOCK), BLOCK)
        q, k, v = q_ref[0, rows, :], k_ref[0, rows, :], v_ref[0, rows, :]
        qs = jnp.concatenate([q * keep0, q * keep1], axis=0)
        s = (_dot_nt(qs, k) * dcat).astype(BF16)
        o2 = _dot(s, v)
        acc_ref[rows, :] = jnp.where(first, o2[:BLOCK], o2[BLOCK:])
        kf = k.astype(F32)
        kw = jnp.concatenate([kf * wkf, kf * wkb], axis=1).T.astype(BF16)
        kv_ref[n] = jnp.where(same2, _dot(kw, v), 0.0)
        return carry

    lax.fori_loop(0, nb, intra, 0, unroll=unroll)

    def scan(t, carry):
        sf, sb = carry
        i = lax.rem(t + nb - 1, nb)
        j = lax.rem(2 * nb - 2 - t, nb)
        st_ref[i, :LANES, :] = sf.astype(BF16)
        st_ref[j, LANES:, :] = sb.astype(BF16)
        return sf * cf + kv_ref[i, :LANES, :], sb * cb + kv_ref[j, LANES:, :]

    zero = jnp.zeros((LANES, LANES), F32)
    lax.fori_loop(0, nb, scan, (zero, zero))

    def inter(n, carry):
        rows = pl.ds(pl.multiple_of(n * BLOCK, BLOCK), BLOCK)
        qf32 = q_ref[0, rows, :].astype(F32)
        qq = jnp.concatenate([qf32 * qf, qf32 * qb], axis=1).astype(BF16)
        o = acc_ref[rows, :] + _dot(qq, st_ref[n])
        sq = o * o
        inv = 1.0 / HALF
        r0 = lax.rsqrt(jnp.sum(jnp.where(first, sq, 0.0), axis=-1, keepdims=True) * inv + EPS)
        r1 = lax.rsqrt(jnp.sum(jnp.where(first, 0.0, sq), axis=-1, keepdims=True) * inv + EPS)
        out = o * jnp.where(first, r0, r1) * gain_ref[0] * g_ref[0, rows, :].astype(F32)
        o_ref[0, rows, :] = out.astype(BF16)
        return carry

    lax.fori_loop(0, nb, inter, 0, unroll=unroll)


def _retention(rq, rk, rv, rg, p):
    b, lp, hw = rq.shape
    pairs = hw // LANES
    nb = lp // BLOCK
    seq = pl.BlockSpec((1, lp, LANES), lambda i, j: (i, 0, j))
    return pl.pallas_call(
        functools.partial(_ret_body, nb=nb),
        grid=(b, pairs),
        in_specs=[seq, seq, seq, seq,
                  pl.BlockSpec((2, BLOCK, BLOCK), lambda i, j: (j, 0, 0)),
                  pl.BlockSpec((1, 6, BLOCK, LANES), lambda i, j: (j, 0, 0, 0)),
                  pl.BlockSpec((1, 1, LANES), lambda i, j: (j, 0, 0))],
        out_specs=seq,
        out_shape=jax.ShapeDtypeStruct((b, lp, hw), BF16),
        scratch_shapes=[pltpu.VMEM((lp, LANES), F32), pltpu.VMEM((nb, 2 * LANES, LANES), F32),
                        pltpu.VMEM((nb, 2 * LANES, LANES), BF16)],
        compiler_params=_params("parallel", "parallel"),
        name="retention",
    )(rq, rk, rv, rg, p["ret_dmat"], p["ret_vec"], p["ret_gain"])


MXU_TILE = 256
MLA_ROWS = 16


def _mla_body(q_ref, k_ref, v_ref, o_ref, s_scr, p_scr, o_scr, *, tq):
    lp = k_ref.shape[1]
    chunks = [(c, min(MXU_TILE, lp - c)) for c in range(0, lp, MXU_TILE)]
    hk = lp // 2 // MXU_TILE * MXU_TILE

    def scores(hd):
        q = q_ref[0, :, hd * LANES:(hd + 1) * LANES]
        for c, w in chunks:
            s_scr[hd, :, c:c + w] = _dot_nt(q, k_ref[0, c:c + w, hd * LANES:(hd + 1) * LANES])

    def probs(hd):
        for r in range(0, tq, MLA_ROWS):
            sb = s_scr[hd, r:r + MLA_ROWS, :]
            m = jnp.max(sb, axis=-1, keepdims=True)
            p_scr[hd, r:r + MLA_ROWS, :] = jnp.exp2(sb - m).astype(BF16)

    def weighted(hd):
        v = v_ref[0, :, hd * LANES:(hd + 1) * LANES]
        if hk == 0:
            return _dot(p_scr[hd], v)
        return _dot(p_scr[hd, :, :hk], v[:hk]) + _dot(p_scr[hd, :, hk:], v[hk:])

    scores(0)
    probs(0)
    scores(1)
    o_scr[...] = weighted(0)
    probs(1)
    oa = o_scr[...]
    ob = weighted(1)
    lane = lax.broadcasted_iota(jnp.int32, (tq, LANES), 1)
    row = lax.broadcasted_iota(jnp.int32, (tq, LANES), 0) + pl.program_id(2) * tq
    out = jnp.where(lane < HALF, oa * pltpu.roll(1.0 / oa, HALF, 1),
                    pltpu.roll(ob, HALF, 1) * (1.0 / ob))
    o_ref[0] = jnp.where(_is_pad_row(row, lp - BLOCK), 0.0, out).astype(BF16)


def _mla(mq, mk, mv):
    b, lp, _ = mq.shape
    pairs = B_HEADS // 2
    tq = _pick(lp, (384, 128))
    return pl.pallas_call(
        functools.partial(_mla_body, tq=tq),
        grid=(b, pairs, lp // tq),
        in_specs=[pl.BlockSpec((1, tq, 2 * LANES), lambda i, j, t: (i, t, j)),
                  pl.BlockSpec((1, lp, 2 * LANES), lambda i, j, t: (i, 0, j)),
                  pl.BlockSpec((1, lp, 2 * LANES), lambda i, j, t: (i, 0, j))],
        out_specs=pl.BlockSpec((1, tq, LANES), lambda i, j, t: (i, t, j)),
        out_shape=jax.ShapeDtypeStruct((b, lp, B_HEADS * B_DV), BF16),
        scratch_shapes=[pltpu.VMEM((2, tq, lp), F32), pltpu.VMEM((2, tq, lp), BF16),
                        pltpu.VMEM((tq, LANES), F32)],
        compiler_params=_params("parallel", "parallel", "parallel"),
        name="mla",
    )(mq, mk, mv)


def _head_rms(x, first, gain):
    sq = x * x
    inv = 1.0 / HALF
    r0 = lax.rsqrt(jnp.sum(jnp.where(first, sq, 0.0), axis=-1, keepdims=True) * inv + EPS)
    r1 = lax.rsqrt(jnp.sum(jnp.where(first, 0.0, sq), axis=-1, keepdims=True) * inv + EPS)
    return x * jnp.where(first, r0, r1) * gain


def _odd_in_body(h_ref, g_ref, win_ref, gq_ref, gk_ref, tab_ref, q_ref, k_ref, vt_ref, *, tm):
    x = h_ref[0]
    hn = _rms_rows(x, g_ref[...]).astype(BF16)
    z = _dot(hn, win_ref[...])
    cos, lo, hi = tab_ref[0], tab_ref[1], tab_ref[2]
    first = lax.broadcasted_iota(jnp.int32, (tm, LANES), 1) < HALF
    nq = C_HEADS * C_DH
    nk = 2 * C_KV_HEADS * C_DH
    for c in range(nq // LANES):
        sl = slice(c * LANES, (c + 1) * LANES)
        y = _head_rms(z[:, sl], first, gq_ref[...])
        q_ref[0, :, sl] = _rope_pairs(y, cos, lo, hi, C_ROT // 2).astype(BF16)
    for c in range(nk // LANES):
        sl = slice(c * LANES, (c + 1) * LANES)
        y = _head_rms(z[:, nq + c * LANES:nq + (c + 1) * LANES], first, gk_ref[...])
        k_ref[0, :, sl] = _rope_pairs(y, cos, lo, hi, C_ROT // 2).astype(BF16)
    vt_ref[0] = z[:, nq + nk:].T.astype(BF16)


def _odd_in(h, p):
    b, lp, d = h.shape
    tm = _pick(lp, (384, 128))
    tok = lambda w: pl.BlockSpec((1, tm, w), lambda i, j: (i, j, 0))
    tab = pl.BlockSpec((3, tm, LANES), lambda i, j: (0, j, 0))
    consts = [p["mix_g"], p["w_in"], p["gq"], p["gk"]]
    nq, nk, nv = C_HEADS * C_DH, 2 * C_KV_HEADS * C_DH, C_KV_HEADS * C_DH
    return pl.pallas_call(
        functools.partial(_odd_in_body, tm=tm),
        grid=(b, lp // tm),
        in_specs=[tok(d)] + [_resident(c.shape) for c in consts] + [tab],
        out_specs=[tok(nq), tok(nk), pl.BlockSpec((1, nv, tm), lambda i, j: (i, 0, j))],
        out_shape=[jax.ShapeDtypeStruct((b, lp, nq), BF16), jax.ShapeDtypeStruct((b, lp, nk), BF16),
                   jax.ShapeDtypeStruct((b, nv, lp), BF16)],
        compiler_params=_params("parallel", "parallel"),
        name="odd_in",
    )(h, *consts, p["swa_tab"])


def _swa_body(sink_ref, q_ref, km_ref, kp_ref, kc_ref, kn_ref, vm_ref, vp_ref, vc_ref, vn_ref,
              o_ref, *, nb):
    stored = pl.program_id(1)
    n = jnp.where(stored == nb - 1, 0, stored + 1)
    nkeys = 4 * BLOCK
    group = C_HEADS // C_KV_HEADS
    kk = lax.broadcasted_iota(jnp.int32, (nkeys, BLOCK), 0)
    qq = lax.broadcasted_iota(jnp.int32, (nkeys, BLOCK), 1)
    q_pos = n * BLOCK + qq - FRONT_PAD
    k_pos = (n - 2) * BLOCK + kk - FRONT_PAD
    seq_len = nb * BLOCK - FRONT_PAD
    band = (kk >= BLOCK) & (k_pos >= N_META) & (k_pos < seq_len) & (jnp.abs(q_pos - k_pos) <= C_WINDOW)
    mask = band | ((kk < BLOCK) & (kk >= FRONT_PAD))
    lane = lax.broadcasted_iota(jnp.int32, (BLOCK, LANES), 1)
    rowi = lax.broadcasted_iota(jnp.int32, (BLOCK, LANES), 0)
    first = lane < HALF
    keep = (jnp.where(first, 1.0, 0.0).astype(BF16), jnp.where(first, 0.0, 1.0).astype(BF16))
    row_ok = rowi + n * BLOCK >= FRONT_PAD
    ones = jnp.ones((C_DH, nkeys), BF16)
    for kv in range(C_KV_HEADS):
        kvs = slice(kv * LANES, (kv + 1) * LANES)
        kcat = jnp.concatenate([r[0, :, kvs] for r in (km_ref, kp_ref, kc_ref, kn_ref)], axis=0)
        vt = jnp.concatenate([r[0, kv * C_DH:(kv + 1) * C_DH, :] for r in (vm_ref, vp_ref, vc_ref, vn_ref)],
                             axis=1)
        vt = jnp.concatenate([vt, ones], axis=0)
        pairs = [q_ref[0, :, (kv * group // 2 + i) * LANES:(kv * group // 2 + i + 1) * LANES]
                 for i in range(group // 2)]
        qs = jnp.concatenate([p2 * keep[w] for p2 in pairs for w in range(2)], axis=0)
        st = _dot_nt(kcat, qs)
        normed = []
        for g in range(group):
            sink = sink_ref[kv * group + g] * LOG2E
            s = jnp.where(mask, st[:, g * BLOCK:(g + 1) * BLOCK], NEG)
            m = jnp.maximum(jnp.max(s, axis=0, keepdims=True), sink)
            ot = _dot(vt, jnp.exp2(s - m).astype(BF16))
            den = ot[C_DH:C_DH + 1, :] + jnp.exp2(sink - m)
            normed.append(ot[:C_DH, :] * (1.0 / den))
        for i in range(group // 2):
            slab = jnp.concatenate([normed[2 * i], normed[2 * i + 1]], axis=0).T
            sl = slice((kv * group // 2 + i) * LANES, (kv * group // 2 + i + 1) * LANES)
            o_ref[0, :, sl] = jnp.where(row_ok, slab, 0.0).astype(BF16)


def _swa(q, k, vt, sink):
    b, lp, qw = q.shape
    nb = lp // BLOCK
    kw = k.shape[-1]
    vw = vt.shape[1]
    meta = nb - 1
    prev = lambda n: jnp.maximum(n - 1, 0)
    nxt = lambda n: jnp.where(n == meta, 0, jnp.minimum(n + 1, meta - 1))
    qspec = pl.BlockSpec((1, BLOCK, qw), lambda i, n: (i, n, 0))
    kspecs = [pl.BlockSpec((1, BLOCK, kw), lambda i, n: (i, meta, 0)),
              pl.BlockSpec((1, BLOCK, kw), lambda i, n: (i, prev(n), 0)),
              pl.BlockSpec((1, BLOCK, kw), lambda i, n: (i, n, 0)),
              pl.BlockSpec((1, BLOCK, kw), lambda i, n: (i, nxt(n), 0))]
    vspecs = [pl.BlockSpec((1, vw, BLOCK), lambda i, n: (i, 0, meta)),
              pl.BlockSpec((1, vw, BLOCK), lambda i, n: (i, 0, prev(n))),
              pl.BlockSpec((1, vw, BLOCK), lambda i, n: (i, 0, n)),
              pl.BlockSpec((1, vw, BLOCK), lambda i, n: (i, 0, nxt(n)))]
    return pl.pallas_call(
        functools.partial(_swa_body, nb=nb),
        grid=(b, nb),
        in_specs=[pl.BlockSpec(memory_space=pltpu.SMEM), qspec] + kspecs + vspecs,
        out_specs=qspec,
        out_shape=jax.ShapeDtypeStruct((b, lp, qw), BF16),
        compiler_params=_params("parallel", "parallel"),
        name="swa",
    )(sink, q, k, k, k, k, vt, vt, vt, vt)


def _rope_table(lp, theta, rot, group, offset, scale=1.0):
    half = rot // 2
    pos = _positions(lp)
    lane = np.arange(LANES) % group - offset
    in_lo = (lane >= 0) & (lane < half)
    in_hi = (lane >= half) & (lane < rot)
    idx = np.where(in_lo | in_hi, lane % half, 0)
    inv = theta ** (-jnp.asarray(idx, F32) * 2.0 / rot)
    ang = pos[:, None] * inv[None, :]
    cos = jnp.where(in_lo | in_hi, jnp.cos(ang), 1.0)
    sin = jnp.sin(ang)
    lo = jnp.where(in_lo, -sin, 0.0)
    hi = jnp.where(in_hi, sin, 0.0)
    return jnp.stack([cos, lo, hi]) * scale


def _pair_lanes(x):
    h, n = x.shape
    return jnp.repeat(x.reshape(h // 2, 2, n).transpose(0, 2, 1), HALF, axis=2)


def _prep_ffn(gain, wg, wu, wd, fc=256):
    d, f = wg.shape
    nc = f // fc
    return (gain.reshape(1, d),
            wg.reshape(d, nc, fc).transpose(1, 0, 2).astype(BF16),
            wu.reshape(d, nc, fc).transpose(1, 0, 2).astype(BF16),
            wd.reshape(nc, fc, d).astype(BF16))


def _prep_even(lp, mix_g, w_in, dec_f, dec_b, ret_norm, q_norm, w_qb, kv_norm, w_kvb, gq, gk, w_out):
    d = w_in.shape[0]
    hw = A_HEADS * A_DK
    cut = 4 * hw + B_Q_LORA + B_KV_LORA
    lanes = np.arange(LANES)
    partner = lanes.copy()
    partner[B_NOPE:B_NOPE + B_ROPE // 2] += B_ROPE // 2
    partner[B_NOPE + B_ROPE // 2:B_QK] -= B_ROPE // 2
    is_rope = jnp.asarray((lanes >= B_NOPE) & (lanes < B_QK))
    rot = lambda w: jnp.where(is_rope, w[..., partner], 0.0)
    kr_cols = jnp.concatenate([jnp.zeros((d, B_NOPE), F32), w_in[:, cut:],
                               jnp.zeros((d, LANES - B_QK), F32)], axis=1)
    qk_perm = lambda w: w.reshape(d, A_HEADS // 2, 2, 2, A_DK // 2).transpose(0, 1, 3, 2, 4).reshape(d, hw)
    w_in2 = jnp.concatenate([w_in[:, 4 * hw:cut], kr_cols, rot(kr_cols),
                             qk_perm(w_in[:, :hw]), qk_perm(w_in[:, hw:2 * hw]), w_in[:, 2 * hw:4 * hw]],
                            axis=1).astype(BF16)
    w_qb2 = jnp.pad(w_qb.reshape(B_Q_LORA, B_HEADS, B_QK), ((0, 0), (0, 0), (0, LANES - B_QK)))
    w_kv3 = w_kvb.reshape(B_KV_LORA, B_HEADS, B_NOPE + B_DV)
    w_kb = jnp.pad(w_kv3[:, :, :B_NOPE], ((0, 0), (0, 0), (0, LANES - B_NOPE)))
    w_vb = jnp.pad(w_kv3[:, :, B_NOPE:], ((0, 0), (0, 0), (0, LANES - B_DV)))
    pad96 = lambda g: jnp.pad(g, (0, LANES - B_QK)).reshape(1, LANES)
    with_rot = lambda g: jnp.concatenate([g, rot(g)], axis=0)
    mla_tab = _rope_table(lp, B_THETA, B_ROPE, LANES, B_NOPE)
    pos = _positions(lp)
    ang = pos[:, None] * (A_THETA ** (-jnp.asarray(lanes % (A_DK // 2), F32) * 2.0 / A_DK))[None, :]
    ret_tab = jnp.stack([jnp.cos(ang), jnp.where(jnp.asarray(lanes < HALF), -1.0, 1.0) * jnp.sin(ang)])
    qk_lanes = lambda x: _pair_lanes(x).reshape(x.shape[0] // 2, x.shape[1], 2, 2, A_DK // 2
                                                ).transpose(0, 1, 3, 2, 4).reshape(-1, x.shape[1], LANES)

    lgf = -jnp.exp(dec_f.astype(F32))
    lgb = -jnp.exp(dec_b.astype(F32))
    idx = jnp.arange(BLOCK, dtype=F32)
    diff = idx[:, None] - idx[None, :]
    dmat = (jnp.where(diff >= 0, jnp.exp(lgf[:, None, None] * jnp.maximum(diff, 0.0)), 0.0)
            + jnp.where(diff < 0, jnp.exp(lgb[:, None, None] * jnp.maximum(-diff, 0.0)), 0.0))
    ones = jnp.ones((BLOCK,), F32)
    vec = jnp.stack([
        qk_lanes(jnp.exp(lgf[:, None] * (BLOCK - 1 - idx)[None, :])),
        qk_lanes(jnp.exp(lgb[:, None] * idx[None, :])),
        qk_lanes(jnp.exp(lgf[:, None] * (idx + 1.0)[None, :])),
        qk_lanes(jnp.exp(lgb[:, None] * (BLOCK - idx)[None, :])),
        _pair_lanes(jnp.exp(BLOCK * lgf)[:, None] * ones[None, :]),
        _pair_lanes(jnp.exp(BLOCK * lgb)[:, None] * ones[None, :]),
    ], axis=1)
    half = A_HEADS * A_DK
    return dict(
        mix_g=mix_g.reshape(1, d), w_in=w_in2,
        q_norm=q_norm.reshape(1, -1), w_qb=w_qb2.reshape(B_Q_LORA, -1).astype(BF16),
        w_qr=rot(w_qb2).reshape(B_Q_LORA, -1).astype(BF16),
        kv_norm=kv_norm.reshape(1, -1), w_kb=w_kb.reshape(B_KV_LORA, -1).astype(BF16),
        w_vb=w_vb.reshape(B_KV_LORA, -1).astype(BF16),
        gq=with_rot(pad96(gq) * (B_QK ** -0.5 * LOG2E)), gk=with_rot(pad96(gk)),
        ret_tab=ret_tab,
        mla_tab=jnp.stack([mla_tab[0], mla_tab[1] + mla_tab[2]]),
        ret_dmat=dmat, ret_vec=vec,
        ret_gain=ret_norm.reshape(A_HEADS // 2, 1, LANES),
        w_out=(w_out[:half].astype(BF16), w_out[half:].astype(BF16)),
    )


def _prep_odd(lp, mix_g, w_in, gq, gk, sink, w_out):
    d = w_in.shape[0]
    nq = C_HEADS * C_DH
    nkv = C_KV_HEADS * C_DH
    dup = lambda w: jnp.repeat(w.reshape(d, C_KV_HEADS, 1, C_DH), 2, axis=2).reshape(d, 2 * nkv)
    w_in2 = jnp.concatenate([w_in[:, :nq], dup(w_in[:, nq:nq + nkv]), w_in[:, nq + nkv:]], axis=1)
    two = lambda g: jnp.concatenate([g, g]).reshape(1, LANES)
    return dict(
        mix_g=mix_g.reshape(1, d), w_in=w_in2.astype(BF16),
        gq=two(gq) * (C_DH ** -0.5 * LOG2E), gk=two(gk),
        swa_tab=_rope_table(lp, C_THETA, C_ROT, C_DH, 0),
        sink=sink.astype(F32), w_out=(w_out.astype(BF16),),
    )


def _trunk(x, meta, layers):
    b, seq, d = x.shape
    lp = seq + BLOCK
    tail = jnp.concatenate([jnp.zeros((FRONT_PAD, d), x.dtype), meta.astype(x.dtype)], axis=0)
    tail = _ffn(tail, layers[0]["ffn0"])
    h = _ffn_head(x, tail, layers[0]["ffn0"]).reshape(b * lp, d)
    for i, layer in enumerate(layers):
        h3 = h.reshape(b, lp, d)
        mp = layer["mix"]
        if layer["even"]:
            rq, rk, rv, rg, mq, mk, mv = _even_in(h3, mp)
            xs = [_retention(rq, rk, rv, rg, mp), _mla(mq, mk, mv)]
        else:
            q, k, v = _odd_in(h3, mp)
            xs = [_swa(q, k, v, mp["sink"])]
        if i + 1 == len(layers):
            return _proj_ffn_tail(h3, xs, mp["w_out"], layer["ffn1"])
        xs = [a.reshape(b * lp, a.shape[-1]) for a in xs]
        h = _proj_ffn(h, xs, mp["w_out"], layer["ffn1"])
        h = _ffn(h, layers[i + 1]["ffn0"])


def kernel(x_prompt, x_sample, meta_tokens, ffn_norm, ffn_w_gate, ffn_w_up, ffn_w_down, mix_norm, even_w_in, ret_decay_f, ret_decay_b, ret_out_norm, mla_q_norm, mla_w_qb, mla_kv_norm, mla_w_kvb, mla_qk_norm_q, mla_qk_norm_k, even_w_out, odd_w_in, swa_q_norm, swa_k_norm, swa_sink, odd_w_out):
    depth = ffn_norm.shape[0]
    assert x_prompt.shape[1] == x_sample.shape[1]
    lp = x_prompt.shape[1] + BLOCK
    layers = []
    for layer in range(depth):
        i = layer // 2
        ffn = [_prep_ffn(ffn_norm[layer, s], ffn_w_gate[layer, s], ffn_w_up[layer, s], ffn_w_down[layer, s])
               for s in range(2)]
        if layer % 2 == 0:
            mix = _prep_even(lp, mix_norm[layer], even_w_in[i], ret_decay_f[i], ret_decay_b[i],
                             ret_out_norm[i], mla_q_norm[i], mla_w_qb[i], mla_kv_norm[i], mla_w_kvb[i],
                             mla_qk_norm_q[i], mla_qk_norm_k[i], even_w_out[i])
        else:
            mix = _prep_odd(lp, mix_norm[layer], odd_w_in[i], swa_q_norm[i], swa_k_norm[i],
                            swa_sink[i], odd_w_out[i])
        layers.append(dict(even=layer % 2 == 0, ffn0=ffn[0], ffn1=ffn[1], mix=mix))
    return (_trunk(x_prompt, meta_tokens, layers), _trunk(x_sample, meta_tokens, layers))
```

```python
import functools

import jax
import jax.numpy as jnp
import numpy as np
from jax import lax
from jax.experimental import pallas as pl
from jax.experimental.pallas import tpu as pltpu

F32 = jnp.float32
BF16 = jnp.bfloat16

D_MODEL = 1024
N_META = 16
BLOCK = 128
FRONT_PAD = BLOCK - N_META
EPS = 1e-6
NEG = -1e30
LOG2E = 1.4426950408889634
LANES = 128
HALF = LANES // 2

A_HEADS = 8
A_DK = 64
A_THETA = 10000.0
B_HEADS = 8
B_Q_LORA = 256
B_KV_LORA = 128
B_NOPE = 64
B_ROPE = 32
B_DV = 64
B_QK = B_NOPE + B_ROPE
B_THETA = 10000.0
C_HEADS = 16
C_KV_HEADS = 4
C_DH = 64
C_ROT = 16
C_THETA = 500000.0
C_WINDOW = 128

VMEM_LIMIT = 56 * 1024 * 1024


def _params(*sem):
    return pltpu.CompilerParams(dimension_semantics=sem, vmem_limit_bytes=VMEM_LIMIT)


def _resident(shape):
    zeros = (0,) * len(shape)
    return pl.BlockSpec(shape, lambda *_: zeros, pipeline_mode=pl.Buffered(1))


def _rms_rows(x, gain):
    return x * lax.rsqrt(jnp.mean(x * x, axis=-1, keepdims=True) + EPS) * gain


def _dot(a, b):
    return jnp.dot(a, b, preferred_element_type=F32)


def _dot_nt(a, b):
    return lax.dot_general(a, b, (((1,), (1,)), ((), ())), preferred_element_type=F32)


def _is_pad_row(row, seq):
    return (row >= seq) & (row < seq + FRONT_PAD)


def _positions(lp):
    row = jnp.arange(lp, dtype=F32)
    seq = lp - BLOCK
    return jnp.where(row < seq, row + N_META, row - (seq + FRONT_PAD))


def _pick(n, options):
    for o in options:
        if n % o == 0:
            return o
    raise ValueError(f"no tile in {options} divides {n}")


def _ffn_core(x, g_ref, wg_ref, wu_ref, wd_ref):
    xn = _rms_rows(x, g_ref[...]).astype(BF16)
    acc = jnp.zeros_like(x)
    for c in range(wg_ref.shape[0]):
        gate = _dot(xn, wg_ref[c])
        up = _dot(xn, wu_ref[c])
        act = (gate * (1.0 / (1.0 + jnp.exp(-gate))) * up).astype(BF16)
        acc = acc + _dot(act, wd_ref[c])
    return x + 0.5 * acc


def _ffn_body(h_ref, g_ref, wg_ref, wu_ref, wd_ref, o_ref):
    o_ref[...] = _ffn_core(h_ref[...], g_ref, wg_ref, wu_ref, wd_ref)


def _ffn_head_body(x_ref, tail_ref, g_ref, wg_ref, wu_ref, wd_ref, o_ref, *, nt):
    t = pl.program_id(1)

    @pl.when(t < nt)
    def _():
        o_ref[0] = _ffn_core(x_ref[0], g_ref, wg_ref, wu_ref, wd_ref)

    @pl.when(t == nt)
    def _():
        o_ref[0, :BLOCK, :] = tail_ref[...]


def _proj_ffn_body(*refs, n_in):
    tile = lambda r: r[...] if len(r.shape) == 2 else r[0]
    h_ref = refs[0]
    xs = refs[1:1 + n_in]
    ws = refs[1 + n_in:1 + 2 * n_in]
    g_ref, wg_ref, wu_ref, wd_ref, o_ref = refs[1 + 2 * n_in:]
    h2 = tile(h_ref)
    for x_ref, w_ref in zip(xs, ws):
        h2 = h2 + _dot(tile(x_ref), w_ref[...])
    out = _ffn_core(h2, g_ref, wg_ref, wu_ref, wd_ref)
    if len(o_ref.shape) == 2:
        o_ref[...] = out
    else:
        o_ref[0] = out


def _ffn_specs(ffn):
    return [_resident(a.shape) for a in ffn]


def _ffn(h2d, ffn):
    t, d = h2d.shape
    tm = _pick(t, (512, 384, 256, 128))
    row = pl.BlockSpec((tm, d), lambda i: (i, 0))
    return pl.pallas_call(
        _ffn_body,
        grid=(t // tm,),
        in_specs=[row] + _ffn_specs(ffn),
        out_specs=row,
        out_shape=jax.ShapeDtypeStruct((t, d), F32),
        compiler_params=_params("parallel"),
        name="ffn",
    )(h2d, *ffn)


def _ffn_head(x, tail, ffn):
    b, seq, d = x.shape
    tm = _pick(seq, (512, 256, 128))
    nt = seq // tm
    return pl.pallas_call(
        functools.partial(_ffn_head_body, nt=nt),
        grid=(b, nt + 1),
        in_specs=[pl.BlockSpec((1, tm, d), lambda i, t: (i, jnp.minimum(t, nt - 1), 0)),
                  pl.BlockSpec((BLOCK, d), lambda i, t: (0, 0))] + _ffn_specs(ffn),
        out_specs=pl.BlockSpec((1, tm, d), lambda i, t: (i, t, 0)),
        out_shape=jax.ShapeDtypeStruct((b, seq + BLOCK, d), F32),
        compiler_params=_params("parallel", "arbitrary"),
        name="ffn_head",
    )(x, tail, *ffn)


def _proj_ffn_tail(h, xs, ws, ffn):
    b, lp, d = h.shape
    seq = lp - BLOCK
    tm = _pick(seq, (512, 256, 128))
    tok = lambda w: pl.BlockSpec((1, tm, w), lambda i, t: (i, t, 0))
    return pl.pallas_call(
        functools.partial(_proj_ffn_body, n_in=len(xs)),
        grid=(b, seq // tm),
        in_specs=[tok(d)] + [tok(x.shape[-1]) for x in xs] + [_resident(w.shape) for w in ws]
        + _ffn_specs(ffn),
        out_specs=tok(d),
        out_shape=jax.ShapeDtypeStruct((b, seq, d), F32),
        compiler_params=_params("parallel", "parallel"),
        name="proj_ffn_tail",
    )(h, *xs, *ws, *ffn)


def _proj_ffn(h2d, xs, ws, ffn):
    t, d = h2d.shape
    tm = _pick(t, (512, 384, 256, 128))
    row = pl.BlockSpec((tm, d), lambda i: (i, 0))
    x_specs = [pl.BlockSpec((tm, x.shape[1]), lambda i: (i, 0)) for x in xs]
    w_specs = [_resident(w.shape) for w in ws]
    return pl.pallas_call(
        functools.partial(_proj_ffn_body, n_in=len(xs)),
        grid=(t // tm,),
        in_specs=[row] + x_specs + w_specs + _ffn_specs(ffn),
        out_specs=row,
        out_shape=jax.ShapeDtypeStruct((t, d), F32),
        compiler_params=_params("parallel"),
        name="proj_ffn",
    )(h2d, *xs, *ws, *ffn)


def _even_in_body(h_ref, g_ref, win_ref, qn_ref, wqb_ref, wqr_ref, kvn_ref, wkb_ref, wvb_ref,
                  gq_ref, gk_ref, rtab_ref, mtab_ref,
                  rq_ref, rk_ref, rv_ref, rg_ref, mq_ref, mk_ref, mv_ref, *, tm, seq):
    x = h_ref[0]
    hn = _rms_rows(x, g_ref[...]).astype(BF16)
    z = _dot(hn, win_ref[...])
    o = B_Q_LORA + B_KV_LORA
    cq = _rms_rows(z[:, :B_Q_LORA], qn_ref[...]).astype(BF16)
    ckv = _rms_rows(z[:, B_Q_LORA:o], kvn_ref[...]).astype(BF16)
    kr = z[:, o:o + LANES]
    kr_rot = z[:, o + LANES:o + 2 * LANES]
    q_all = _dot(cq, wqb_ref[...])
    q_rot = _dot(cq, wqr_ref[...])
    k_all = _dot(ckv, wkb_ref[...])
    v_all = _dot(ckv, wvb_ref[...])

    mcos, msin = mtab_ref[0], mtab_ref[1]
    lane = lax.broadcasted_iota(jnp.int32, (tm, LANES), 1)
    v_ones = jnp.where(lane >= HALF, 1.0, 0.0)
    row = lax.broadcasted_iota(jnp.int32, (tm, LANES), 0) + pl.program_id(1) * tm
    bias_lane = lane == B_QK
    q_bias = jnp.where(bias_lane, 1.0, 0.0)
    k_bias = jnp.where(bias_lane & _is_pad_row(row, seq), NEG, 0.0)
    inv = 1.0 / B_QK
    qcos, qsin = mcos * gq_ref[0:1], msin * gq_ref[1:2]
    kcos, ksin = mcos * gk_ref[0:1], msin * gk_ref[1:2]
    kr_sin = kr_rot * ksin
    for hd in range(B_HEADS):
        sl = slice(hd * LANES, (hd + 1) * LANES)
        qh = q_all[:, sl]
        r = lax.rsqrt(jnp.sum(qh * qh, axis=-1, keepdims=True) * inv + EPS)
        mq_ref[0, :, sl] = ((qh * qcos + q_rot[:, sl] * qsin) * r + q_bias).astype(BF16)
        kh = k_all[:, sl] + kr
        r = lax.rsqrt(jnp.sum(kh * kh, axis=-1, keepdims=True) * inv + EPS)
        mk_ref[0, :, sl] = ((kh * kcos + kr_sin) * r + k_bias).astype(BF16)
        mv_ref[0, :, sl] = (v_all[:, sl] + v_ones).astype(BF16)

    hw = A_HEADS * A_DK
    o += 2 * LANES
    rcos, rsin = rtab_ref[0], rtab_ref[1]
    for c in range(hw // LANES):
        sl = slice(c * LANES, (c + 1) * LANES)
        q = z[:, o + c * LANES:o + (c + 1) * LANES]
        k = z[:, o + hw + c * LANES:o + hw + (c + 1) * LANES]
        rq_ref[0, :, sl] = (q * rcos + pltpu.roll(q, HALF, 1) * rsin).astype(BF16)
        rk_ref[0, :, sl] = ((k * rcos + pltpu.roll(k, HALF, 1) * rsin) * (A_DK ** -0.5)).astype(BF16)
    rv_ref[0] = z[:, o + 2 * hw:o + 3 * hw].astype(BF16)
    ga = z[:, o + 3 * hw:o + 4 * hw]
    rg_ref[0] = (ga * (1.0 / (1.0 + jnp.exp(-ga)))).astype(BF16)


def _even_in(h, p):
    b, lp, d = h.shape
    tm = _pick(lp, (384, 128))
    tok = lambda w: pl.BlockSpec((1, tm, w), lambda i, j: (i, j, 0))
    tab = pl.BlockSpec((2, tm, LANES), lambda i, j: (0, j, 0))
    consts = [p["mix_g"], p["w_in"], p["q_norm"], p["w_qb"], p["w_qr"], p["kv_norm"], p["w_kb"], p["w_vb"],
              p["gq"], p["gk"]]
    hw = A_HEADS * A_DK
    widths = [hw, hw, hw, hw, B_HEADS * LANES, B_HEADS * LANES, B_HEADS * LANES]
    return pl.pallas_call(
        functools.partial(_even_in_body, tm=tm, seq=lp - BLOCK),
        grid=(b, lp // tm),
        in_specs=[tok(d)] + [_resident(c.shape) for c in consts] + [tab, tab],
        out_specs=[tok(w) for w in widths],
        out_shape=[jax.ShapeDtypeStruct((b, lp, w), BF16) for w in widths],
        compiler_params=_params("parallel", "parallel"),
        name="even_in",
    )(h, *consts, p["ret_tab"], p["mla_tab"])


def _ret_body(q_ref, k_ref, v_ref, g_ref, dmat_ref, vec_ref, gain_ref, o_ref, acc_ref, kv_ref, st_ref,
              *, nb):
    lane = lax.broadcasted_iota(jnp.int32, (BLOCK, LANES), 1)
    first = lane < HALF
    wkf, wkb, qf, qb = vec_ref[0, 0], vec_ref[0, 1], vec_ref[0, 2], vec_ref[0, 3]
    cf, cb = vec_ref[0, 4][:1], vec_ref[0, 5][:1]
    dcat = jnp.concatenate([dmat_ref[0], dmat_ref[1]], axis=0)
    qk_first = lane % HALF < HALF // 2
    keep0 = jnp.where(qk_first, 1.0, 0.0).astype(BF16)
    keep1 = jnp.where(qk_first, 0.0, 1.0).astype(BF16)
    row2 = lax.broadcasted_iota(jnp.int32, (2 * LANES, LANES), 0) % HALF
    col2 = lax.broadcasted_iota(jnp.int32, (2 * LANES, LANES), 1)
    same2 = (row2 < HALF // 2) == (col2 < HALF)
    unroll = next(u for u in (11, 3, 1) if nb % u == 0)

    def intra(grp, carry):
        ns = [grp * unroll + u for u in range(unroll)]
        rows = [pl.ds(pl.multiple_of(n * BLOCK, BLOCK), BLOCK) for n in ns]
        qs = [q_ref[0, r, :] for r in rows]
        ks = [k_ref[0, r, :] for r in rows]
        vs = [v_ref[0, r, :] for r in rows]
        ss = [_dot_nt(jnp.concatenate([q * keep0, q * keep1], axis=0), k) for q, k in zip(qs, ks)]
        kws = [jnp.concatenate([k.astype(F32) * wkf, k.astype(F32) * wkb], axis=1).T.astype(BF16)
               for k in ks]
        kvs = [_dot(kw, v) for kw, v in zip(kws, vs)]
        ss = [(s * dcat).astype(BF16) for s in ss]
        os = [_dot(s, v) for s, v in zip(ss, vs)]
        for n, r, kv, o2 in zip(ns, rows, kvs, os):
            kv_ref[n] = jnp.where(same2, kv, 0.0)
            acc_ref[r, :] = jnp.where(first, o2[:BLOCK], o2[BLOCK:])
        return carry

    lax.fori_loop(0, nb // unroll, intra, 0)

    def scan(t, carry):
        sf, sb = carry
        i = lax.rem(t + nb - 1, nb)
        j = lax.rem(2 * nb - 2 - t, nb)
        st_ref[i, :LANES, :] = sf.astype(BF16)
        st_ref[j, LANES:, :] = sb.astype(BF16)
        return sf * cf + kv_ref[i, :LANES, :], sb * cb + kv_ref[j, LANES:, :]

    zero = jnp.zeros((LANES, LANES), F32)
    lax.fori_loop(0, nb, scan, (zero, zero))

    def inter(n, carry):
        rows = pl.ds(pl.multiple_of(n * BLOCK, BLOCK), BLOCK)
        qf32 = q_ref[0, rows, :].astype(F32)
        qq = jnp.concatenate([qf32 * qf, qf32 * qb], axis=1).astype(BF16)
        o = acc_ref[rows, :] + _dot(qq, st_ref[n])
        sq = o * o
        inv = 1.0 / HALF
        r0 = lax.rsqrt(jnp.sum(jnp.where(first, sq, 0.0), axis=-1, keepdims=True) * inv + EPS)
        r1 = lax.rsqrt(jnp.sum(jnp.where(first, 0.0, sq), axis=-1, keepdims=True) * inv + EPS)
        out = o * jnp.where(first, r0, r1) * gain_ref[0] * g_ref[0, rows, :].astype(F32)
        o_ref[0, rows, :] = out.astype(BF16)
        return carry

    lax.fori_loop(0, nb, inter, 0, unroll=unroll)


def _retention(rq, rk, rv, rg, p):
    b, lp, hw = rq.shape
    pairs = hw // LANES
    nb = lp // BLOCK
    seq = pl.BlockSpec((1, lp, LANES), lambda i, j: (i, 0, j))
    return pl.pallas_call(
        functools.partial(_ret_body, nb=nb),
        grid=(b, pairs),
        in_specs=[seq, seq, seq, seq,
                  pl.BlockSpec((2, BLOCK, BLOCK), lambda i, j: (j, 0, 0)),
                  pl.BlockSpec((1, 6, BLOCK, LANES), lambda i, j: (j, 0, 0, 0)),
                  pl.BlockSpec((1, 1, LANES), lambda i, j: (j, 0, 0))],
        out_specs=seq,
        out_shape=jax.ShapeDtypeStruct((b, lp, hw), BF16),
        scratch_shapes=[pltpu.VMEM((lp, LANES), F32), pltpu.VMEM((nb, 2 * LANES, LANES), F32),
                        pltpu.VMEM((nb, 2 * LANES, LANES), BF16)],
        compiler_params=_params("parallel", "parallel"),
        name="retention",
    )(rq, rk, rv, rg, p["ret_dmat"], p["ret_vec"], p["ret_gain"])


MXU_TILE = 256
MLA_ROWS = 16


def _mla_body(q_ref, k_ref, v_ref, o_ref, s_scr, p_scr, o_scr, *, tq):
    lp = k_ref.shape[1]
    chunks = [(c, min(MXU_TILE, lp - c)) for c in range(0, lp, MXU_TILE)]
    hk = lp // 2 // MXU_TILE * MXU_TILE

    def scores(hd):
        q = q_ref[0, :, hd * LANES:(hd + 1) * LANES]
        for c, w in chunks:
            s_scr[hd, :, c:c + w] = _dot_nt(q, k_ref[0, c:c + w, hd * LANES:(hd + 1) * LANES])

    def probs(hd):
        for r in range(0, tq, MLA_ROWS):
            sb = s_scr[hd, r:r + MLA_ROWS, :]
            m = jnp.max(sb, axis=-1, keepdims=True)
            p_scr[hd, r:r + MLA_ROWS, :] = jnp.exp2(sb - m).astype(BF16)

    def weighted(hd):
        v = v_ref[0, :, hd * LANES:(hd + 1) * LANES]
        if hk == 0:
            return _dot(p_scr[hd], v)
        return _dot(p_scr[hd, :, :hk], v[:hk]) + _dot(p_scr[hd, :, hk:], v[hk:])

    scores(0)
    scores(1)
    probs(0)
    o_scr[...] = weighted(0)
    probs(1)
    oa = o_scr[...]
    ob = weighted(1)
    lane = lax.broadcasted_iota(jnp.int32, (tq, LANES), 1)
    row = lax.broadcasted_iota(jnp.int32, (tq, LANES), 0) + pl.program_id(2) * tq
    out = jnp.where(lane < HALF, oa * pltpu.roll(1.0 / oa, HALF, 1),
                    pltpu.roll(ob, HALF, 1) * (1.0 / ob))
    o_ref[0] = jnp.where(_is_pad_row(row, lp - BLOCK), 0.0, out).astype(BF16)


def _mla(mq, mk, mv):
    b, lp, _ = mq.shape
    pairs = B_HEADS // 2
    tq = _pick(lp, (384, 128))
    return pl.pallas_call(
        functools.partial(_mla_body, tq=tq),
        grid=(b, pairs, lp // tq),
        in_specs=[pl.BlockSpec((1, tq, 2 * LANES), lambda i, j, t: (i, t, j)),
                  pl.BlockSpec((1, lp, 2 * LANES), lambda i, j, t: (i, 0, j)),
                  pl.BlockSpec((1, lp, 2 * LANES), lambda i, j, t: (i, 0, j))],
        out_specs=pl.BlockSpec((1, tq, LANES), lambda i, j, t: (i, t, j)),
        out_shape=jax.ShapeDtypeStruct((b, lp, B_HEADS * B_DV), BF16),
        scratch_shapes=[pltpu.VMEM((2, tq, lp), F32), pltpu.VMEM((2, tq, lp), BF16),
                        pltpu.VMEM((tq, LANES), F32)],
        compiler_params=_params("parallel", "parallel", "parallel"),
        name="mla",
    )(mq, mk, mv)


def _odd_in_body(h_ref, g_ref, win_ref, gq_ref, gk_ref, tab_ref, q_ref, k_ref, vt_ref, *, tm):
    x = h_ref[0]
    hn = _rms_rows(x, g_ref[...]).astype(BF16)
    z = _dot(hn, win_ref[...])
    cos, sin = tab_ref[0], tab_ref[1]
    is_a = lax.broadcasted_iota(jnp.int32, (tm, LANES), 1) % HALF < HALF // 2
    nq = C_HEADS * C_DH
    nk = 2 * C_KV_HEADS * C_DH
    inv = 1.0 / C_DH
    rope = lambda y: y * cos + pltpu.roll(y, HALF, 1) * sin
    for c in range(nq // LANES):
        sl = slice(c * LANES, (c + 1) * LANES)
        xq = z[:, sl]
        sq = xq * xq
        ra = lax.rsqrt(jnp.sum(jnp.where(is_a, sq, 0.0), axis=-1, keepdims=True) * inv + EPS)
        rb = lax.rsqrt(jnp.sum(jnp.where(is_a, 0.0, sq), axis=-1, keepdims=True) * inv + EPS)
        q_ref[0, :, sl] = rope(xq * jnp.where(is_a, ra, rb) * gq_ref[...]).astype(BF16)
    for c in range(nk // LANES):
        sl = slice(c * LANES, (c + 1) * LANES)
        xk = z[:, nq + c * LANES:nq + (c + 1) * LANES]
        r = lax.rsqrt(jnp.sum(xk * xk, axis=-1, keepdims=True) * (0.5 * inv) + EPS)
        k_ref[0, :, sl] = rope(xk * r * gk_ref[...]).astype(BF16)
    vt_ref[0] = z[:, nq + nk:].T.astype(BF16)


def _odd_in(h, p):
    b, lp, d = h.shape
    tm = _pick(lp, (384, 128))
    tok = lambda w: pl.BlockSpec((1, tm, w), lambda i, j: (i, j, 0))
    tab = pl.BlockSpec((2, tm, LANES), lambda i, j: (0, j, 0))
    consts = [p["mix_g"], p["w_in"], p["gq"], p["gk"]]
    nq, nk, nv = C_HEADS * C_DH, 2 * C_KV_HEADS * C_DH, C_KV_HEADS * C_DH
    return pl.pallas_call(
        functools.partial(_odd_in_body, tm=tm),
        grid=(b, lp // tm),
        in_specs=[tok(d)] + [_resident(c.shape) for c in consts] + [tab],
        out_specs=[tok(nq), tok(nk), pl.BlockSpec((1, nv, tm), lambda i, j: (i, 0, j))],
        out_shape=[jax.ShapeDtypeStruct((b, lp, nq), BF16), jax.ShapeDtypeStruct((b, lp, nk), BF16),
                   jax.ShapeDtypeStruct((b, nv, lp), BF16)],
        compiler_params=_params("parallel", "parallel"),
        name="odd_in",
    )(h, *consts, p["swa_tab"])


def _swa_body(sink_ref, q_ref, km_ref, kp_ref, kc_ref, kn_ref, vm_ref, vp_ref, vc_ref, vn_ref,
              o_ref, *, nb):
    stored = pl.program_id(1)
    n = jnp.where(stored == nb - 1, 0, stored + 1)
    nkeys = 4 * BLOCK
    group = C_HEADS // C_KV_HEADS
    kk = lax.broadcasted_iota(jnp.int32, (nkeys, BLOCK), 0)
    qq = lax.broadcasted_iota(jnp.int32, (nkeys, BLOCK), 1)
    q_pos = n * BLOCK + qq - FRONT_PAD
    k_pos = (n - 2) * BLOCK + kk - FRONT_PAD
    seq_len = nb * BLOCK - FRONT_PAD
    band = (kk >= BLOCK) & (k_pos >= N_META) & (k_pos < seq_len) & (jnp.abs(q_pos - k_pos) <= C_WINDOW)
    mask = band | ((kk < BLOCK) & (kk >= FRONT_PAD))
    lane = lax.broadcasted_iota(jnp.int32, (BLOCK, LANES), 1)
    rowi = lax.broadcasted_iota(jnp.int32, (BLOCK, LANES), 0)
    is_a = lane % HALF < HALF // 2
    keep = (jnp.where(is_a, 1.0, 0.0).astype(BF16), jnp.where(is_a, 0.0, 1.0).astype(BF16))
    row_ok = rowi + n * BLOCK >= FRONT_PAD
    ones = jnp.ones((C_DH, nkeys), BF16)
    scores = []
    for kv in range(C_KV_HEADS):
        kvs = slice(kv * LANES, (kv + 1) * LANES)
        kcat = jnp.concatenate([r[0, :, kvs] for r in (km_ref, kp_ref, kc_ref, kn_ref)], axis=0)
        pairs = [q_ref[0, :, (kv * group // 2 + i) * LANES:(kv * group // 2 + i + 1) * LANES]
                 for i in range(group // 2)]
        qs = jnp.concatenate([p2 * keep[w] for p2 in pairs for w in range(2)], axis=0)
        scores.append(_dot_nt(kcat, qs))
    for kv in range(C_KV_HEADS):
        st = scores[kv]
        vt = jnp.concatenate([r[0, kv * C_DH:(kv + 1) * C_DH, :] for r in (vm_ref, vp_ref, vc_ref, vn_ref)],
                             axis=1)
        vt = jnp.concatenate([vt, ones], axis=0)
        normed = []
        for g in range(group):
            sink = sink_ref[kv * group + g] * LOG2E
            s = jnp.where(mask, st[:, g * BLOCK:(g + 1) * BLOCK], NEG)
            m = jnp.maximum(jnp.max(s, axis=0, keepdims=True), sink)
            ot = _dot(vt, jnp.exp2(s - m).astype(BF16))
            den = ot[C_DH:C_DH + 1, :] + jnp.exp2(sink - m)
            normed.append(ot[:C_DH, :] * (1.0 / den))
        for i in range(group // 2):
            slab = jnp.concatenate([normed[2 * i], normed[2 * i + 1]], axis=0).T
            sl = slice((kv * group // 2 + i) * LANES, (kv * group // 2 + i + 1) * LANES)
            o_ref[0, :, sl] = jnp.where(row_ok, slab, 0.0).astype(BF16)


def _swa(q, k, vt, sink):
    b, lp, qw = q.shape
    nb = lp // BLOCK
    kw = k.shape[-1]
    vw = vt.shape[1]
    meta = nb - 1
    prev = lambda n: jnp.maximum(n - 1, 0)
    nxt = lambda n: jnp.where(n == meta, 0, jnp.minimum(n + 1, meta - 1))
    qspec = pl.BlockSpec((1, BLOCK, qw), lambda i, n: (i, n, 0))
    kspecs = [pl.BlockSpec((1, BLOCK, kw), lambda i, n: (i, meta, 0)),
              pl.BlockSpec((1, BLOCK, kw), lambda i, n: (i, prev(n), 0)),
              pl.BlockSpec((1, BLOCK, kw), lambda i, n: (i, n, 0)),
              pl.BlockSpec((1, BLOCK, kw), lambda i, n: (i, nxt(n), 0))]
    vspecs = [pl.BlockSpec((1, vw, BLOCK), lambda i, n: (i, 0, meta)),
              pl.BlockSpec((1, vw, BLOCK), lambda i, n: (i, 0, prev(n))),
              pl.BlockSpec((1, vw, BLOCK), lambda i, n: (i, 0, n)),
              pl.BlockSpec((1, vw, BLOCK), lambda i, n: (i, 0, nxt(n)))]
    return pl.pallas_call(
        functools.partial(_swa_body, nb=nb),
        grid=(b, nb),
        in_specs=[pl.BlockSpec(memory_space=pltpu.SMEM), qspec] + kspecs + vspecs,
        out_specs=qspec,
        out_shape=jax.ShapeDtypeStruct((b, lp, qw), BF16),
        compiler_params=_params("parallel", "parallel"),
        name="swa",
    )(sink, q, k, k, k, k, vt, vt, vt, vt)


def _rope_table(lp, theta, rot, group, offset, scale=1.0):
    half = rot // 2
    pos = _positions(lp)
    lane = np.arange(LANES) % group - offset
    in_lo = (lane >= 0) & (lane < half)
    in_hi = (lane >= half) & (lane < rot)
    idx = np.where(in_lo | in_hi, lane % half, 0)
    inv = theta ** (-jnp.asarray(idx, F32) * 2.0 / rot)
    ang = pos[:, None] * inv[None, :]
    cos = jnp.where(in_lo | in_hi, jnp.cos(ang), 1.0)
    sin = jnp.sin(ang)
    lo = jnp.where(in_lo, -sin, 0.0)
    hi = jnp.where(in_hi, sin, 0.0)
    return jnp.stack([cos, lo, hi]) * scale


def _split_rope_table(lp, theta, rot):
    lanes = np.arange(LANES)
    r = lanes % (HALF // 2)
    active = jnp.asarray(r < rot // 2)
    inv = theta ** (-jnp.asarray(np.where(r < rot // 2, r, 0), F32) * 2.0 / rot)
    ang = _positions(lp)[:, None] * inv[None, :]
    sign = jnp.where(jnp.asarray(lanes < HALF), -1.0, 1.0)
    return jnp.stack([jnp.where(active, jnp.cos(ang), 1.0), jnp.where(active, sign * jnp.sin(ang), 0.0)])


def _pair_lanes(x):
    h, n = x.shape
    return jnp.repeat(x.reshape(h // 2, 2, n).transpose(0, 2, 1), HALF, axis=2)


def _prep_ffn(gain, wg, wu, wd, fc=256):
    d, f = wg.shape
    nc = f // fc
    return (gain.reshape(1, d),
            wg.reshape(d, nc, fc).transpose(1, 0, 2).astype(BF16),
            wu.reshape(d, nc, fc).transpose(1, 0, 2).astype(BF16),
            wd.reshape(nc, fc, d).astype(BF16))


def _prep_even(lp, mix_g, w_in, dec_f, dec_b, ret_norm, q_norm, w_qb, kv_norm, w_kvb, gq, gk, w_out):
    d = w_in.shape[0]
    hw = A_HEADS * A_DK
    cut = 4 * hw + B_Q_LORA + B_KV_LORA
    lanes = np.arange(LANES)
    partner = lanes.copy()
    partner[B_NOPE:B_NOPE + B_ROPE // 2] += B_ROPE // 2
    partner[B_NOPE + B_ROPE // 2:B_QK] -= B_ROPE // 2
    is_rope = jnp.asarray((lanes >= B_NOPE) & (lanes < B_QK))
    rot = lambda w: jnp.where(is_rope, w[..., partner], 0.0)
    kr_cols = jnp.concatenate([jnp.zeros((d, B_NOPE), F32), w_in[:, cut:],
                               jnp.zeros((d, LANES - B_QK), F32)], axis=1)
    qk_perm = lambda w: w.reshape(d, A_HEADS // 2, 2, 2, A_DK // 2).transpose(0, 1, 3, 2, 4).reshape(d, hw)
    w_in2 = jnp.concatenate([w_in[:, 4 * hw:cut], kr_cols, rot(kr_cols),
                             qk_perm(w_in[:, :hw]), qk_perm(w_in[:, hw:2 * hw]), w_in[:, 2 * hw:4 * hw]],
                            axis=1).astype(BF16)
    w_qb2 = jnp.pad(w_qb.reshape(B_Q_LORA, B_HEADS, B_QK), ((0, 0), (0, 0), (0, LANES - B_QK)))
    w_kv3 = w_kvb.reshape(B_KV_LORA, B_HEADS, B_NOPE + B_DV)
    w_kb = jnp.pad(w_kv3[:, :, :B_NOPE], ((0, 0), (0, 0), (0, LANES - B_NOPE)))
    w_vb = jnp.pad(w_kv3[:, :, B_NOPE:], ((0, 0), (0, 0), (0, LANES - B_DV)))
    pad96 = lambda g: jnp.pad(g, (0, LANES - B_QK)).reshape(1, LANES)
    with_rot = lambda g: jnp.concatenate([g, rot(g)], axis=0)
    mla_tab = _rope_table(lp, B_THETA, B_ROPE, LANES, B_NOPE)
    ret_tab = _split_rope_table(lp, A_THETA, A_DK)
    qk_lanes = lambda x: _pair_lanes(x).reshape(x.shape[0] // 2, x.shape[1], 2, 2, A_DK // 2
                                                ).transpose(0, 1, 3, 2, 4).reshape(-1, x.shape[1], LANES)

    lgf = -jnp.exp(dec_f.astype(F32))
    lgb = -jnp.exp(dec_b.astype(F32))
    idx = jnp.arange(BLOCK, dtype=F32)
    diff = idx[:, None] - idx[None, :]
    dmat = (jnp.where(diff >= 0, jnp.exp(lgf[:, None, None] * jnp.maximum(diff, 0.0)), 0.0)
            + jnp.where(diff < 0, jnp.exp(lgb[:, None, None] * jnp.maximum(-diff, 0.0)), 0.0))
    ones = jnp.ones((BLOCK,), F32)
    vec = jnp.stack([
        qk_lanes(jnp.exp(lgf[:, None] * (BLOCK - 1 - idx)[None, :])),
        qk_lanes(jnp.exp(lgb[:, None] * idx[None, :])),
        qk_lanes(jnp.exp(lgf[:, None] * (idx + 1.0)[None, :])),
        qk_lanes(jnp.exp(lgb[:, None] * (BLOCK - idx)[None, :])),
        _pair_lanes(jnp.exp(BLOCK * lgf)[:, None] * ones[None, :]),
        _pair_lanes(jnp.exp(BLOCK * lgb)[:, None] * ones[None, :]),
    ], axis=1)
    half = A_HEADS * A_DK
    return dict(
        mix_g=mix_g.reshape(1, d), w_in=w_in2,
        q_norm=q_norm.reshape(1, -1), w_qb=w_qb2.reshape(B_Q_LORA, -1).astype(BF16),
        w_qr=rot(w_qb2).reshape(B_Q_LORA, -1).astype(BF16),
        kv_norm=kv_norm.reshape(1, -1), w_kb=w_kb.reshape(B_KV_LORA, -1).astype(BF16),
        w_vb=w_vb.reshape(B_KV_LORA, -1).astype(BF16),
        gq=with_rot(pad96(gq) * (B_QK ** -0.5 * LOG2E)), gk=with_rot(pad96(gk)),
        ret_tab=ret_tab,
        mla_tab=jnp.stack([mla_tab[0], mla_tab[1] + mla_tab[2]]),
        ret_dmat=dmat, ret_vec=vec,
        ret_gain=ret_norm.reshape(A_HEADS // 2, 1, LANES),
        w_out=(w_out[:half].astype(BF16), w_out[half:].astype(BF16)),
    )


def _prep_odd(lp, mix_g, w_in, gq, gk, sink, w_out):
    d = w_in.shape[0]
    nq = C_HEADS * C_DH
    nkv = C_KV_HEADS * C_DH
    qtr, hr = C_DH // 2, C_ROT // 2
    perm = np.concatenate([np.arange(hr), np.arange(C_ROT, C_ROT + qtr - hr),
                           np.arange(hr, C_ROT), np.arange(C_ROT + qtr - hr, C_DH)])
    split = lambda w, heads: w.reshape(-1, heads, C_DH)[:, :, perm].reshape(-1, heads, 2, qtr)
    q_cols = lambda w: split(w, C_HEADS).reshape(-1, C_HEADS // 2, 2, 2, qtr).transpose(0, 1, 3, 2, 4
                                                                                         ).reshape(-1, nq)
    k_cols = lambda w: jnp.repeat(split(w, C_KV_HEADS)[:, :, :, None, :], 2, axis=3).reshape(-1, 2 * nkv)
    w_in2 = jnp.concatenate([q_cols(w_in[:, :nq]), k_cols(w_in[:, nq:nq + nkv]), w_in[:, nq + nkv:]], axis=1)
    slab_gain = lambda g: jnp.repeat(g[perm].reshape(2, 1, qtr), 2, axis=1).reshape(1, LANES)
    return dict(
        mix_g=mix_g.reshape(1, d), w_in=w_in2.astype(BF16),
        gq=slab_gain(gq) * (C_DH ** -0.5 * LOG2E), gk=slab_gain(gk),
        swa_tab=_split_rope_table(lp, C_THETA, C_ROT),
        sink=sink.astype(F32), w_out=(w_out.astype(BF16),),
    )


def _trunk(x, meta, layers):
    b, seq, d = x.shape
    lp = seq + BLOCK
    tail = jnp.concatenate([jnp.zeros((FRONT_PAD, d), x.dtype), meta.astype(x.dtype)], axis=0)
    tail = _ffn(tail, layers[0]["ffn0"])
    h = _ffn_head(x, tail, layers[0]["ffn0"]).reshape(b * lp, d)
    for i, layer in enumerate(layers):
        h3 = h.reshape(b, lp, d)
        mp = layer["mix"]
        if layer["even"]:
            rq, rk, rv, rg, mq, mk, mv = _even_in(h3, mp)
            xs = [_retention(rq, rk, rv, rg, mp), _mla(mq, mk, mv)]
        else:
            q, k, v = _odd_in(h3, mp)
            xs = [_swa(q, k, v, mp["sink"])]
        if i + 1 == len(layers):
            return _proj_ffn_tail(h3, xs, mp["w_out"], layer["ffn1"])
        xs = [a.reshape(b * lp, a.shape[-1]) for a in xs]
        h = _proj_ffn(h, xs, mp["w_out"], layer["ffn1"])
        h = _ffn(h, layers[i + 1]["ffn0"])


def kernel(x_prompt, x_sample, meta_tokens, ffn_norm, ffn_w_gate, ffn_w_up, ffn_w_down, mix_norm, even_w_in, ret_decay_f, ret_decay_b, ret_out_norm, mla_q_norm, mla_w_qb, mla_kv_norm, mla_w_kvb, mla_qk_norm_q, mla_qk_norm_k, even_w_out, odd_w_in, swa_q_norm, swa_k_norm, swa_sink, odd_w_out):
    depth = ffn_norm.shape[0]
    assert x_prompt.shape[1] == x_sample.shape[1]
    lp = x_prompt.shape[1] + BLOCK
    layers = []
    for layer in range(depth):
        i = layer // 2
        ffn = [_prep_ffn(ffn_norm[layer, s], ffn_w_gate[layer, s], ffn_w_up[layer, s], ffn_w_down[layer, s])
               for s in range(2)]
        if layer % 2 == 0:
            mix = _prep_even(lp, mix_norm[layer], even_w_in[i], ret_decay_f[i], ret_decay_b[i],
                             ret_out_norm[i], mla_q_norm[i], mla_w_qb[i], mla_kv_norm[i], mla_w_kvb[i],
                             mla_qk_norm_q[i], mla_qk_norm_k[i], even_w_out[i])
        else:
            mix = _prep_odd(lp, mix_norm[layer], odd_w_in[i], swa_q_norm[i], swa_k_norm[i],
                            swa_sink[i], odd_w_out[i])
        layers.append(dict(even=layer % 2 == 0, ffn0=ffn[0], ffn1=ffn[1], mix=mix))
    return (_trunk(x_prompt, meta_tokens, layers), _trunk(x_sample, meta_tokens, layers))
```

```python
import functools

import jax
import jax.numpy as jnp
import numpy as np
from jax import lax
from jax.experimental import pallas as pl
from jax.experimental.pallas import tpu as pltpu

F32 = jnp.float32
BF16 = jnp.bfloat16

D_MODEL = 1024
N_META = 16
BLOCK = 128
FRONT_PAD = BLOCK - N_META
EPS = 1e-6
NEG = -1e30
LOG2E = 1.4426950408889634
LANES = 128
HALF = LANES // 2

A_HEADS = 8
A_DK = 64
A_THETA = 10000.0
B_HEADS = 8
B_Q_LORA = 256
B_KV_LORA = 128
B_NOPE = 64
B_ROPE = 32
B_DV = 64
B_QK = B_NOPE + B_ROPE
B_THETA = 10000.0
C_HEADS = 16
C_KV_HEADS = 4
C_DH = 64
C_ROT = 16
C_THETA = 500000.0
C_WINDOW = 128

VMEM_LIMIT = 56 * 1024 * 1024


def _params(*sem):
    return pltpu.CompilerParams(dimension_semantics=sem, vmem_limit_bytes=VMEM_LIMIT)


def _resident(shape):
    zeros = (0,) * len(shape)
    return pl.BlockSpec(shape, lambda *_: zeros, pipeline_mode=pl.Buffered(1))


def _rms_rows(x, gain):
    return x * lax.rsqrt(jnp.mean(x * x, axis=-1, keepdims=True) + EPS) * gain


def _dot(a, b):
    return jnp.dot(a, b, preferred_element_type=F32)


def _dot_nt(a, b):
    return lax.dot_general(a, b, (((1,), (1,)), ((), ())), preferred_element_type=F32)


def _is_pad_row(row, seq):
    return (row >= seq) & (row < seq + FRONT_PAD)


def _positions(lp):
    row = jnp.arange(lp, dtype=F32)
    seq = lp - BLOCK
    return jnp.where(row < seq, row + N_META, row - (seq + FRONT_PAD))


def _pick(n, options):
    for o in options:
        if n % o == 0:
            return o
    raise ValueError(f"no tile in {options} divides {n}")


def _ffn_core(x, g_ref, wg_ref, wu_ref, wd_ref):
    xn = _rms_rows(x, g_ref[...]).astype(BF16)
    acc = jnp.zeros_like(x)
    fc = wd_ref.shape[1]
    for c in range(wd_ref.shape[0]):
        gate = _dot(xn, wg_ref[:, c * fc:(c + 1) * fc])
        up = _dot(xn, wu_ref[:, c * fc:(c + 1) * fc])
        act = (gate * (1.0 / (1.0 + jnp.exp(-gate))) * up).astype(BF16)
        acc = acc + _dot(act, wd_ref[c])
    return x + 0.5 * acc


def _ffn_body(h_ref, g_ref, wg_ref, wu_ref, wd_ref, o_ref):
    o_ref[...] = _ffn_core(h_ref[...], g_ref, wg_ref, wu_ref, wd_ref)


def _ffn_head_body(x_ref, tail_ref, g_ref, wg_ref, wu_ref, wd_ref, o_ref, *, nt):
    t = pl.program_id(1)

    @pl.when(t < nt)
    def _():
        o_ref[0] = _ffn_core(x_ref[0], g_ref, wg_ref, wu_ref, wd_ref)

    @pl.when(t == nt)
    def _():
        o_ref[0, :BLOCK, :] = tail_ref[...]


def _proj_ffn_body(*refs, n_in):
    tile = lambda r: r[...] if len(r.shape) == 2 else r[0]
    h_ref = refs[0]
    xs = refs[1:1 + n_in]
    ws = refs[1 + n_in:1 + 2 * n_in]
    g_ref, wg_ref, wu_ref, wd_ref, o_ref = refs[1 + 2 * n_in:]
    h2 = tile(h_ref)
    for x_ref, w_ref in zip(xs, ws):
        h2 = h2 + _dot(tile(x_ref), w_ref[...])
    out = _ffn_core(h2, g_ref, wg_ref, wu_ref, wd_ref)
    if len(o_ref.shape) == 2:
        o_ref[...] = out
    else:
        o_ref[0] = out


def _ffn_specs(ffn):
    return [_resident(a.shape) for a in ffn]


def _ffn(h2d, ffn):
    t, d = h2d.shape
    tm = _pick(t, (512, 384, 256, 128))
    row = pl.BlockSpec((tm, d), lambda i: (i, 0))
    return pl.pallas_call(
        _ffn_body,
        grid=(t // tm,),
        in_specs=[row] + _ffn_specs(ffn),
        out_specs=row,
        out_shape=jax.ShapeDtypeStruct((t, d), F32),
        compiler_params=_params("parallel"),
        name="ffn",
    )(h2d, *ffn)


def _ffn_head(x, tail, ffn):
    b, seq, d = x.shape
    tm = _pick(seq, (512, 256, 128))
    nt = seq // tm
    return pl.pallas_call(
        functools.partial(_ffn_head_body, nt=nt),
        grid=(b, nt + 1),
        in_specs=[pl.BlockSpec((1, tm, d), lambda i, t: (i, jnp.minimum(t, nt - 1), 0)),
                  pl.BlockSpec((BLOCK, d), lambda i, t: (0, 0))] + _ffn_specs(ffn),
        out_specs=pl.BlockSpec((1, tm, d), lambda i, t: (i, t, 0)),
        out_shape=jax.ShapeDtypeStruct((b, seq + BLOCK, d), F32),
        compiler_params=_params("parallel", "arbitrary"),
        name="ffn_head",
    )(x, tail, *ffn)


def _proj_ffn_tail(h, xs, ws, ffn):
    b, lp, d = h.shape
    seq = lp - BLOCK
    tm = _pick(seq, (512, 256, 128))
    tok = lambda w: pl.BlockSpec((1, tm, w), lambda i, t: (i, t, 0))
    return pl.pallas_call(
        functools.partial(_proj_ffn_body, n_in=len(xs)),
        grid=(b, seq // tm),
        in_specs=[tok(d)] + [tok(x.shape[-1]) for x in xs] + [_resident(w.shape) for w in ws]
        + _ffn_specs(ffn),
        out_specs=tok(d),
        out_shape=jax.ShapeDtypeStruct((b, seq, d), F32),
        compiler_params=_params("parallel", "parallel"),
        name="proj_ffn_tail",
    )(h, *xs, *ws, *ffn)


def _proj_ffn(h2d, xs, ws, ffn):
    t, d = h2d.shape
    tm = _pick(t, (512, 384, 256, 128))
    row = pl.BlockSpec((tm, d), lambda i: (i, 0))
    x_specs = [pl.BlockSpec((tm, x.shape[1]), lambda i: (i, 0)) for x in xs]
    w_specs = [_resident(w.shape) for w in ws]
    return pl.pallas_call(
        functools.partial(_proj_ffn_body, n_in=len(xs)),
        grid=(t // tm,),
        in_specs=[row] + x_specs + w_specs + _ffn_specs(ffn),
        out_specs=row,
        out_shape=jax.ShapeDtypeStruct((t, d), F32),
        compiler_params=_params("parallel"),
        name="proj_ffn",
    )(h2d, *xs, *ws, *ffn)


def _even_in_body(h_ref, g_ref, win_ref, qn_ref, wqb_ref, wqr_ref, kvn_ref, wkb_ref, wvb_ref,
                  gq_ref, gk_ref, rtab_ref, mtab_ref,
                  rq_ref, rk_ref, rv_ref, rg_ref, mq_ref, mk_ref, mv_ref, *, tm, seq):
    x = h_ref[0]
    hn = _rms_rows(x, g_ref[...]).astype(BF16)
    z = _dot(hn, win_ref[...])
    o = B_Q_LORA + B_KV_LORA
    cq = _rms_rows(z[:, :B_Q_LORA], qn_ref[...]).astype(BF16)
    ckv = _rms_rows(z[:, B_Q_LORA:o], kvn_ref[...]).astype(BF16)
    kr = z[:, o:o + LANES]
    kr_rot = z[:, o + LANES:o + 2 * LANES]
    q_all = _dot(cq, wqb_ref[...])
    q_rot = _dot(cq, wqr_ref[...])
    k_all = _dot(ckv, wkb_ref[...])
    v_all = _dot(ckv, wvb_ref[...])

    mcos, msin = mtab_ref[0], mtab_ref[1]
    lane = lax.broadcasted_iota(jnp.int32, (tm, LANES), 1)
    v_ones = jnp.where(lane >= HALF, 1.0, 0.0)
    row = lax.broadcasted_iota(jnp.int32, (tm, LANES), 0) + pl.program_id(1) * tm
    bias_lane = lane == B_QK
    q_bias = jnp.where(bias_lane, 1.0, 0.0)
    k_bias = jnp.where(bias_lane & _is_pad_row(row, seq), NEG, 0.0)
    inv = 1.0 / B_QK
    qcos, qsin = mcos * gq_ref[0:1], msin * gq_ref[1:2]
    kcos, ksin = mcos * gk_ref[0:1], msin * gk_ref[1:2]
    kr_sin = kr_rot * ksin
    heads = [slice(hd * LANES, (hd + 1) * LANES) for hd in range(B_HEADS)]
    qhs = [q_all[:, sl] for sl in heads]
    khs = [k_all[:, sl] + kr for sl in heads]
    rqs = [lax.rsqrt(jnp.sum(qh * qh, axis=-1, keepdims=True) * inv + EPS) for qh in qhs]
    rks = [lax.rsqrt(jnp.sum(kh * kh, axis=-1, keepdims=True) * inv + EPS) for kh in khs]
    for sl, qh, kh, rq, rk in zip(heads, qhs, khs, rqs, rks):
        mq_ref[0, :, sl] = ((qh * qcos + q_rot[:, sl] * qsin) * rq + q_bias).astype(BF16)
        mk_ref[0, :, sl] = ((kh * kcos + kr_sin) * rk + k_bias).astype(BF16)
        mv_ref[0, :, sl] = (v_all[:, sl] + v_ones).astype(BF16)

    hw = A_HEADS * A_DK
    o += 2 * LANES
    rcos, rsin = rtab_ref[0], rtab_ref[1]
    for c in range(hw // LANES):
        sl = slice(c * LANES, (c + 1) * LANES)
        q = z[:, o + c * LANES:o + (c + 1) * LANES]
        k = z[:, o + hw + c * LANES:o + hw + (c + 1) * LANES]
        rq_ref[0, :, sl] = (q * rcos + pltpu.roll(q, HALF, 1) * rsin).astype(BF16)
        rk_ref[0, :, sl] = ((k * rcos + pltpu.roll(k, HALF, 1) * rsin) * (A_DK ** -0.5)).astype(BF16)
    rv_ref[0] = z[:, o + 2 * hw:o + 3 * hw].astype(BF16)
    ga = z[:, o + 3 * hw:o + 4 * hw]
    rg_ref[0] = (ga * (1.0 / (1.0 + jnp.exp(-ga)))).astype(BF16)


def _even_in(h, p):
    b, lp, d = h.shape
    tm = _pick(lp, (384, 128))
    tok = lambda w: pl.BlockSpec((1, tm, w), lambda i, j: (i, j, 0))
    tab = pl.BlockSpec((2, tm, LANES), lambda i, j: (0, j, 0))
    consts = [p["mix_g"], p["w_in"], p["q_norm"], p["w_qb"], p["w_qr"], p["kv_norm"], p["w_kb"], p["w_vb"],
              p["gq"], p["gk"]]
    hw = A_HEADS * A_DK
    widths = [hw, hw, hw, hw, B_HEADS * LANES, B_HEADS * LANES, B_HEADS * LANES]
    return pl.pallas_call(
        functools.partial(_even_in_body, tm=tm, seq=lp - BLOCK),
        grid=(b, lp // tm),
        in_specs=[tok(d)] + [_resident(c.shape) for c in consts] + [tab, tab],
        out_specs=[tok(w) for w in widths],
        out_shape=[jax.ShapeDtypeStruct((b, lp, w), BF16) for w in widths],
        compiler_params=_params("parallel", "parallel"),
        name="even_in",
    )(h, *consts, p["ret_tab"], p["mla_tab"])


def _ret_body(q_ref, k_ref, v_ref, g_ref, dmat_ref, vec_ref, gain_ref, o_ref, acc_ref, kv_ref, st_ref,
              *, nb):
    lane = lax.broadcasted_iota(jnp.int32, (BLOCK, LANES), 1)
    first = lane < HALF
    wkf, wkb, qf, qb = vec_ref[0, 0], vec_ref[0, 1], vec_ref[0, 2], vec_ref[0, 3]
    cf, cb = vec_ref[0, 4][:1], vec_ref[0, 5][:1]
    dcat = jnp.concatenate([dmat_ref[0], dmat_ref[1]], axis=0)
    qk_first = lane % HALF < HALF // 2
    keep0 = jnp.where(qk_first, 1.0, 0.0).astype(BF16)
    keep1 = jnp.where(qk_first, 0.0, 1.0).astype(BF16)
    row2 = lax.broadcasted_iota(jnp.int32, (2 * LANES, LANES), 0) % HALF
    col2 = lax.broadcasted_iota(jnp.int32, (2 * LANES, LANES), 1)
    same2 = (row2 < HALF // 2) == (col2 < HALF)
    unroll = next(u for u in (11, 3, 1) if nb % u == 0)

    def intra(grp, carry):
        ns = [grp * unroll + u for u in range(unroll)]
        rows = [pl.ds(pl.multiple_of(n * BLOCK, BLOCK), BLOCK) for n in ns]
        qs = [q_ref[0, r, :] for r in rows]
        ks = [k_ref[0, r, :] for r in rows]
        vs = [v_ref[0, r, :] for r in rows]
        ss = [_dot_nt(jnp.concatenate([q * keep0, q * keep1], axis=0), k) for q, k in zip(qs, ks)]
        kws = [jnp.concatenate([k.astype(F32) * wkf, k.astype(F32) * wkb], axis=1).T.astype(BF16)
               for k in ks]
        kvs = [_dot(kw, v) for kw, v in zip(kws, vs)]
        ss = [(s * dcat).astype(BF16) for s in ss]
        os = [_dot(s, v) for s, v in zip(ss, vs)]
        for n, r, kv, o2 in zip(ns, rows, kvs, os):
            kv_ref[n] = jnp.where(same2, kv, 0.0)
            acc_ref[r, :] = jnp.where(first, o2[:BLOCK], o2[BLOCK:])
        return carry

    lax.fori_loop(0, nb // unroll, intra, 0)

    def scan(t, carry):
        sf, sb = carry
        i = lax.rem(t + nb - 1, nb)
        j = lax.rem(2 * nb - 2 - t, nb)
        st_ref[i, :LANES, :] = sf.astype(BF16)
        st_ref[j, LANES:, :] = sb.astype(BF16)
        return sf * cf + kv_ref[i, :LANES, :], sb * cb + kv_ref[j, LANES:, :]

    zero = jnp.zeros((LANES, LANES), F32)
    lax.fori_loop(0, nb, scan, (zero, zero))

    def inter(n, carry):
        rows = pl.ds(pl.multiple_of(n * BLOCK, BLOCK), BLOCK)
        qf32 = q_ref[0, rows, :].astype(F32)
        qq = jnp.concatenate([qf32 * qf, qf32 * qb], axis=1).astype(BF16)
        o = acc_ref[rows, :] + _dot(qq, st_ref[n])
        sq = o * o
        inv = 1.0 / HALF
        r0 = lax.rsqrt(jnp.sum(jnp.where(first, sq, 0.0), axis=-1, keepdims=True) * inv + EPS)
        r1 = lax.rsqrt(jnp.sum(jnp.where(first, 0.0, sq), axis=-1, keepdims=True) * inv + EPS)
        out = o * jnp.where(first, r0, r1) * gain_ref[0] * g_ref[0, rows, :].astype(F32)
        o_ref[0, rows, :] = out.astype(BF16)
        return carry

    lax.fori_loop(0, nb, inter, 0, unroll=unroll)


def _retention(rq, rk, rv, rg, p):
    b, lp, hw = rq.shape
    pairs = hw // LANES
    nb = lp // BLOCK
    seq = pl.BlockSpec((1, lp, LANES), lambda i, j: (i, 0, j))
    return pl.pallas_call(
        functools.partial(_ret_body, nb=nb),
        grid=(b, pairs),
        in_specs=[seq, seq, seq, seq,
                  pl.BlockSpec((2, BLOCK, BLOCK), lambda i, j: (j, 0, 0)),
                  pl.BlockSpec((1, 6, BLOCK, LANES), lambda i, j: (j, 0, 0, 0)),
                  pl.BlockSpec((1, 1, LANES), lambda i, j: (j, 0, 0))],
        out_specs=seq,
        out_shape=jax.ShapeDtypeStruct((b, lp, hw), BF16),
        scratch_shapes=[pltpu.VMEM((lp, LANES), F32), pltpu.VMEM((nb, 2 * LANES, LANES), F32),
                        pltpu.VMEM((nb, 2 * LANES, LANES), BF16)],
        compiler_params=_params("parallel", "parallel"),
        name="retention",
    )(rq, rk, rv, rg, p["ret_dmat"], p["ret_vec"], p["ret_gain"])


MXU_TILE = 256
MLA_ROWS = 16


def _mla_body(q_ref, k_ref, v_ref, o_ref, s_scr, p_scr, o_scr, *, tq):
    lp = k_ref.shape[1]
    chunks = [(c, min(MXU_TILE, lp - c)) for c in range(0, lp, MXU_TILE)]
    hk = lp // 2 // MXU_TILE * MXU_TILE

    def scores(hd):
        q = q_ref[0, :, hd * LANES:(hd + 1) * LANES]
        for c, w in chunks:
            s_scr[hd, :, c:c + w] = _dot_nt(q, k_ref[0, c:c + w, hd * LANES:(hd + 1) * LANES])

    def probs(hd):
        for r in range(0, tq, MLA_ROWS):
            sb = s_scr[hd, r:r + MLA_ROWS, :]
            m = jnp.max(sb, axis=-1, keepdims=True)
            p_scr[hd, r:r + MLA_ROWS, :] = jnp.exp2(sb - m).astype(BF16)

    def weighted(hd):
        v = v_ref[0, :, hd * LANES:(hd + 1) * LANES]
        if hk == 0:
            return _dot(p_scr[hd], v)
        return _dot(p_scr[hd, :, :hk], v[:hk]) + _dot(p_scr[hd, :, hk:], v[hk:])

    scores(0)
    scores(1)
    probs(0)
    o_scr[...] = weighted(0)
    probs(1)
    oa = o_scr[...]
    ob = weighted(1)
    lane = lax.broadcasted_iota(jnp.int32, (tq, LANES), 1)
    row = lax.broadcasted_iota(jnp.int32, (tq, LANES), 0) + pl.program_id(2) * tq
    out = jnp.where(lane < HALF, oa * pltpu.roll(1.0 / oa, HALF, 1),
                    pltpu.roll(ob, HALF, 1) * (1.0 / ob))
    o_ref[0] = jnp.where(_is_pad_row(row, lp - BLOCK), 0.0, out).astype(BF16)


def _mla(mq, mk, mv):
    b, lp, _ = mq.shape
    pairs = B_HEADS // 2
    tq = _pick(lp, (384, 128))
    return pl.pallas_call(
        functools.partial(_mla_body, tq=tq),
        grid=(b, pairs, lp // tq),
        in_specs=[pl.BlockSpec((1, tq, 2 * LANES), lambda i, j, t: (i, t, j)),
                  pl.BlockSpec((1, lp, 2 * LANES), lambda i, j, t: (i, 0, j)),
                  pl.BlockSpec((1, lp, 2 * LANES), lambda i, j, t: (i, 0, j))],
        out_specs=pl.BlockSpec((1, tq, LANES), lambda i, j, t: (i, t, j)),
        out_shape=jax.ShapeDtypeStruct((b, lp, B_HEADS * B_DV), BF16),
        scratch_shapes=[pltpu.VMEM((2, tq, lp), F32), pltpu.VMEM((2, tq, lp), BF16),
                        pltpu.VMEM((tq, LANES), F32)],
        compiler_params=_params("parallel", "parallel", "parallel"),
        name="mla",
    )(mq, mk, mv)


def _odd_in_body(h_ref, g_ref, win_ref, gq_ref, gk_ref, tab_ref, q_ref, k_ref, vt_ref, *, tm):
    x = h_ref[0]
    hn = _rms_rows(x, g_ref[...]).astype(BF16)
    z = _dot(hn, win_ref[...])
    cos, sin = tab_ref[0], tab_ref[1]
    is_a = lax.broadcasted_iota(jnp.int32, (tm, LANES), 1) % HALF < HALF // 2
    nq = C_HEADS * C_DH
    nk = 2 * C_KV_HEADS * C_DH
    inv = 1.0 / C_DH
    rope = lambda y: y * cos + pltpu.roll(y, HALF, 1) * sin
    xqs = [z[:, c * LANES:(c + 1) * LANES] for c in range(nq // LANES)]
    xks = [z[:, nq + c * LANES:nq + (c + 1) * LANES] for c in range(nk // LANES)]
    rqs = []
    for xq in xqs:
        sq = xq * xq
        ra = lax.rsqrt(jnp.sum(jnp.where(is_a, sq, 0.0), axis=-1, keepdims=True) * inv + EPS)
        rb = lax.rsqrt(jnp.sum(jnp.where(is_a, 0.0, sq), axis=-1, keepdims=True) * inv + EPS)
        rqs.append(jnp.where(is_a, ra, rb))
    rks = [lax.rsqrt(jnp.sum(xk * xk, axis=-1, keepdims=True) * (0.5 * inv) + EPS) for xk in xks]
    for c, (xq, r) in enumerate(zip(xqs, rqs)):
        q_ref[0, :, c * LANES:(c + 1) * LANES] = rope(xq * r * gq_ref[...]).astype(BF16)
    for c, (xk, r) in enumerate(zip(xks, rks)):
        k_ref[0, :, c * LANES:(c + 1) * LANES] = rope(xk * r * gk_ref[...]).astype(BF16)
    vt_ref[0] = z[:, nq + nk:].T.astype(BF16)


def _odd_in(h, p):
    b, lp, d = h.shape
    tm = _pick(lp, (384, 128))
    tok = lambda w: pl.BlockSpec((1, tm, w), lambda i, j: (i, j, 0))
    tab = pl.BlockSpec((2, tm, LANES), lambda i, j: (0, j, 0))
    consts = [p["mix_g"], p["w_in"], p["gq"], p["gk"]]
    nq, nk, nv = C_HEADS * C_DH, 2 * C_KV_HEADS * C_DH, C_KV_HEADS * C_DH
    return pl.pallas_call(
        functools.partial(_odd_in_body, tm=tm),
        grid=(b, lp // tm),
        in_specs=[tok(d)] + [_resident(c.shape) for c in consts] + [tab],
        out_specs=[tok(nq), tok(nk), pl.BlockSpec((1, nv, tm), lambda i, j: (i, 0, j))],
        out_shape=[jax.ShapeDtypeStruct((b, lp, nq), BF16), jax.ShapeDtypeStruct((b, lp, nk), BF16),
                   jax.ShapeDtypeStruct((b, nv, lp), BF16)],
        compiler_params=_params("parallel", "parallel"),
        name="odd_in",
    )(h, *consts, p["swa_tab"])


def _swa_body(sink_ref, q_ref, km_ref, kp_ref, kc_ref, kn_ref, vm_ref, vp_ref, vc_ref, vn_ref,
              o_ref, *, nb):
    stored = pl.program_id(1)
    n = jnp.where(stored == nb - 1, 0, stored + 1)
    nkeys = 4 * BLOCK
    group = C_HEADS // C_KV_HEADS
    kk = lax.broadcasted_iota(jnp.int32, (nkeys, BLOCK), 0)
    qq = lax.broadcasted_iota(jnp.int32, (nkeys, BLOCK), 1)
    q_pos = n * BLOCK + qq - FRONT_PAD
    k_pos = (n - 2) * BLOCK + kk - FRONT_PAD
    seq_len = nb * BLOCK - FRONT_PAD
    band = (kk >= BLOCK) & (k_pos >= N_META) & (k_pos < seq_len) & (jnp.abs(q_pos - k_pos) <= C_WINDOW)
    mask = band | ((kk < BLOCK) & (kk >= FRONT_PAD))
    lane = lax.broadcasted_iota(jnp.int32, (BLOCK, LANES), 1)
    rowi = lax.broadcasted_iota(jnp.int32, (BLOCK, LANES), 0)
    is_a = lane % HALF < HALF // 2
    keep = (jnp.where(is_a, 1.0, 0.0).astype(BF16), jnp.where(is_a, 0.0, 1.0).astype(BF16))
    row_ok = rowi + n * BLOCK >= FRONT_PAD
    ones = jnp.ones((C_DH, nkeys), BF16)
    scores = []
    for kv in range(C_KV_HEADS):
        kvs = slice(kv * LANES, (kv + 1) * LANES)
        kcat = jnp.concatenate([r[0, :, kvs] for r in (km_ref, kp_ref, kc_ref, kn_ref)], axis=0)
        pairs = [q_ref[0, :, (kv * group // 2 + i) * LANES:(kv * group // 2 + i + 1) * LANES]
                 for i in range(group // 2)]
        qs = jnp.concatenate([p2 * keep[w] for p2 in pairs for w in range(2)], axis=0)
        scores.append(_dot_nt(kcat, qs))
    probs, sink_terms = [], []
    for kv in range(C_KV_HEADS):
        for g in range(group):
            sink = sink_ref[kv * group + g] * LOG2E
            s = jnp.where(mask, scores[kv][:, g * BLOCK:(g + 1) * BLOCK], NEG)
            m = jnp.maximum(jnp.max(s, axis=0, keepdims=True), sink)
            probs.append(jnp.exp2(s - m).astype(BF16))
            sink_terms.append(jnp.exp2(sink - m))
    for kv in range(C_KV_HEADS):
        vt = jnp.concatenate([r[0, kv * C_DH:(kv + 1) * C_DH, :] for r in (vm_ref, vp_ref, vc_ref, vn_ref)],
                             axis=1)
        vt = jnp.concatenate([vt, ones], axis=0)
        normed = []
        for g in range(group):
            ot = _dot(vt, probs[kv * group + g])
            den = ot[C_DH:C_DH + 1, :] + sink_terms[kv * group + g]
            normed.append(ot[:C_DH, :] * (1.0 / den))
        for i in range(group // 2):
            slab = jnp.concatenate([normed[2 * i], normed[2 * i + 1]], axis=0).T
            sl = slice((kv * group // 2 + i) * LANES, (kv * group // 2 + i + 1) * LANES)
            o_ref[0, :, sl] = jnp.where(row_ok, slab, 0.0).astype(BF16)


def _swa(q, k, vt, sink):
    b, lp, qw = q.shape
    nb = lp // BLOCK
    kw = k.shape[-1]
    vw = vt.shape[1]
    meta = nb - 1
    prev = lambda n: jnp.maximum(n - 1, 0)
    nxt = lambda n: jnp.where(n == meta, 0, jnp.minimum(n + 1, meta - 1))
    qspec = pl.BlockSpec((1, BLOCK, qw), lambda i, n: (i, n, 0))
    kspecs = [pl.BlockSpec((1, BLOCK, kw), lambda i, n: (i, meta, 0)),
              pl.BlockSpec((1, BLOCK, kw), lambda i, n: (i, prev(n), 0)),
              pl.BlockSpec((1, BLOCK, kw), lambda i, n: (i, n, 0)),
              pl.BlockSpec((1, BLOCK, kw), lambda i, n: (i, nxt(n), 0))]
    vspecs = [pl.BlockSpec((1, vw, BLOCK), lambda i, n: (i, 0, meta)),
              pl.BlockSpec((1, vw, BLOCK), lambda i, n: (i, 0, prev(n))),
              pl.BlockSpec((1, vw, BLOCK), lambda i, n: (i, 0, n)),
              pl.BlockSpec((1, vw, BLOCK), lambda i, n: (i, 0, nxt(n)))]
    return pl.pallas_call(
        functools.partial(_swa_body, nb=nb),
        grid=(b, nb),
        in_specs=[pl.BlockSpec(memory_space=pltpu.SMEM), qspec] + kspecs + vspecs,
        out_specs=qspec,
        out_shape=jax.ShapeDtypeStruct((b, lp, qw), BF16),
        compiler_params=_params("parallel", "parallel"),
        name="swa",
    )(sink, q, k, k, k, k, vt, vt, vt, vt)


def _rope_table(lp, theta, rot, group, offset, scale=1.0):
    half = rot // 2
    pos = _positions(lp)
    lane = np.arange(LANES) % group - offset
    in_lo = (lane >= 0) & (lane < half)
    in_hi = (lane >= half) & (lane < rot)
    idx = np.where(in_lo | in_hi, lane % half, 0)
    inv = theta ** (-jnp.asarray(idx, F32) * 2.0 / rot)
    ang = pos[:, None] * inv[None, :]
    cos = jnp.where(in_lo | in_hi, jnp.cos(ang), 1.0)
    sin = jnp.sin(ang)
    lo = jnp.where(in_lo, -sin, 0.0)
    hi = jnp.where(in_hi, sin, 0.0)
    return jnp.stack([cos, lo, hi]) * scale


def _split_rope_table(lp, theta, rot):
    lanes = np.arange(LANES)
    r = lanes % (HALF // 2)
    active = jnp.asarray(r < rot // 2)
    inv = theta ** (-jnp.asarray(np.where(r < rot // 2, r, 0), F32) * 2.0 / rot)
    ang = _positions(lp)[:, None] * inv[None, :]
    sign = jnp.where(jnp.asarray(lanes < HALF), -1.0, 1.0)
    return jnp.stack([jnp.where(active, jnp.cos(ang), 1.0), jnp.where(active, sign * jnp.sin(ang), 0.0)])


def _pair_lanes(x):
    h, n = x.shape
    return jnp.repeat(x.reshape(h // 2, 2, n).transpose(0, 2, 1), HALF, axis=2)


def _prep_ffn(gain, wg, wu, wd):
    d, f = wg.shape
    return (gain.reshape(1, d), wg.astype(BF16), wu.astype(BF16),
            wd.reshape(f // MXU_TILE, MXU_TILE, d).astype(BF16))


def _prep_even(lp, mix_g, w_in, dec_f, dec_b, ret_norm, q_norm, w_qb, kv_norm, w_kvb, gq, gk, w_out):
    d = w_in.shape[0]
    hw = A_HEADS * A_DK
    cut = 4 * hw + B_Q_LORA + B_KV_LORA
    lanes = np.arange(LANES)
    partner = lanes.copy()
    partner[B_NOPE:B_NOPE + B_ROPE // 2] += B_ROPE // 2
    partner[B_NOPE + B_ROPE // 2:B_QK] -= B_ROPE // 2
    is_rope = jnp.asarray((lanes >= B_NOPE) & (lanes < B_QK))
    rot = lambda w: jnp.where(is_rope, w[..., partner], 0.0)
    kr_cols = jnp.concatenate([jnp.zeros((d, B_NOPE), F32), w_in[:, cut:],
                               jnp.zeros((d, LANES - B_QK), F32)], axis=1)
    qk_perm = lambda w: w.reshape(d, A_HEADS // 2, 2, 2, A_DK // 2).transpose(0, 1, 3, 2, 4).reshape(d, hw)
    w_in2 = jnp.concatenate([w_in[:, 4 * hw:cut], kr_cols, rot(kr_cols),
                             qk_perm(w_in[:, :hw]), qk_perm(w_in[:, hw:2 * hw]), w_in[:, 2 * hw:4 * hw]],
                            axis=1).astype(BF16)
    w_qb2 = jnp.pad(w_qb.reshape(B_Q_LORA, B_HEADS, B_QK), ((0, 0), (0, 0), (0, LANES - B_QK)))
    w_kv3 = w_kvb.reshape(B_KV_LORA, B_HEADS, B_NOPE + B_DV)
    w_kb = jnp.pad(w_kv3[:, :, :B_NOPE], ((0, 0), (0, 0), (0, LANES - B_NOPE)))
    w_vb = jnp.pad(w_kv3[:, :, B_NOPE:], ((0, 0), (0, 0), (0, LANES - B_DV)))
    pad96 = lambda g: jnp.pad(g, (0, LANES - B_QK)).reshape(1, LANES)
    with_rot = lambda g: jnp.concatenate([g, rot(g)], axis=0)
    mla_tab = _rope_table(lp, B_THETA, B_ROPE, LANES, B_NOPE)
    ret_tab = _split_rope_table(lp, A_THETA, A_DK)
    qk_lanes = lambda x: _pair_lanes(x).reshape(x.shape[0] // 2, x.shape[1], 2, 2, A_DK // 2
                                                ).transpose(0, 1, 3, 2, 4).reshape(-1, x.shape[1], LANES)

    lgf = -jnp.exp(dec_f.astype(F32))
    lgb = -jnp.exp(dec_b.astype(F32))
    idx = jnp.arange(BLOCK, dtype=F32)
    diff = idx[:, None] - idx[None, :]
    dmat = (jnp.where(diff >= 0, jnp.exp(lgf[:, None, None] * jnp.maximum(diff, 0.0)), 0.0)
            + jnp.where(diff < 0, jnp.exp(lgb[:, None, None] * jnp.maximum(-diff, 0.0)), 0.0))
    ones = jnp.ones((BLOCK,), F32)
    vec = jnp.stack([
        qk_lanes(jnp.exp(lgf[:, None] * (BLOCK - 1 - idx)[None, :])),
        qk_lanes(jnp.exp(lgb[:, None] * idx[None, :])),
        qk_lanes(jnp.exp(lgf[:, None] * (idx + 1.0)[None, :])),
        qk_lanes(jnp.exp(lgb[:, None] * (BLOCK - idx)[None, :])),
        _pair_lanes(jnp.exp(BLOCK * lgf)[:, None] * ones[None, :]),
        _pair_lanes(jnp.exp(BLOCK * lgb)[:, None] * ones[None, :]),
    ], axis=1)
    half = A_HEADS * A_DK
    return dict(
        mix_g=mix_g.reshape(1, d), w_in=w_in2,
        q_norm=q_norm.reshape(1, -1), w_qb=w_qb2.reshape(B_Q_LORA, -1).astype(BF16),
        w_qr=rot(w_qb2).reshape(B_Q_LORA, -1).astype(BF16),
        kv_norm=kv_norm.reshape(1, -1), w_kb=w_kb.reshape(B_KV_LORA, -1).astype(BF16),
        w_vb=w_vb.reshape(B_KV_LORA, -1).astype(BF16),
        gq=with_rot(pad96(gq) * (B_QK ** -0.5 * LOG2E)), gk=with_rot(pad96(gk)),
        ret_tab=ret_tab,
        mla_tab=jnp.stack([mla_tab[0], mla_tab[1] + mla_tab[2]]),
        ret_dmat=dmat, ret_vec=vec,
        ret_gain=ret_norm.reshape(A_HEADS // 2, 1, LANES),
        w_out=(w_out[:half].astype(BF16), w_out[half:].astype(BF16)),
    )


def _prep_odd(lp, mix_g, w_in, gq, gk, sink, w_out):
    d = w_in.shape[0]
    nq = C_HEADS * C_DH
    nkv = C_KV_HEADS * C_DH
    qtr, hr = C_DH // 2, C_ROT // 2
    perm = np.concatenate([np.arange(hr), np.arange(C_ROT, C_ROT + qtr - hr),
                           np.arange(hr, C_ROT), np.arange(C_ROT + qtr - hr, C_DH)])
    split = lambda w, heads: w.reshape(-1, heads, C_DH)[:, :, perm].reshape(-1, heads, 2, qtr)
    q_cols = lambda w: split(w, C_HEADS).reshape(-1, C_HEADS // 2, 2, 2, qtr).transpose(0, 1, 3, 2, 4
                                                                                         ).reshape(-1, nq)
    k_cols = lambda w: jnp.repeat(split(w, C_KV_HEADS)[:, :, :, None, :], 2, axis=3).reshape(-1, 2 * nkv)
    w_in2 = jnp.concatenate([q_cols(w_in[:, :nq]), k_cols(w_in[:, nq:nq + nkv]), w_in[:, nq + nkv:]], axis=1)
    slab_gain = lambda g: jnp.repeat(g[perm].reshape(2, 1, qtr), 2, axis=1).reshape(1, LANES)
    return dict(
        mix_g=mix_g.reshape(1, d), w_in=w_in2.astype(BF16),
        gq=slab_gain(gq) * (C_DH ** -0.5 * LOG2E), gk=slab_gain(gk),
        swa_tab=_split_rope_table(lp, C_THETA, C_ROT),
        sink=sink.astype(F32), w_out=(w_out.astype(BF16),),
    )


def _trunk(x, meta, layers):
    b, seq, d = x.shape
    lp = seq + BLOCK
    tail = jnp.concatenate([jnp.zeros((FRONT_PAD, d), x.dtype), meta.astype(x.dtype)], axis=0)
    tail = _ffn(tail, layers[0]["ffn0"])
    h = _ffn_head(x, tail, layers[0]["ffn0"]).reshape(b * lp, d)
    for i, layer in enumerate(layers):
        h3 = h.reshape(b, lp, d)
        mp = layer["mix"]
        if layer["even"]:
            rq, rk, rv, rg, mq, mk, mv = _even_in(h3, mp)
            xs = [_retention(rq, rk, rv, rg, mp), _mla(mq, mk, mv)]
        else:
            q, k, v = _odd_in(h3, mp)
            xs = [_swa(q, k, v, mp["sink"])]
        if i + 1 == len(layers):
            return _proj_ffn_tail(h3, xs, mp["w_out"], layer["ffn1"])
        xs = [a.reshape(b * lp, a.shape[-1]) for a in xs]
        h = _proj_ffn(h, xs, mp["w_out"], layer["ffn1"])
        h = _ffn(h, layers[i + 1]["ffn0"])


def kernel(x_prompt, x_sample, meta_tokens, ffn_norm, ffn_w_gate, ffn_w_up, ffn_w_down, mix_norm, even_w_in, ret_decay_f, ret_decay_b, ret_out_norm, mla_q_norm, mla_w_qb, mla_kv_norm, mla_w_kvb, mla_qk_norm_q, mla_qk_norm_k, even_w_out, odd_w_in, swa_q_norm, swa_k_norm, swa_sink, odd_w_out):
    depth = ffn_norm.shape[0]
    assert x_prompt.shape[1] == x_sample.shape[1]
    lp = x_prompt.shape[1] + BLOCK
    layers = []
    for layer in range(depth):
        i = layer // 2
        ffn = [_prep_ffn(ffn_norm[layer, s], ffn_w_gate[layer, s], ffn_w_up[layer, s], ffn_w_down[layer, s])
               for s in range(2)]
        if layer % 2 == 0:
            mix = _prep_even(lp, mix_norm[layer], even_w_in[i], ret_decay_f[i], ret_decay_b[i],
                             ret_out_norm[i], mla_q_norm[i], mla_w_qb[i], mla_kv_norm[i], mla_w_kvb[i],
                             mla_qk_norm_q[i], mla_qk_norm_k[i], even_w_out[i])
        else:
            mix = _prep_odd(lp, mix_norm[layer], odd_w_in[i], swa_q_norm[i], swa_k_norm[i],
                            swa_sink[i], odd_w_out[i])
        layers.append(dict(even=layer % 2 == 0, ffn0=ffn[0], ffn1=ffn[1], mix=mix))
    return (_trunk(x_prompt, meta_tokens, layers), _trunk(x_sample, meta_tokens, layers))
```

```python
import functools

import jax
import jax.numpy as jnp
import numpy as np
from jax import lax
from jax.experimental import pallas as pl
from jax.experimental.pallas import tpu as pltpu

F32 = jnp.float32
BF16 = jnp.bfloat16

D_MODEL = 1024
N_META = 16
BLOCK = 128
FRONT_PAD = BLOCK - N_META
EPS = 1e-6
NEG = -1e30
LOG2E = 1.4426950408889634
LANES = 128
HALF = LANES // 2

A_HEADS = 8
A_DK = 64
A_THETA = 10000.0
B_HEADS = 8
B_Q_LORA = 256
B_KV_LORA = 128
B_NOPE = 64
B_ROPE = 32
B_DV = 64
B_QK = B_NOPE + B_ROPE
B_THETA = 10000.0
C_HEADS = 16
C_KV_HEADS = 4
C_DH = 64
C_ROT = 16
C_THETA = 500000.0
C_WINDOW = 128

VMEM_LIMIT = 56 * 1024 * 1024


def _params(*sem):
    return pltpu.CompilerParams(dimension_semantics=sem, vmem_limit_bytes=VMEM_LIMIT)


def _resident(shape):
    zeros = (0,) * len(shape)
    return pl.BlockSpec(shape, lambda *_: zeros, pipeline_mode=pl.Buffered(1))


def _rms_rows(x, gain):
    return x * lax.rsqrt(jnp.mean(x * x, axis=-1, keepdims=True) + EPS) * gain


def _dot(a, b):
    return jnp.dot(a, b, preferred_element_type=F32)


def _dot_nt(a, b):
    return lax.dot_general(a, b, (((1,), (1,)), ((), ())), preferred_element_type=F32)


def _is_pad_row(row, seq):
    return (row >= seq) & (row < seq + FRONT_PAD)


def _positions(lp):
    row = jnp.arange(lp, dtype=F32)
    seq = lp - BLOCK
    return jnp.where(row < seq, row + N_META, row - (seq + FRONT_PAD))


def _pick(n, options):
    for o in options:
        if n % o == 0:
            return o
    raise ValueError(f"no tile in {options} divides {n}")


def _ffn_core(x, g_ref, wg_ref, wu_ref, wd_ref):
    xn = _rms_rows(x, g_ref[...]).astype(BF16)
    acc = jnp.zeros_like(x)
    fc = wd_ref.shape[1]
    for c in range(wd_ref.shape[0]):
        gate = _dot(xn, wg_ref[:, c * fc:(c + 1) * fc])
        up = _dot(xn, wu_ref[:, c * fc:(c + 1) * fc])
        act = (gate * (1.0 / (1.0 + jnp.exp(-gate))) * up).astype(BF16)
        acc = acc + _dot(act, wd_ref[c])
    return x + 0.5 * acc


def _ffn_body(h_ref, g_ref, wg_ref, wu_ref, wd_ref, o_ref):
    o_ref[...] = _ffn_core(h_ref[...], g_ref, wg_ref, wu_ref, wd_ref)


def _ffn_head_body(x_ref, tail_ref, g_ref, wg_ref, wu_ref, wd_ref, o_ref, *, nt):
    t = pl.program_id(1)

    @pl.when(t < nt)
    def _():
        o_ref[0] = _ffn_core(x_ref[0], g_ref, wg_ref, wu_ref, wd_ref)

    @pl.when(t == nt)
    def _():
        o_ref[0, :BLOCK, :] = tail_ref[...]


def _proj_ffn_body(*refs, n_in):
    tile = lambda r: r[...] if len(r.shape) == 2 else r[0]
    h_ref = refs[0]
    xs = refs[1:1 + n_in]
    ws = refs[1 + n_in:1 + 2 * n_in]
    g_ref, wg_ref, wu_ref, wd_ref, o_ref = refs[1 + 2 * n_in:]
    h2 = tile(h_ref)
    for x_ref, w_ref in zip(xs, ws):
        h2 = h2 + _dot(tile(x_ref), w_ref[...])
    out = _ffn_core(h2, g_ref, wg_ref, wu_ref, wd_ref)
    if len(o_ref.shape) == 2:
        o_ref[...] = out
    else:
        o_ref[0] = out


def _ffn_specs(ffn):
    return [_resident(a.shape) for a in ffn]


def _ffn(h2d, ffn):
    t, d = h2d.shape
    tm = _pick(t, (512, 384, 256, 128))
    row = pl.BlockSpec((tm, d), lambda i: (i, 0))
    return pl.pallas_call(
        _ffn_body,
        grid=(t // tm,),
        in_specs=[row] + _ffn_specs(ffn),
        out_specs=row,
        out_shape=jax.ShapeDtypeStruct((t, d), F32),
        compiler_params=_params("parallel"),
        name="ffn",
    )(h2d, *ffn)


def _ffn_head(x, tail, ffn):
    b, seq, d = x.shape
    tm = _pick(seq, (512, 256, 128))
    nt = seq // tm
    return pl.pallas_call(
        functools.partial(_ffn_head_body, nt=nt),
        grid=(b, nt + 1),
        in_specs=[pl.BlockSpec((1, tm, d), lambda i, t: (i, jnp.minimum(t, nt - 1), 0)),
                  pl.BlockSpec((BLOCK, d), lambda i, t: (0, 0))] + _ffn_specs(ffn),
        out_specs=pl.BlockSpec((1, tm, d), lambda i, t: (i, t, 0)),
        out_shape=jax.ShapeDtypeStruct((b, seq + BLOCK, d), F32),
        compiler_params=_params("parallel", "arbitrary"),
        name="ffn_head",
    )(x, tail, *ffn)


def _proj_ffn_tail(h, xs, ws, ffn):
    b, lp, d = h.shape
    seq = lp - BLOCK
    tm = _pick(seq, (512, 256, 128))
    tok = lambda w: pl.BlockSpec((1, tm, w), lambda i, t: (i, t, 0))
    return pl.pallas_call(
        functools.partial(_proj_ffn_body, n_in=len(xs)),
        grid=(b, seq // tm),
        in_specs=[tok(d)] + [tok(x.shape[-1]) for x in xs] + [_resident(w.shape) for w in ws]
        + _ffn_specs(ffn),
        out_specs=tok(d),
        out_shape=jax.ShapeDtypeStruct((b, seq, d), F32),
        compiler_params=_params("parallel", "parallel"),
        name="proj_ffn_tail",
    )(h, *xs, *ws, *ffn)


def _proj_ffn(h2d, xs, ws, ffn):
    t, d = h2d.shape
    tm = _pick(t, (512, 384, 256, 128))
    row = pl.BlockSpec((tm, d), lambda i: (i, 0))
    x_specs = [pl.BlockSpec((tm, x.shape[1]), lambda i: (i, 0)) for x in xs]
    w_specs = [_resident(w.shape) for w in ws]
    return pl.pallas_call(
        functools.partial(_proj_ffn_body, n_in=len(xs)),
        grid=(t // tm,),
        in_specs=[row] + x_specs + w_specs + _ffn_specs(ffn),
        out_specs=row,
        out_shape=jax.ShapeDtypeStruct((t, d), F32),
        compiler_params=_params("parallel"),
        name="proj_ffn",
    )(h2d, *xs, *ws, *ffn)


def _even_in_body(h_ref, g_ref, win_ref, qn_ref, wqb_ref, wqr_ref, kvn_ref, wkb_ref, wvb_ref,
                  gq_ref, gk_ref, rtab_ref, mtab_ref,
                  rq_ref, rk_ref, rv_ref, rg_ref, mq_ref, mk_ref, mv_ref, *, tm, seq):
    ns = h_ref.shape[0]
    o = B_Q_LORA + B_KV_LORA
    for s in range(ns):
        z = _dot(_rms_rows(h_ref[s], g_ref[...]).astype(BF16), win_ref[...])
        cq = _rms_rows(z[:, :B_Q_LORA], qn_ref[...]).astype(BF16)
        ckv = _rms_rows(z[:, B_Q_LORA:o], kvn_ref[...]).astype(BF16)
        q_all = _dot(cq, wqb_ref[...])
        q_rot = _dot(cq, wqr_ref[...])
        k_all = _dot(ckv, wkb_ref[...])
        v_all = _dot(ckv, wvb_ref[...])
        _even_in_tail(s, z, q_all, q_rot, k_all, v_all, gq_ref, gk_ref, rtab_ref, mtab_ref,
                      rq_ref, rk_ref, rv_ref, rg_ref, mq_ref, mk_ref, mv_ref, tm=tm, seq=seq)


def _even_in_tail(s, z, q_all, q_rot, k_all, v_all, gq_ref, gk_ref, rtab_ref, mtab_ref,
                  rq_ref, rk_ref, rv_ref, rg_ref, mq_ref, mk_ref, mv_ref, *, tm, seq):
    o = B_Q_LORA + B_KV_LORA
    kr = z[:, o:o + LANES]
    kr_rot = z[:, o + LANES:o + 2 * LANES]
    mcos, msin = mtab_ref[0], mtab_ref[1]
    lane = lax.broadcasted_iota(jnp.int32, (tm, LANES), 1)
    v_ones = jnp.where(lane >= HALF, 1.0, 0.0)
    row = lax.broadcasted_iota(jnp.int32, (tm, LANES), 0) + pl.program_id(1) * tm
    bias_lane = lane == B_QK
    q_bias = jnp.where(bias_lane, 1.0, 0.0)
    k_bias = jnp.where(bias_lane & _is_pad_row(row, seq), NEG, 0.0)
    inv = 1.0 / B_QK
    qcos, qsin = mcos * gq_ref[0:1], msin * gq_ref[1:2]
    kcos, ksin = mcos * gk_ref[0:1], msin * gk_ref[1:2]
    kr_sin = kr_rot * ksin
    heads = [slice(hd * LANES, (hd + 1) * LANES) for hd in range(B_HEADS)]
    qhs = [q_all[:, sl] for sl in heads]
    khs = [k_all[:, sl] + kr for sl in heads]
    rqs = [lax.rsqrt(jnp.sum(qh * qh, axis=-1, keepdims=True) * inv + EPS) for qh in qhs]
    rks = [lax.rsqrt(jnp.sum(kh * kh, axis=-1, keepdims=True) * inv + EPS) for kh in khs]
    for sl, qh, kh, rq, rk in zip(heads, qhs, khs, rqs, rks):
        mq_ref[s, :, sl] = ((qh * qcos + q_rot[:, sl] * qsin) * rq + q_bias).astype(BF16)
        mk_ref[s, :, sl] = ((kh * kcos + kr_sin) * rk + k_bias).astype(BF16)
        mv_ref[s, :, sl] = (v_all[:, sl] + v_ones).astype(BF16)

    hw = A_HEADS * A_DK
    o += 2 * LANES
    rcos, rsin = rtab_ref[0], rtab_ref[1]
    for c in range(hw // LANES):
        sl = slice(c * LANES, (c + 1) * LANES)
        q = z[:, o + c * LANES:o + (c + 1) * LANES]
        k = z[:, o + hw + c * LANES:o + hw + (c + 1) * LANES]
        rq_ref[s, :, sl] = (q * rcos + pltpu.roll(q, HALF, 1) * rsin).astype(BF16)
        rk_ref[s, :, sl] = ((k * rcos + pltpu.roll(k, HALF, 1) * rsin) * (A_DK ** -0.5)).astype(BF16)
    rv_ref[s] = z[:, o + 2 * hw:o + 3 * hw].astype(BF16)
    ga = z[:, o + 3 * hw:o + 4 * hw]
    rg_ref[s] = (ga * (1.0 / (1.0 + jnp.exp(-ga)))).astype(BF16)


def _even_in(h, p):
    b, lp, d = h.shape
    tm = _pick(lp, (384, 128))
    ns = 2 if b % 2 == 0 else 1
    tok = lambda w: pl.BlockSpec((ns, tm, w), lambda i, j: (i, j, 0))
    tab = pl.BlockSpec((2, tm, LANES), lambda i, j: (0, j, 0))
    consts = [p["mix_g"], p["w_in"], p["q_norm"], p["w_qb"], p["w_qr"], p["kv_norm"], p["w_kb"], p["w_vb"],
              p["gq"], p["gk"]]
    hw = A_HEADS * A_DK
    widths = [hw, hw, hw, hw, B_HEADS * LANES, B_HEADS * LANES, B_HEADS * LANES]
    return pl.pallas_call(
        functools.partial(_even_in_body, tm=tm, seq=lp - BLOCK),
        grid=(b // ns, lp // tm),
        in_specs=[tok(d)] + [_resident(c.shape) for c in consts] + [tab, tab],
        out_specs=[tok(w) for w in widths],
        out_shape=[jax.ShapeDtypeStruct((b, lp, w), BF16) for w in widths],
        compiler_params=_params("parallel", "parallel"),
        name="even_in",
    )(h, *consts, p["ret_tab"], p["mla_tab"])


def _ret_body(q_ref, k_ref, v_ref, g_ref, dmat_ref, vec_ref, gain_ref, o_ref, acc_ref, kv_ref, st_ref,
              *, nb):
    lane = lax.broadcasted_iota(jnp.int32, (BLOCK, LANES), 1)
    first = lane < HALF
    wkf, wkb, qf, qb = vec_ref[0, 0], vec_ref[0, 1], vec_ref[0, 2], vec_ref[0, 3]
    cf, cb = vec_ref[0, 4][:1], vec_ref[0, 5][:1]
    dcat = jnp.concatenate([dmat_ref[0], dmat_ref[1]], axis=0)
    qk_first = lane % HALF < HALF // 2
    keep0 = jnp.where(qk_first, 1.0, 0.0).astype(BF16)
    keep1 = jnp.where(qk_first, 0.0, 1.0).astype(BF16)
    row2 = lax.broadcasted_iota(jnp.int32, (2 * LANES, LANES), 0) % HALF
    col2 = lax.broadcasted_iota(jnp.int32, (2 * LANES, LANES), 1)
    same2 = (row2 < HALF // 2) == (col2 < HALF)
    unroll = next(u for u in (11, 3, 1) if nb % u == 0)

    def intra(grp, carry):
        ns = [grp * unroll + u for u in range(unroll)]
        rows = [pl.ds(pl.multiple_of(n * BLOCK, BLOCK), BLOCK) for n in ns]
        qs = [q_ref[0, r, :] for r in rows]
        ks = [k_ref[0, r, :] for r in rows]
        vs = [v_ref[0, r, :] for r in rows]
        ss = [_dot_nt(jnp.concatenate([q * keep0, q * keep1], axis=0), k) for q, k in zip(qs, ks)]
        kws = [jnp.concatenate([k.astype(F32) * wkf, k.astype(F32) * wkb], axis=1).T.astype(BF16)
               for k in ks]
        kvs = [_dot(kw, v) for kw, v in zip(kws, vs)]
        ss = [(s * dcat).astype(BF16) for s in ss]
        os = [_dot(s, v) for s, v in zip(ss, vs)]
        for n, r, kv, o2 in zip(ns, rows, kvs, os):
            kv_ref[n] = jnp.where(same2, kv, 0.0)
            acc_ref[r, :] = jnp.where(first, o2[:BLOCK], o2[BLOCK:])
        return carry

    lax.fori_loop(0, nb // unroll, intra, 0)

    def scan(t, carry):
        sf, sb = carry
        i = lax.rem(t + nb - 1, nb)
        j = lax.rem(2 * nb - 2 - t, nb)
        st_ref[i, :LANES, :] = sf.astype(BF16)
        st_ref[j, LANES:, :] = sb.astype(BF16)
        return sf * cf + kv_ref[i, :LANES, :], sb * cb + kv_ref[j, LANES:, :]

    zero = jnp.zeros((LANES, LANES), F32)
    lax.fori_loop(0, nb, scan, (zero, zero))

    def inter(n, carry):
        rows = pl.ds(pl.multiple_of(n * BLOCK, BLOCK), BLOCK)
        qf32 = q_ref[0, rows, :].astype(F32)
        qq = jnp.concatenate([qf32 * qf, qf32 * qb], axis=1).astype(BF16)
        o = acc_ref[rows, :] + _dot(qq, st_ref[n])
        sq = o * o
        inv = 1.0 / HALF
        r0 = lax.rsqrt(jnp.sum(jnp.where(first, sq, 0.0), axis=-1, keepdims=True) * inv + EPS)
        r1 = lax.rsqrt(jnp.sum(jnp.where(first, 0.0, sq), axis=-1, keepdims=True) * inv + EPS)
        out = o * jnp.where(first, r0, r1) * gain_ref[0] * g_ref[0, rows, :].astype(F32)
        o_ref[0, rows, :] = out.astype(BF16)
        return carry

    lax.fori_loop(0, nb, inter, 0, unroll=unroll)


def _retention(rq, rk, rv, rg, p):
    b, lp, hw = rq.shape
    pairs = hw // LANES
    nb = lp // BLOCK
    seq = pl.BlockSpec((1, lp, LANES), lambda i, j: (i, 0, j))
    return pl.pallas_call(
        functools.partial(_ret_body, nb=nb),
        grid=(b, pairs),
        in_specs=[seq, seq, seq, seq,
                  pl.BlockSpec((2, BLOCK, BLOCK), lambda i, j: (j, 0, 0)),
                  pl.BlockSpec((1, 6, BLOCK, LANES), lambda i, j: (j, 0, 0, 0)),
                  pl.BlockSpec((1, 1, LANES), lambda i, j: (j, 0, 0))],
        out_specs=seq,
        out_shape=jax.ShapeDtypeStruct((b, lp, hw), BF16),
        scratch_shapes=[pltpu.VMEM((lp, LANES), F32), pltpu.VMEM((nb, 2 * LANES, LANES), F32),
                        pltpu.VMEM((nb, 2 * LANES, LANES), BF16)],
        compiler_params=_params("parallel", "parallel"),
        name="retention",
    )(rq, rk, rv, rg, p["ret_dmat"], p["ret_vec"], p["ret_gain"])


MXU_TILE = 256
MLA_ROWS = 16


def _mla_body(q_ref, k_ref, v_ref, o_ref, s_scr, p_scr, o_scr, *, tq):
    lp = k_ref.shape[1]
    chunks = [(c, min(MXU_TILE, lp - c)) for c in range(0, lp, MXU_TILE)]
    hk = lp // 2 // MXU_TILE * MXU_TILE

    def scores(hd):
        q = q_ref[0, :, hd * LANES:(hd + 1) * LANES]
        for c, w in chunks:
            s_scr[hd, :, c:c + w] = _dot_nt(q, k_ref[0, c:c + w, hd * LANES:(hd + 1) * LANES])

    def probs(hd):
        for r in range(0, tq, MLA_ROWS):
            sb = s_scr[hd, r:r + MLA_ROWS, :]
            m = jnp.max(sb, axis=-1, keepdims=True)
            p_scr[hd, r:r + MLA_ROWS, :] = jnp.exp2(sb - m).astype(BF16)

    def weighted(hd):
        v = v_ref[0, :, hd * LANES:(hd + 1) * LANES]
        if hk == 0:
            return _dot(p_scr[hd], v)
        return _dot(p_scr[hd, :, :hk], v[:hk]) + _dot(p_scr[hd, :, hk:], v[hk:])

    scores(0)
    scores(1)
    probs(0)
    o_scr[...] = weighted(0)
    probs(1)
    oa = o_scr[...]
    ob = weighted(1)
    lane = lax.broadcasted_iota(jnp.int32, (tq, LANES), 1)
    row = lax.broadcasted_iota(jnp.int32, (tq, LANES), 0) + pl.program_id(2) * tq
    out = jnp.where(lane < HALF, oa * pltpu.roll(1.0 / oa, HALF, 1),
                    pltpu.roll(ob, HALF, 1) * (1.0 / ob))
    o_ref[0] = jnp.where(_is_pad_row(row, lp - BLOCK), 0.0, out).astype(BF16)


def _mla(mq, mk, mv):
    b, lp, _ = mq.shape
    pairs = B_HEADS // 2
    tq = _pick(lp, (384, 128))
    return pl.pallas_call(
        functools.partial(_mla_body, tq=tq),
        grid=(b, pairs, lp // tq),
        in_specs=[pl.BlockSpec((1, tq, 2 * LANES), lambda i, j, t: (i, t, j)),
                  pl.BlockSpec((1, lp, 2 * LANES), lambda i, j, t: (i, 0, j)),
                  pl.BlockSpec((1, lp, 2 * LANES), lambda i, j, t: (i, 0, j))],
        out_specs=pl.BlockSpec((1, tq, LANES), lambda i, j, t: (i, t, j)),
        out_shape=jax.ShapeDtypeStruct((b, lp, B_HEADS * B_DV), BF16),
        scratch_shapes=[pltpu.VMEM((2, tq, lp), F32), pltpu.VMEM((2, tq, lp), BF16),
                        pltpu.VMEM((tq, LANES), F32)],
        compiler_params=_params("parallel", "parallel", "parallel"),
        name="mla",
    )(mq, mk, mv)


def _odd_in_body(h_ref, g_ref, win_ref, gq_ref, gk_ref, tab_ref, q_ref, k_ref, vt_ref, *, tm):
    cos, sin = tab_ref[0], tab_ref[1]
    is_a = lax.broadcasted_iota(jnp.int32, (tm, LANES), 1) % HALF < HALF // 2
    nq = C_HEADS * C_DH
    nk = 2 * C_KV_HEADS * C_DH
    inv = 1.0 / C_DH
    rope = lambda y: y * cos + pltpu.roll(y, HALF, 1) * sin
    zs = [_dot(_rms_rows(h_ref[s], g_ref[...]).astype(BF16), win_ref[...]) for s in range(h_ref.shape[0])]
    for s, z in enumerate(zs):
        xqs = [z[:, c * LANES:(c + 1) * LANES] for c in range(nq // LANES)]
        xks = [z[:, nq + c * LANES:nq + (c + 1) * LANES] for c in range(nk // LANES)]
        rqs = []
        for xq in xqs:
            sq = xq * xq
            ra = lax.rsqrt(jnp.sum(jnp.where(is_a, sq, 0.0), axis=-1, keepdims=True) * inv + EPS)
            rb = lax.rsqrt(jnp.sum(jnp.where(is_a, 0.0, sq), axis=-1, keepdims=True) * inv + EPS)
            rqs.append(jnp.where(is_a, ra, rb))
        rks = [lax.rsqrt(jnp.sum(xk * xk, axis=-1, keepdims=True) * (0.5 * inv) + EPS) for xk in xks]
        for c, (xq, r) in enumerate(zip(xqs, rqs)):
            q_ref[s, :, c * LANES:(c + 1) * LANES] = rope(xq * r * gq_ref[...]).astype(BF16)
        for c, (xk, r) in enumerate(zip(xks, rks)):
            k_ref[s, :, c * LANES:(c + 1) * LANES] = rope(xk * r * gk_ref[...]).astype(BF16)
        vt_ref[s] = z[:, nq + nk:].T.astype(BF16)


def _odd_in(h, p):
    b, lp, d = h.shape
    tm = _pick(lp, (384, 128))
    ns = 2 if b % 2 == 0 else 1
    tok = lambda w: pl.BlockSpec((ns, tm, w), lambda i, j: (i, j, 0))
    tab = pl.BlockSpec((2, tm, LANES), lambda i, j: (0, j, 0))
    consts = [p["mix_g"], p["w_in"], p["gq"], p["gk"]]
    nq, nk, nv = C_HEADS * C_DH, 2 * C_KV_HEADS * C_DH, C_KV_HEADS * C_DH
    return pl.pallas_call(
        functools.partial(_odd_in_body, tm=tm),
        grid=(b // ns, lp // tm),
        in_specs=[tok(d)] + [_resident(c.shape) for c in consts] + [tab],
        out_specs=[tok(nq), tok(nk), pl.BlockSpec((ns, nv, tm), lambda i, j: (i, 0, j))],
        out_shape=[jax.ShapeDtypeStruct((b, lp, nq), BF16), jax.ShapeDtypeStruct((b, lp, nk), BF16),
                   jax.ShapeDtypeStruct((b, nv, lp), BF16)],
        compiler_params=_params("parallel", "parallel"),
        name="odd_in",
    )(h, *consts, p["swa_tab"])


def _swa_body(sink_ref, q_ref, *refs, nb, nq):
    km_ref, k_refs = refs[0], refs[1:nq + 3]
    vm_ref, v_refs = refs[nq + 3], refs[nq + 4:2 * nq + 6]
    o_ref = refs[-1]
    nkeys = 4 * BLOCK
    group = C_HEADS // C_KV_HEADS
    kk = lax.broadcasted_iota(jnp.int32, (nkeys, BLOCK), 0)
    qq = lax.broadcasted_iota(jnp.int32, (nkeys, BLOCK), 1)
    lane = lax.broadcasted_iota(jnp.int32, (BLOCK, LANES), 1)
    rowi = lax.broadcasted_iota(jnp.int32, (BLOCK, LANES), 0)
    is_a = lane % HALF < HALF // 2
    keep = (jnp.where(is_a, 1.0, 0.0).astype(BF16), jnp.where(is_a, 0.0, 1.0).astype(BF16))
    ones = jnp.ones((C_DH, nkeys), BF16)
    seq_len = nb * BLOCK - FRONT_PAD
    masks, rows_ok, scores = [], [], []
    for i in range(nq):
        stored = pl.program_id(1) * nq + i
        n = jnp.where(stored == nb - 1, 0, stored + 1)
        q_pos = n * BLOCK + qq - FRONT_PAD
        k_pos = (n - 2) * BLOCK + kk - FRONT_PAD
        band = (kk >= BLOCK) & (k_pos >= N_META) & (k_pos < seq_len) & (jnp.abs(q_pos - k_pos) <= C_WINDOW)
        masks.append(band | ((kk < BLOCK) & (kk >= FRONT_PAD)))
        rows_ok.append(rowi + n * BLOCK >= FRONT_PAD)
        rows = slice(i * BLOCK, (i + 1) * BLOCK)
        for kv in range(C_KV_HEADS):
            kvs = slice(kv * LANES, (kv + 1) * LANES)
            kcat = jnp.concatenate([r[0, :, kvs] for r in (km_ref, k_refs[i], k_refs[i + 1], k_refs[i + 2])],
                                   axis=0)
            pairs = [q_ref[0, rows, (kv * group // 2 + j) * LANES:(kv * group // 2 + j + 1) * LANES]
                     for j in range(group // 2)]
            qs = jnp.concatenate([p2 * keep[w] for p2 in pairs for w in range(2)], axis=0)
            scores.append(_dot_nt(kcat, qs))
    for i in range(nq):
        rows = slice(i * BLOCK, (i + 1) * BLOCK)
        for kv in range(C_KV_HEADS):
            st = scores[i * C_KV_HEADS + kv]
            vt = jnp.concatenate([r[0, kv * C_DH:(kv + 1) * C_DH, :]
                                  for r in (vm_ref, v_refs[i], v_refs[i + 1], v_refs[i + 2])], axis=1)
            vt = jnp.concatenate([vt, ones], axis=0)
            normed = []
            for g in range(group):
                sink = sink_ref[kv * group + g] * LOG2E
                s = jnp.where(masks[i], st[:, g * BLOCK:(g + 1) * BLOCK], NEG)
                m = jnp.maximum(jnp.max(s, axis=0, keepdims=True), sink)
                ot = _dot(vt, jnp.exp2(s - m).astype(BF16))
                den = ot[C_DH:C_DH + 1, :] + jnp.exp2(sink - m)
                normed.append(ot[:C_DH, :] * (1.0 / den))
            for j in range(group // 2):
                slab = jnp.concatenate([normed[2 * j], normed[2 * j + 1]], axis=0).T
                sl = slice((kv * group // 2 + j) * LANES, (kv * group // 2 + j + 1) * LANES)
                o_ref[0, rows, sl] = jnp.where(rows_ok[i], slab, 0.0).astype(BF16)


def _swa(q, k, vt, sink):
    b, lp, qw = q.shape
    nb = lp // BLOCK
    kw = k.shape[-1]
    vw = vt.shape[1]
    nq = 3 if nb % 3 == 0 else 1
    meta = nb - 1

    def slot(t):
        def index(n):
            raw = n * nq - 1 + t
            return jnp.where((raw < 0) | (raw >= nb), 0, raw)
        return index

    slots = [slot(t) for t in range(nq + 2)]
    qspec = pl.BlockSpec((1, nq * BLOCK, qw), lambda i, n: (i, n, 0))
    kspecs = [pl.BlockSpec((1, BLOCK, kw), lambda i, n: (i, meta, 0))]
    kspecs += [pl.BlockSpec((1, BLOCK, kw), lambda i, n, f=f: (i, f(n), 0)) for f in slots]
    vspecs = [pl.BlockSpec((1, vw, BLOCK), lambda i, n: (i, 0, meta))]
    vspecs += [pl.BlockSpec((1, vw, BLOCK), lambda i, n, f=f: (i, 0, f(n))) for f in slots]
    return pl.pallas_call(
        functools.partial(_swa_body, nb=nb, nq=nq),
        grid=(b, nb // nq),
        in_specs=[pl.BlockSpec(memory_space=pltpu.SMEM), qspec] + kspecs + vspecs,
        out_specs=qspec,
        out_shape=jax.ShapeDtypeStruct((b, lp, qw), BF16),
        compiler_params=_params("parallel", "parallel"),
        name="swa",
    )(sink, q, *([k] * (nq + 3)), *([vt] * (nq + 3)))


def _rope_table(lp, theta, rot, group, offset, scale=1.0):
    half = rot // 2
    pos = _positions(lp)
    lane = np.arange(LANES) % group - offset
    in_lo = (lane >= 0) & (lane < half)
    in_hi = (lane >= half) & (lane < rot)
    idx = np.where(in_lo | in_hi, lane % half, 0)
    inv = theta ** (-jnp.asarray(idx, F32) * 2.0 / rot)
    ang = pos[:, None] * inv[None, :]
    cos = jnp.where(in_lo | in_hi, jnp.cos(ang), 1.0)
    sin = jnp.sin(ang)
    lo = jnp.where(in_lo, -sin, 0.0)
    hi = jnp.where(in_hi, sin, 0.0)
    return jnp.stack([cos, lo, hi]) * scale


def _split_rope_table(lp, theta, rot):
    lanes = np.arange(LANES)
    r = lanes % (HALF // 2)
    active = jnp.asarray(r < rot // 2)
    inv = theta ** (-jnp.asarray(np.where(r < rot // 2, r, 0), F32) * 2.0 / rot)
    ang = _positions(lp)[:, None] * inv[None, :]
    sign = jnp.where(jnp.asarray(lanes < HALF), -1.0, 1.0)
    return jnp.stack([jnp.where(active, jnp.cos(ang), 1.0), jnp.where(active, sign * jnp.sin(ang), 0.0)])


def _pair_lanes(x):
    h, n = x.shape
    return jnp.repeat(x.reshape(h // 2, 2, n).transpose(0, 2, 1), HALF, axis=2)


def _prep_ffn(gain, wg, wu, wd):
    d, f = wg.shape
    return (gain.reshape(1, d), wg.astype(BF16), wu.astype(BF16),
            wd.reshape(f // MXU_TILE, MXU_TILE, d).astype(BF16))


def _prep_even(lp, mix_g, w_in, dec_f, dec_b, ret_norm, q_norm, w_qb, kv_norm, w_kvb, gq, gk, w_out):
    d = w_in.shape[0]
    hw = A_HEADS * A_DK
    cut = 4 * hw + B_Q_LORA + B_KV_LORA
    lanes = np.arange(LANES)
    partner = lanes.copy()
    partner[B_NOPE:B_NOPE + B_ROPE // 2] += B_ROPE // 2
    partner[B_NOPE + B_ROPE // 2:B_QK] -= B_ROPE // 2
    is_rope = jnp.asarray((lanes >= B_NOPE) & (lanes < B_QK))
    rot = lambda w: jnp.where(is_rope, w[..., partner], 0.0)
    kr_cols = jnp.concatenate([jnp.zeros((d, B_NOPE), F32), w_in[:, cut:],
                               jnp.zeros((d, LANES - B_QK), F32)], axis=1)
    qk_perm = lambda w: w.reshape(d, A_HEADS // 2, 2, 2, A_DK // 2).transpose(0, 1, 3, 2, 4).reshape(d, hw)
    w_in2 = jnp.concatenate([w_in[:, 4 * hw:cut], kr_cols, rot(kr_cols),
                             qk_perm(w_in[:, :hw]), qk_perm(w_in[:, hw:2 * hw]), w_in[:, 2 * hw:4 * hw]],
                            axis=1).astype(BF16)
    w_qb2 = jnp.pad(w_qb.reshape(B_Q_LORA, B_HEADS, B_QK), ((0, 0), (0, 0), (0, LANES - B_QK)))
    w_kv3 = w_kvb.reshape(B_KV_LORA, B_HEADS, B_NOPE + B_DV)
    w_kb = jnp.pad(w_kv3[:, :, :B_NOPE], ((0, 0), (0, 0), (0, LANES - B_NOPE)))
    w_vb = jnp.pad(w_kv3[:, :, B_NOPE:], ((0, 0), (0, 0), (0, LANES - B_DV)))
    pad96 = lambda g: jnp.pad(g, (0, LANES - B_QK)).reshape(1, LANES)
    with_rot = lambda g: jnp.concatenate([g, rot(g)], axis=0)
    mla_tab = _rope_table(lp, B_THETA, B_ROPE, LANES, B_NOPE)
    ret_tab = _split_rope_table(lp, A_THETA, A_DK)
    qk_lanes = lambda x: _pair_lanes(x).reshape(x.shape[0] // 2, x.shape[1], 2, 2, A_DK // 2
                                                ).transpose(0, 1, 3, 2, 4).reshape(-1, x.shape[1], LANES)

    lgf = -jnp.exp(dec_f.astype(F32))
    lgb = -jnp.exp(dec_b.astype(F32))
    idx = jnp.arange(BLOCK, dtype=F32)
    diff = idx[:, None] - idx[None, :]
    dmat = (jnp.where(diff >= 0, jnp.exp(lgf[:, None, None] * jnp.maximum(diff, 0.0)), 0.0)
            + jnp.where(diff < 0, jnp.exp(lgb[:, None, None] * jnp.maximum(-diff, 0.0)), 0.0))
    ones = jnp.ones((BLOCK,), F32)
    vec = jnp.stack([
        qk_lanes(jnp.exp(lgf[:, None] * (BLOCK - 1 - idx)[None, :])),
        qk_lanes(jnp.exp(lgb[:, None] * idx[None, :])),
        qk_lanes(jnp.exp(lgf[:, None] * (idx + 1.0)[None, :])),
        qk_lanes(jnp.exp(lgb[:, None] * (BLOCK - idx)[None, :])),
        _pair_lanes(jnp.exp(BLOCK * lgf)[:, None] * ones[None, :]),
        _pair_lanes(jnp.exp(BLOCK * lgb)[:, None] * ones[None, :]),
    ], axis=1)
    half = A_HEADS * A_DK
    return dict(
        mix_g=mix_g.reshape(1, d), w_in=w_in2,
        q_norm=q_norm.reshape(1, -1), w_qb=w_qb2.reshape(B_Q_LORA, -1).astype(BF16),
        w_qr=rot(w_qb2).reshape(B_Q_LORA, -1).astype(BF16),
        kv_norm=kv_norm.reshape(1, -1), w_kb=w_kb.reshape(B_KV_LORA, -1).astype(BF16),
        w_vb=w_vb.reshape(B_KV_LORA, -1).astype(BF16),
        gq=with_rot(pad96(gq) * (B_QK ** -0.5 * LOG2E)), gk=with_rot(pad96(gk)),
        ret_tab=ret_tab,
        mla_tab=jnp.stack([mla_tab[0], mla_tab[1] + mla_tab[2]]),
        ret_dmat=dmat, ret_vec=vec,
        ret_gain=ret_norm.reshape(A_HEADS // 2, 1, LANES),
        w_out=(w_out[:half].astype(BF16), w_out[half:].astype(BF16)),
    )


def _prep_odd(lp, mix_g, w_in, gq, gk, sink, w_out):
    d = w_in.shape[0]
    nq = C_HEADS * C_DH
    nkv = C_KV_HEADS * C_DH
    qtr, hr = C_DH // 2, C_ROT // 2
    perm = np.concatenate([np.arange(hr), np.arange(C_ROT, C_ROT + qtr - hr),
                           np.arange(hr, C_ROT), np.arange(C_ROT + qtr - hr, C_DH)])
    split = lambda w, heads: w.reshape(-1, heads, C_DH)[:, :, perm].reshape(-1, heads, 2, qtr)
    q_cols = lambda w: split(w, C_HEADS).reshape(-1, C_HEADS // 2, 2, 2, qtr).transpose(0, 1, 3, 2, 4
                                                                                         ).reshape(-1, nq)
    k_cols = lambda w: jnp.repeat(split(w, C_KV_HEADS)[:, :, :, None, :], 2, axis=3).reshape(-1, 2 * nkv)
    w_in2 = jnp.concatenate([q_cols(w_in[:, :nq]), k_cols(w_in[:, nq:nq + nkv]), w_in[:, nq + nkv:]], axis=1)
    slab_gain = lambda g: jnp.repeat(g[perm].reshape(2, 1, qtr), 2, axis=1).reshape(1, LANES)
    return dict(
        mix_g=mix_g.reshape(1, d), w_in=w_in2.astype(BF16),
        gq=slab_gain(gq) * (C_DH ** -0.5 * LOG2E), gk=slab_gain(gk),
        swa_tab=_split_rope_table(lp, C_THETA, C_ROT),
        sink=sink.astype(F32), w_out=(w_out.astype(BF16),),
    )


def _trunk(x, meta, layers):
    b, seq, d = x.shape
    lp = seq + BLOCK
    tail = jnp.concatenate([jnp.zeros((FRONT_PAD, d), x.dtype), meta.astype(x.dtype)], axis=0)
    tail = _ffn(tail, layers[0]["ffn0"])
    h = _ffn_head(x, tail, layers[0]["ffn0"]).reshape(b * lp, d)
    for i, layer in enumerate(layers):
        h3 = h.reshape(b, lp, d)
        mp = layer["mix"]
        if layer["even"]:
            rq, rk, rv, rg, mq, mk, mv = _even_in(h3, mp)
            xs = [_retention(rq, rk, rv, rg, mp), _mla(mq, mk, mv)]
        else:
            q, k, v = _odd_in(h3, mp)
            xs = [_swa(q, k, v, mp["sink"])]
        if i + 1 == len(layers):
            return _proj_ffn_tail(h3, xs, mp["w_out"], layer["ffn1"])
        xs = [a.reshape(b * lp, a.shape[-1]) for a in xs]
        h = _proj_ffn(h, xs, mp["w_out"], layer["ffn1"])
        h = _ffn(h, layers[i + 1]["ffn0"])


def kernel(x_prompt, x_sample, meta_tokens, ffn_norm, ffn_w_gate, ffn_w_up, ffn_w_down, mix_norm, even_w_in, ret_decay_f, ret_decay_b, ret_out_norm, mla_q_norm, mla_w_qb, mla_kv_norm, mla_w_kvb, mla_qk_norm_q, mla_qk_norm_k, even_w_out, odd_w_in, swa_q_norm, swa_k_norm, swa_sink, odd_w_out):
    depth = ffn_norm.shape[0]
    assert x_prompt.shape[1] == x_sample.shape[1]
    lp = x_prompt.shape[1] + BLOCK
    layers = []
    for layer in range(depth):
        i = layer // 2
        ffn = [_prep_ffn(ffn_norm[layer, s], ffn_w_gate[layer, s], ffn_w_up[layer, s], ffn_w_down[layer, s])
               for s in range(2)]
        if layer % 2 == 0:
            mix = _prep_even(lp, mix_norm[layer], even_w_in[i], ret_decay_f[i], ret_decay_b[i],
                             ret_out_norm[i], mla_q_norm[i], mla_w_qb[i], mla_kv_norm[i], mla_w_kvb[i],
                             mla_qk_norm_q[i], mla_qk_norm_k[i], even_w_out[i])
        else:
            mix = _prep_odd(lp, mix_norm[layer], odd_w_in[i], swa_q_norm[i], swa_k_norm[i],
                            swa_sink[i], odd_w_out[i])
        layers.append(dict(even=layer % 2 == 0, ffn0=ffn[0], ffn1=ffn[1], mix=mix))
    return (_trunk(x_prompt, meta_tokens, layers), _trunk(x_sample, meta_tokens, layers))
```

```python
import functools

import jax
import jax.numpy as jnp
import numpy as np
from jax import lax
from jax.experimental import pallas as pl
from jax.experimental.pallas import tpu as pltpu

F32 = jnp.float32
BF16 = jnp.bfloat16

D_MODEL = 1024
N_META = 16
BLOCK = 128
FRONT_PAD = BLOCK - N_META
EPS = 1e-6
NEG = -1e30
LOG2E = 1.4426950408889634
LANES = 128
HALF = LANES // 2

A_HEADS = 8
A_DK = 64
A_THETA = 10000.0
B_HEADS = 8
B_Q_LORA = 256
B_KV_LORA = 128
B_NOPE = 64
B_ROPE = 32
B_DV = 64
B_QK = B_NOPE + B_ROPE
B_THETA = 10000.0
C_HEADS = 16
C_KV_HEADS = 4
C_DH = 64
C_ROT = 16
C_THETA = 500000.0
C_WINDOW = 128

VMEM_LIMIT = 56 * 1024 * 1024
MXU_TILE = 256

FFN_TILES = (512, 384, 256, 128)
FFN_CHUNK = MXU_TILE
SEQ_TILES = (384, 128)
SEQS_PER_STEP = (2, 1)
RET_GROUPS = (11, 3, 1)
SWA_BLOCKS = (3, 1)
MLA_ROWS = 16


def _params(*sem):
    return pltpu.CompilerParams(dimension_semantics=sem, vmem_limit_bytes=VMEM_LIMIT)


def _resident(shape):
    zeros = (0,) * len(shape)
    return pl.BlockSpec(shape, lambda *_: zeros, pipeline_mode=pl.Buffered(1))


def _rms_rows(x, gain):
    return x * lax.rsqrt(jnp.mean(x * x, axis=-1, keepdims=True) + EPS) * gain


def _dot(a, b):
    return jnp.dot(a, b, preferred_element_type=F32)


def _dot_nt(a, b):
    return lax.dot_general(a, b, (((1,), (1,)), ((), ())), preferred_element_type=F32)


def _is_pad_row(row, seq):
    return (row >= seq) & (row < seq + FRONT_PAD)


def _positions(lp):
    row = jnp.arange(lp, dtype=F32)
    seq = lp - BLOCK
    return jnp.where(row < seq, row + N_META, row - (seq + FRONT_PAD))


def _pick(n, options):
    for o in options:
        if n % o == 0:
            return o
    raise ValueError(f"no tile in {options} divides {n}")


def _ffn_core(x, g_ref, wg_ref, wu_ref, wd_ref):
    xn = _rms_rows(x, g_ref[...]).astype(BF16)
    acc = jnp.zeros_like(x)
    fc = wd_ref.shape[1]
    for c in range(wd_ref.shape[0]):
        gate = _dot(xn, wg_ref[:, c * fc:(c + 1) * fc])
        up = _dot(xn, wu_ref[:, c * fc:(c + 1) * fc])
        act = (gate * (1.0 / (1.0 + jnp.exp(-gate))) * up).astype(BF16)
        acc = acc + _dot(act, wd_ref[c])
    return x + 0.5 * acc


def _ffn_body(h_ref, g_ref, wg_ref, wu_ref, wd_ref, o_ref):
    o_ref[...] = _ffn_core(h_ref[...], g_ref, wg_ref, wu_ref, wd_ref)


def _ffn_head_body(x_ref, tail_ref, g_ref, wg_ref, wu_ref, wd_ref, o_ref, *, nt):
    t = pl.program_id(1)

    @pl.when(t < nt)
    def _():
        o_ref[0] = _ffn_core(x_ref[0], g_ref, wg_ref, wu_ref, wd_ref)

    @pl.when(t == nt)
    def _():
        o_ref[0, :BLOCK, :] = tail_ref[...]


def _proj_ffn_body(*refs, n_in):
    tile = lambda r: r[...] if len(r.shape) == 2 else r[0]
    h_ref = refs[0]
    xs = refs[1:1 + n_in]
    ws = refs[1 + n_in:1 + 2 * n_in]
    g_ref, wg_ref, wu_ref, wd_ref, o_ref = refs[1 + 2 * n_in:]
    h2 = tile(h_ref)
    for x_ref, w_ref in zip(xs, ws):
        h2 = h2 + _dot(tile(x_ref), w_ref[...])
    out = _ffn_core(h2, g_ref, wg_ref, wu_ref, wd_ref)
    if len(o_ref.shape) == 2:
        o_ref[...] = out
    else:
        o_ref[0] = out


def _ffn_specs(ffn):
    return [_resident(a.shape) for a in ffn]


def _ffn(h2d, ffn):
    t, d = h2d.shape
    tm = _pick(t, FFN_TILES)
    row = pl.BlockSpec((tm, d), lambda i: (i, 0))
    return pl.pallas_call(
        _ffn_body,
        grid=(t // tm,),
        in_specs=[row] + _ffn_specs(ffn),
        out_specs=row,
        out_shape=jax.ShapeDtypeStruct((t, d), F32),
        compiler_params=_params("parallel"),
        name="ffn",
    )(h2d, *ffn)


def _ffn_head(x, tail, ffn):
    b, seq, d = x.shape
    tm = _pick(seq, FFN_TILES)
    nt = seq // tm
    return pl.pallas_call(
        functools.partial(_ffn_head_body, nt=nt),
        grid=(b, nt + 1),
        in_specs=[pl.BlockSpec((1, tm, d), lambda i, t: (i, jnp.minimum(t, nt - 1), 0)),
                  pl.BlockSpec((BLOCK, d), lambda i, t: (0, 0))] + _ffn_specs(ffn),
        out_specs=pl.BlockSpec((1, tm, d), lambda i, t: (i, t, 0)),
        out_shape=jax.ShapeDtypeStruct((b, seq + BLOCK, d), F32),
        compiler_params=_params("parallel", "arbitrary"),
        name="ffn_head",
    )(x, tail, *ffn)


def _proj_ffn_tail(h, xs, ws, ffn):
    b, lp, d = h.shape
    seq = lp - BLOCK
    tm = _pick(seq, FFN_TILES)
    tok = lambda w: pl.BlockSpec((1, tm, w), lambda i, t: (i, t, 0))
    return pl.pallas_call(
        functools.partial(_proj_ffn_body, n_in=len(xs)),
        grid=(b, seq // tm),
        in_specs=[tok(d)] + [tok(x.shape[-1]) for x in xs] + [_resident(w.shape) for w in ws]
        + _ffn_specs(ffn),
        out_specs=tok(d),
        out_shape=jax.ShapeDtypeStruct((b, seq, d), F32),
        compiler_params=_params("parallel", "parallel"),
        name="proj_ffn_tail",
    )(h, *xs, *ws, *ffn)


def _proj_ffn(h2d, xs, ws, ffn):
    t, d = h2d.shape
    tm = _pick(t, FFN_TILES)
    row = pl.BlockSpec((tm, d), lambda i: (i, 0))
    x_specs = [pl.BlockSpec((tm, x.shape[1]), lambda i: (i, 0)) for x in xs]
    w_specs = [_resident(w.shape) for w in ws]
    return pl.pallas_call(
        functools.partial(_proj_ffn_body, n_in=len(xs)),
        grid=(t // tm,),
        in_specs=[row] + x_specs + w_specs + _ffn_specs(ffn),
        out_specs=row,
        out_shape=jax.ShapeDtypeStruct((t, d), F32),
        compiler_params=_params("parallel"),
        name="proj_ffn",
    )(h2d, *xs, *ws, *ffn)


def _even_in_body(h_ref, g_ref, win_ref, qn_ref, wqb_ref, wqr_ref, kvn_ref, wkb_ref, wvb_ref,
                  gq_ref, gk_ref, rtab_ref, mtab_ref,
                  rq_ref, rk_ref, rv_ref, rg_ref, mq_ref, mk_ref, mv_ref, *, tm, seq):
    ns = h_ref.shape[0]
    o = B_Q_LORA + B_KV_LORA
    for s in range(ns):
        z = _dot(_rms_rows(h_ref[s], g_ref[...]).astype(BF16), win_ref[...])
        cq = _rms_rows(z[:, :B_Q_LORA], qn_ref[...]).astype(BF16)
        ckv = _rms_rows(z[:, B_Q_LORA:o], kvn_ref[...]).astype(BF16)
        q_all = _dot(cq, wqb_ref[...])
        q_rot = _dot(cq, wqr_ref[...])
        k_all = _dot(ckv, wkb_ref[...])
        v_all = _dot(ckv, wvb_ref[...])
        _even_in_tail(s, z, q_all, q_rot, k_all, v_all, gq_ref, gk_ref, rtab_ref, mtab_ref,
                      rq_ref, rk_ref, rv_ref, rg_ref, mq_ref, mk_ref, mv_ref, tm=tm, seq=seq)


def _even_in_tail(s, z, q_all, q_rot, k_all, v_all, gq_ref, gk_ref, rtab_ref, mtab_ref,
                  rq_ref, rk_ref, rv_ref, rg_ref, mq_ref, mk_ref, mv_ref, *, tm, seq):
    o = B_Q_LORA + B_KV_LORA
    kr = z[:, o:o + LANES]
    kr_rot = z[:, o + LANES:o + 2 * LANES]
    mcos, msin = mtab_ref[0], mtab_ref[1]
    lane = lax.broadcasted_iota(jnp.int32, (tm, LANES), 1)
    v_ones = jnp.where(lane >= HALF, 1.0, 0.0)
    row = lax.broadcasted_iota(jnp.int32, (tm, LANES), 0) + pl.program_id(1) * tm
    bias_lane = lane == B_QK
    q_bias = jnp.where(bias_lane, 1.0, 0.0)
    k_bias = jnp.where(bias_lane & _is_pad_row(row, seq), NEG, 0.0)
    inv = 1.0 / B_QK
    qcos, qsin = mcos * gq_ref[0:1], msin * gq_ref[1:2]
    kcos, ksin = mcos * gk_ref[0:1], msin * gk_ref[1:2]
    kr_sin = kr_rot * ksin
    heads = [slice(hd * LANES, (hd + 1) * LANES) for hd in range(B_HEADS)]
    qhs = [q_all[:, sl] for sl in heads]
    khs = [k_all[:, sl] + kr for sl in heads]
    rqs = [lax.rsqrt(jnp.sum(qh * qh, axis=-1, keepdims=True) * inv + EPS) for qh in qhs]
    rks = [lax.rsqrt(jnp.sum(kh * kh, axis=-1, keepdims=True) * inv + EPS) for kh in khs]
    for sl, qh, kh, rq, rk in zip(heads, qhs, khs, rqs, rks):
        mq_ref[s, :, sl] = ((qh * qcos + q_rot[:, sl] * qsin) * rq + q_bias).astype(BF16)
        mk_ref[s, :, sl] = ((kh * kcos + kr_sin) * rk + k_bias).astype(BF16)
        mv_ref[s, :, sl] = (v_all[:, sl] + v_ones).astype(BF16)

    hw = A_HEADS * A_DK
    o += 2 * LANES
    rcos, rsin = rtab_ref[0], rtab_ref[1]
    for c in range(hw // LANES):
        sl = slice(c * LANES, (c + 1) * LANES)
        q = z[:, o + c * LANES:o + (c + 1) * LANES]
        k = z[:, o + hw + c * LANES:o + hw + (c + 1) * LANES]
        rq_ref[s, :, sl] = (q * rcos + pltpu.roll(q, HALF, 1) * rsin).astype(BF16)
        rk_ref[s, :, sl] = ((k * rcos + pltpu.roll(k, HALF, 1) * rsin) * (A_DK ** -0.5)).astype(BF16)
    rv_ref[s] = z[:, o + 2 * hw:o + 3 * hw].astype(BF16)
    ga = z[:, o + 3 * hw:o + 4 * hw]
    rg_ref[s] = (ga * (1.0 / (1.0 + jnp.exp(-ga)))).astype(BF16)


def _even_in(h, p):
    b, lp, d = h.shape
    tm = _pick(lp, SEQ_TILES)
    ns = _pick(b, SEQS_PER_STEP)
    tok = lambda w: pl.BlockSpec((ns, tm, w), lambda i, j: (i, j, 0))
    tab = pl.BlockSpec((2, tm, LANES), lambda i, j: (0, j, 0))
    consts = [p["mix_g"], p["w_in"], p["q_norm"], p["w_qb"], p["w_qr"], p["kv_norm"], p["w_kb"], p["w_vb"],
              p["gq"], p["gk"]]
    hw = A_HEADS * A_DK
    widths = [hw, hw, hw, hw, B_HEADS * LANES, B_HEADS * LANES, B_HEADS * LANES]
    return pl.pallas_call(
        functools.partial(_even_in_body, tm=tm, seq=lp - BLOCK),
        grid=(b // ns, lp // tm),
        in_specs=[tok(d)] + [_resident(c.shape) for c in consts] + [tab, tab],
        out_specs=[tok(w) for w in widths],
        out_shape=[jax.ShapeDtypeStruct((b, lp, w), BF16) for w in widths],
        compiler_params=_params("parallel", "parallel"),
        name="even_in",
    )(h, *consts, p["ret_tab"], p["mla_tab"])


def _ret_body(q_ref, k_ref, v_ref, g_ref, dmat_ref, vec_ref, gain_ref, o_ref, acc_ref, kv_ref, st_ref,
              *, nb):
    lane = lax.broadcasted_iota(jnp.int32, (BLOCK, LANES), 1)
    first = lane < HALF
    wkf, wkb, qf, qb = vec_ref[0, 0], vec_ref[0, 1], vec_ref[0, 2], vec_ref[0, 3]
    cf, cb = vec_ref[0, 4][:1], vec_ref[0, 5][:1]
    dcat = jnp.concatenate([dmat_ref[0], dmat_ref[1]], axis=0)
    qk_first = lane % HALF < HALF // 2
    keep0 = jnp.where(qk_first, 1.0, 0.0).astype(BF16)
    keep1 = jnp.where(qk_first, 0.0, 1.0).astype(BF16)
    row2 = lax.broadcasted_iota(jnp.int32, (2 * LANES, LANES), 0) % HALF
    col2 = lax.broadcasted_iota(jnp.int32, (2 * LANES, LANES), 1)
    same2 = (row2 < HALF // 2) == (col2 < HALF)
    unroll = _pick(nb, RET_GROUPS)

    def intra(grp, carry):
        ns = [grp * unroll + u for u in range(unroll)]
        rows = [pl.ds(pl.multiple_of(n * BLOCK, BLOCK), BLOCK) for n in ns]
        qs = [q_ref[0, r, :] for r in rows]
        ks = [k_ref[0, r, :] for r in rows]
        vs = [v_ref[0, r, :] for r in rows]
        ss = [_dot_nt(jnp.concatenate([q * keep0, q * keep1], axis=0), k) for q, k in zip(qs, ks)]
        kws = [jnp.concatenate([k.astype(F32) * wkf, k.astype(F32) * wkb], axis=1).T.astype(BF16)
               for k in ks]
        kvs = [_dot(kw, v) for kw, v in zip(kws, vs)]
        ss = [(s * dcat).astype(BF16) for s in ss]
        os = [_dot(s, v) for s, v in zip(ss, vs)]
        for n, r, kv, o2 in zip(ns, rows, kvs, os):
            kv_ref[n] = jnp.where(same2, kv, 0.0)
            acc_ref[r, :] = jnp.where(first, o2[:BLOCK], o2[BLOCK:])
        return carry

    lax.fori_loop(0, nb // unroll, intra, 0)

    def scan(t, carry):
        sf, sb = carry
        i = lax.rem(t + nb - 1, nb)
        j = lax.rem(2 * nb - 2 - t, nb)
        st_ref[i, :LANES, :] = sf.astype(BF16)
        st_ref[j, LANES:, :] = sb.astype(BF16)
        return sf * cf + kv_ref[i, :LANES, :], sb * cb + kv_ref[j, LANES:, :]

    zero = jnp.zeros((LANES, LANES), F32)
    lax.fori_loop(0, nb, scan, (zero, zero))

    def inter(n, carry):
        rows = pl.ds(pl.multiple_of(n * BLOCK, BLOCK), BLOCK)
        qf32 = q_ref[0, rows, :].astype(F32)
        qq = jnp.concatenate([qf32 * qf, qf32 * qb], axis=1).astype(BF16)
        o = acc_ref[rows, :] + _dot(qq, st_ref[n])
        sq = o * o
        inv = 1.0 / HALF
        r0 = lax.rsqrt(jnp.sum(jnp.where(first, sq, 0.0), axis=-1, keepdims=True) * inv + EPS)
        r1 = lax.rsqrt(jnp.sum(jnp.where(first, 0.0, sq), axis=-1, keepdims=True) * inv + EPS)
        out = o * jnp.where(first, r0, r1) * gain_ref[0] * g_ref[0, rows, :].astype(F32)
        o_ref[0, rows, :] = out.astype(BF16)
        return carry

    lax.fori_loop(0, nb, inter, 0, unroll=unroll)


def _retention(rq, rk, rv, rg, p):
    b, lp, hw = rq.shape
    pairs = hw // LANES
    nb = lp // BLOCK
    seq = pl.BlockSpec((1, lp, LANES), lambda i, j: (i, 0, j))
    return pl.pallas_call(
        functools.partial(_ret_body, nb=nb),
        grid=(b, pairs),
        in_specs=[seq, seq, seq, seq,
                  pl.BlockSpec((2, BLOCK, BLOCK), lambda i, j: (j, 0, 0)),
                  pl.BlockSpec((1, 6, BLOCK, LANES), lambda i, j: (j, 0, 0, 0)),
                  pl.BlockSpec((1, 1, LANES), lambda i, j: (j, 0, 0))],
        out_specs=seq,
        out_shape=jax.ShapeDtypeStruct((b, lp, hw), BF16),
        scratch_shapes=[pltpu.VMEM((lp, LANES), F32), pltpu.VMEM((nb, 2 * LANES, LANES), F32),
                        pltpu.VMEM((nb, 2 * LANES, LANES), BF16)],
        compiler_params=_params("parallel", "parallel"),
        name="retention",
    )(rq, rk, rv, rg, p["ret_dmat"], p["ret_vec"], p["ret_gain"])


def _mla_body(q_ref, k_ref, v_ref, o_ref, s_scr, p_scr, o_scr, *, tq):
    lp = k_ref.shape[1]
    chunks = [(c, min(MXU_TILE, lp - c)) for c in range(0, lp, MXU_TILE)]
    hk = lp // 2 // MXU_TILE * MXU_TILE

    def scores(hd):
        q = q_ref[0, :, hd * LANES:(hd + 1) * LANES]
        for c, w in chunks:
            s_scr[hd, :, c:c + w] = _dot_nt(q, k_ref[0, c:c + w, hd * LANES:(hd + 1) * LANES])

    def probs(hd):
        for r in range(0, tq, MLA_ROWS):
            sb = s_scr[hd, r:r + MLA_ROWS, :]
            m = jnp.max(sb, axis=-1, keepdims=True)
            p_scr[hd, r:r + MLA_ROWS, :] = jnp.exp2(sb - m).astype(BF16)

    def weighted(hd):
        v = v_ref[0, :, hd * LANES:(hd + 1) * LANES]
        if hk == 0:
            return _dot(p_scr[hd], v)
        return _dot(p_scr[hd, :, :hk], v[:hk]) + _dot(p_scr[hd, :, hk:], v[hk:])

    scores(0)
    scores(1)
    probs(0)
    o_scr[...] = weighted(0)
    probs(1)
    oa = o_scr[...]
    ob = weighted(1)
    lane = lax.broadcasted_iota(jnp.int32, (tq, LANES), 1)
    row = lax.broadcasted_iota(jnp.int32, (tq, LANES), 0) + pl.program_id(2) * tq
    out = jnp.where(lane < HALF, oa * pltpu.roll(1.0 / oa, HALF, 1),
                    pltpu.roll(ob, HALF, 1) * (1.0 / ob))
    o_ref[0] = jnp.where(_is_pad_row(row, lp - BLOCK), 0.0, out).astype(BF16)


def _mla(mq, mk, mv):
    b, lp, _ = mq.shape
    pairs = B_HEADS // 2
    tq = _pick(lp, SEQ_TILES)
    return pl.pallas_call(
        functools.partial(_mla_body, tq=tq),
        grid=(b, pairs, lp // tq),
        in_specs=[pl.BlockSpec((1, tq, 2 * LANES), lambda i, j, t: (i, t, j)),
                  pl.BlockSpec((1, lp, 2 * LANES), lambda i, j, t: (i, 0, j)),
                  pl.BlockSpec((1, lp, 2 * LANES), lambda i, j, t: (i, 0, j))],
        out_specs=pl.BlockSpec((1, tq, LANES), lambda i, j, t: (i, t, j)),
        out_shape=jax.ShapeDtypeStruct((b, lp, B_HEADS * B_DV), BF16),
        scratch_shapes=[pltpu.VMEM((2, tq, lp), F32), pltpu.VMEM((2, tq, lp), BF16),
                        pltpu.VMEM((tq, LANES), F32)],
        compiler_params=_params("parallel", "parallel", "parallel"),
        name="mla",
    )(mq, mk, mv)


def _odd_in_body(h_ref, g_ref, win_ref, gq_ref, gk_ref, tab_ref, q_ref, k_ref, vt_ref, *, tm):
    cos, sin = tab_ref[0], tab_ref[1]
    is_a = lax.broadcasted_iota(jnp.int32, (tm, LANES), 1) % HALF < HALF // 2
    nq = C_HEADS * C_DH
    nk = 2 * C_KV_HEADS * C_DH
    inv = 1.0 / C_DH
    rope = lambda y: y * cos + pltpu.roll(y, HALF, 1) * sin
    zs = [_dot(_rms_rows(h_ref[s], g_ref[...]).astype(BF16), win_ref[...]) for s in range(h_ref.shape[0])]
    for s, z in enumerate(zs):
        xqs = [z[:, c * LANES:(c + 1) * LANES] for c in range(nq // LANES)]
        xks = [z[:, nq + c * LANES:nq + (c + 1) * LANES] for c in range(nk // LANES)]
        rqs = []
        for xq in xqs:
            sq = xq * xq
            ra = lax.rsqrt(jnp.sum(jnp.where(is_a, sq, 0.0), axis=-1, keepdims=True) * inv + EPS)
            rb = lax.rsqrt(jnp.sum(jnp.where(is_a, 0.0, sq), axis=-1, keepdims=True) * inv + EPS)
            rqs.append(jnp.where(is_a, ra, rb))
        rks = [lax.rsqrt(jnp.sum(xk * xk, axis=-1, keepdims=True) * (0.5 * inv) + EPS) for xk in xks]
        for c, (xq, r) in enumerate(zip(xqs, rqs)):
            q_ref[s, :, c * LANES:(c + 1) * LANES] = rope(xq * r * gq_ref[...]).astype(BF16)
        for c, (xk, r) in enumerate(zip(xks, rks)):
            k_ref[s, :, c * LANES:(c + 1) * LANES] = rope(xk * r * gk_ref[...]).astype(BF16)
        vt_ref[s] = z[:, nq + nk:].T.astype(BF16)


def _odd_in(h, p):
    b, lp, d = h.shape
    tm = _pick(lp, SEQ_TILES)
    ns = _pick(b, SEQS_PER_STEP)
    tok = lambda w: pl.BlockSpec((ns, tm, w), lambda i, j: (i, j, 0))
    tab = pl.BlockSpec((2, tm, LANES), lambda i, j: (0, j, 0))
    consts = [p["mix_g"], p["w_in"], p["gq"], p["gk"]]
    nq, nk, nv = C_HEADS * C_DH, 2 * C_KV_HEADS * C_DH, C_KV_HEADS * C_DH
    return pl.pallas_call(
        functools.partial(_odd_in_body, tm=tm),
        grid=(b // ns, lp // tm),
        in_specs=[tok(d)] + [_resident(c.shape) for c in consts] + [tab],
        out_specs=[tok(nq), tok(nk), pl.BlockSpec((ns, nv, tm), lambda i, j: (i, 0, j))],
        out_shape=[jax.ShapeDtypeStruct((b, lp, nq), BF16), jax.ShapeDtypeStruct((b, lp, nk), BF16),
                   jax.ShapeDtypeStruct((b, nv, lp), BF16)],
        compiler_params=_params("parallel", "parallel"),
        name="odd_in",
    )(h, *consts, p["swa_tab"])


def _swa_body(sink_ref, q_ref, *refs, nb, nq):
    km_ref, k_refs = refs[0], refs[1:nq + 3]
    vm_ref, v_refs = refs[nq + 3], refs[nq + 4:2 * nq + 6]
    o_ref = refs[-1]
    nband = 3 * BLOCK
    group = C_HEADS // C_KV_HEADS
    kk = lax.broadcasted_iota(jnp.int32, (nband, BLOCK), 0)
    qq = lax.broadcasted_iota(jnp.int32, (nband, BLOCK), 1)
    lane = lax.broadcasted_iota(jnp.int32, (BLOCK, LANES), 1)
    rowi = lax.broadcasted_iota(jnp.int32, (BLOCK, LANES), 0)
    is_a = lane % HALF < HALF // 2
    keep = (jnp.where(is_a, 1.0, 0.0).astype(BF16), jnp.where(is_a, 0.0, 1.0).astype(BF16))
    ones = jnp.ones((C_DH, BLOCK + nband), BF16)
    no_pad_keys = jnp.zeros((FRONT_PAD, BLOCK), BF16)
    seq_len = nb * BLOCK - FRONT_PAD
    masks, rows_ok, meta_scores, band_scores = [], [], [], []
    for i in range(nq):
        stored = pl.program_id(1) * nq + i
        n = jnp.where(stored == nb - 1, 0, stored + 1)
        q_pos = n * BLOCK + qq - FRONT_PAD
        k_pos = (n - 1) * BLOCK + kk - FRONT_PAD
        masks.append((k_pos >= N_META) & (k_pos < seq_len) & (jnp.abs(q_pos - k_pos) <= C_WINDOW))
        rows_ok.append(rowi + n * BLOCK >= FRONT_PAD)
        rows = slice(i * BLOCK, (i + 1) * BLOCK)
        for kv in range(C_KV_HEADS):
            kvs = slice(kv * LANES, (kv + 1) * LANES)
            kcat = jnp.concatenate([r[0, :, kvs] for r in (km_ref, k_refs[i], k_refs[i + 1], k_refs[i + 2])],
                                   axis=0)
            pairs = [q_ref[0, rows, (kv * group // 2 + j) * LANES:(kv * group // 2 + j + 1) * LANES]
                     for j in range(group // 2)]
            qs = jnp.concatenate([p2 * keep[w] for p2 in pairs for w in range(2)], axis=0)
            st = _dot_nt(kcat, qs)
            meta_scores.append(st[:N_META])
            band_scores.append(st[N_META:])
    for i in range(nq):
        rows = slice(i * BLOCK, (i + 1) * BLOCK)
        for kv in range(C_KV_HEADS):
            sm_all, sb_all = meta_scores[i * C_KV_HEADS + kv], band_scores[i * C_KV_HEADS + kv]
            vt = jnp.concatenate([r[0, kv * C_DH:(kv + 1) * C_DH, :]
                                  for r in (vm_ref, v_refs[i], v_refs[i + 1], v_refs[i + 2])], axis=1)
            vt = jnp.concatenate([vt, ones], axis=0)
            normed = []
            for g in range(group):
                sink = sink_ref[kv * group + g] * LOG2E
                sm = sm_all[:, g * BLOCK:(g + 1) * BLOCK]
                sb = jnp.where(masks[i], sb_all[:, g * BLOCK:(g + 1) * BLOCK], NEG)
                m = jnp.maximum(jnp.maximum(jnp.max(sm, axis=0, keepdims=True),
                                            jnp.max(sb, axis=0, keepdims=True)), sink)
                pt = jnp.concatenate([no_pad_keys, jnp.exp2(sm - m).astype(BF16),
                                      jnp.exp2(sb - m).astype(BF16)], axis=0)
                ot = _dot(vt, pt)
                den = ot[C_DH:C_DH + 1, :] + jnp.exp2(sink - m)
                normed.append(ot[:C_DH, :] * (1.0 / den))
            for j in range(group // 2):
                slab = jnp.concatenate([normed[2 * j], normed[2 * j + 1]], axis=0).T
                sl = slice((kv * group // 2 + j) * LANES, (kv * group // 2 + j + 1) * LANES)
                o_ref[0, rows, sl] = jnp.where(rows_ok[i], slab, 0.0).astype(BF16)


def _swa(q, k, vt, sink):
    b, lp, qw = q.shape
    nb = lp // BLOCK
    kw = k.shape[-1]
    vw = vt.shape[1]
    nq = _pick(nb, SWA_BLOCKS)
    meta = nb - 1

    def slot(t):
        def index(n):
            raw = n * nq - 1 + t
            return jnp.where((raw < 0) | (raw >= nb), 0, raw)
        return index

    slots = [slot(t) for t in range(nq + 2)]
    qspec = pl.BlockSpec((1, nq * BLOCK, qw), lambda i, n: (i, n, 0))
    kspecs = [pl.BlockSpec((1, N_META, kw), lambda i, n: (i, lp // N_META - 1, 0))]
    kspecs += [pl.BlockSpec((1, BLOCK, kw), lambda i, n, f=f: (i, f(n), 0)) for f in slots]
    vspecs = [pl.BlockSpec((1, vw, BLOCK), lambda i, n: (i, 0, meta))]
    vspecs += [pl.BlockSpec((1, vw, BLOCK), lambda i, n, f=f: (i, 0, f(n))) for f in slots]
    return pl.pallas_call(
        functools.partial(_swa_body, nb=nb, nq=nq),
        grid=(b, nb // nq),
        in_specs=[pl.BlockSpec(memory_space=pltpu.SMEM), qspec] + kspecs + vspecs,
        out_specs=qspec,
        out_shape=jax.ShapeDtypeStruct((b, lp, qw), BF16),
        compiler_params=_params("parallel", "parallel"),
        name="swa",
    )(sink, q, *([k] * (nq + 3)), *([vt] * (nq + 3)))


def _rope_table(lp, theta, rot, offset):
    half = rot // 2
    lane = np.arange(LANES) - offset
    in_lo = (lane >= 0) & (lane < half)
    in_hi = (lane >= half) & (lane < rot)
    idx = np.where(in_lo | in_hi, lane % half, 0)
    inv = theta ** (-jnp.asarray(idx, F32) * 2.0 / rot)
    ang = _positions(lp)[:, None] * inv[None, :]
    sin = jnp.sin(ang)
    return jnp.stack([jnp.where(in_lo | in_hi, jnp.cos(ang), 1.0),
                      jnp.where(in_lo, -sin, jnp.where(in_hi, sin, 0.0))])


def _split_rope_table(lp, theta, rot):
    lanes = np.arange(LANES)
    r = lanes % (HALF // 2)
    active = jnp.asarray(r < rot // 2)
    inv = theta ** (-jnp.asarray(np.where(r < rot // 2, r, 0), F32) * 2.0 / rot)
    ang = _positions(lp)[:, None] * inv[None, :]
    sign = jnp.where(jnp.asarray(lanes < HALF), -1.0, 1.0)
    return jnp.stack([jnp.where(active, jnp.cos(ang), 1.0), jnp.where(active, sign * jnp.sin(ang), 0.0)])


def _pair_lanes(x):
    h, n = x.shape
    return jnp.repeat(x.reshape(h // 2, 2, n).transpose(0, 2, 1), HALF, axis=2)


def _prep_ffn(gain, wg, wu, wd):
    d, f = wg.shape
    return (gain.reshape(1, d), wg.astype(BF16), wu.astype(BF16),
            wd.reshape(f // FFN_CHUNK, FFN_CHUNK, d).astype(BF16))


def _prep_even(lp, mix_g, w_in, dec_f, dec_b, ret_norm, q_norm, w_qb, kv_norm, w_kvb, gq, gk, w_out):
    d = w_in.shape[0]
    hw = A_HEADS * A_DK
    cut = 4 * hw + B_Q_LORA + B_KV_LORA
    lanes = np.arange(LANES)
    partner = lanes.copy()
    partner[B_NOPE:B_NOPE + B_ROPE // 2] += B_ROPE // 2
    partner[B_NOPE + B_ROPE // 2:B_QK] -= B_ROPE // 2
    is_rope = jnp.asarray((lanes >= B_NOPE) & (lanes < B_QK))
    rot = lambda w: jnp.where(is_rope, w[..., partner], 0.0)
    kr_cols = jnp.concatenate([jnp.zeros((d, B_NOPE), F32), w_in[:, cut:],
                               jnp.zeros((d, LANES - B_QK), F32)], axis=1)
    qk_perm = lambda w: w.reshape(d, A_HEADS // 2, 2, 2, A_DK // 2).transpose(0, 1, 3, 2, 4).reshape(d, hw)
    w_in2 = jnp.concatenate([w_in[:, 4 * hw:cut], kr_cols, rot(kr_cols),
                             qk_perm(w_in[:, :hw]), qk_perm(w_in[:, hw:2 * hw]), w_in[:, 2 * hw:4 * hw]],
                            axis=1).astype(BF16)
    w_qb2 = jnp.pad(w_qb.reshape(B_Q_LORA, B_HEADS, B_QK), ((0, 0), (0, 0), (0, LANES - B_QK)))
    w_kv3 = w_kvb.reshape(B_KV_LORA, B_HEADS, B_NOPE + B_DV)
    w_kb = jnp.pad(w_kv3[:, :, :B_NOPE], ((0, 0), (0, 0), (0, LANES - B_NOPE)))
    w_vb = jnp.pad(w_kv3[:, :, B_NOPE:], ((0, 0), (0, 0), (0, LANES - B_DV)))
    pad96 = lambda g: jnp.pad(g, (0, LANES - B_QK)).reshape(1, LANES)
    with_rot = lambda g: jnp.concatenate([g, rot(g)], axis=0)
    ret_tab = _split_rope_table(lp, A_THETA, A_DK)
    qk_lanes = lambda x: _pair_lanes(x).reshape(x.shape[0] // 2, x.shape[1], 2, 2, A_DK // 2
                                                ).transpose(0, 1, 3, 2, 4).reshape(-1, x.shape[1], LANES)

    lgf = -jnp.exp(dec_f.astype(F32))
    lgb = -jnp.exp(dec_b.astype(F32))
    idx = jnp.arange(BLOCK, dtype=F32)
    diff = idx[:, None] - idx[None, :]
    dmat = (jnp.where(diff >= 0, jnp.exp(lgf[:, None, None] * jnp.maximum(diff, 0.0)), 0.0)
            + jnp.where(diff < 0, jnp.exp(lgb[:, None, None] * jnp.maximum(-diff, 0.0)), 0.0))
    ones = jnp.ones((BLOCK,), F32)
    vec = jnp.stack([
        qk_lanes(jnp.exp(lgf[:, None] * (BLOCK - 1 - idx)[None, :])),
        qk_lanes(jnp.exp(lgb[:, None] * idx[None, :])),
        qk_lanes(jnp.exp(lgf[:, None] * (idx + 1.0)[None, :])),
        qk_lanes(jnp.exp(lgb[:, None] * (BLOCK - idx)[None, :])),
        _pair_lanes(jnp.exp(BLOCK * lgf)[:, None] * ones[None, :]),
        _pair_lanes(jnp.exp(BLOCK * lgb)[:, None] * ones[None, :]),
    ], axis=1)
    half = A_HEADS * A_DK
    return dict(
        mix_g=mix_g.reshape(1, d), w_in=w_in2,
        q_norm=q_norm.reshape(1, -1), w_qb=w_qb2.reshape(B_Q_LORA, -1).astype(BF16),
        w_qr=rot(w_qb2).reshape(B_Q_LORA, -1).astype(BF16),
        kv_norm=kv_norm.reshape(1, -1), w_kb=w_kb.reshape(B_KV_LORA, -1).astype(BF16),
        w_vb=w_vb.reshape(B_KV_LORA, -1).astype(BF16),
        gq=with_rot(pad96(gq) * (B_QK ** -0.5 * LOG2E)), gk=with_rot(pad96(gk)),
        ret_tab=ret_tab,
        mla_tab=_rope_table(lp, B_THETA, B_ROPE, B_NOPE),
        ret_dmat=dmat, ret_vec=vec,
        ret_gain=ret_norm.reshape(A_HEADS // 2, 1, LANES),
        w_out=(w_out[:half].astype(BF16), w_out[half:].astype(BF16)),
    )


def _prep_odd(lp, mix_g, w_in, gq, gk, sink, w_out):
    d = w_in.shape[0]
    nq = C_HEADS * C_DH
    nkv = C_KV_HEADS * C_DH
    qtr, hr = C_DH // 2, C_ROT // 2
    perm = np.concatenate([np.arange(hr), np.arange(C_ROT, C_ROT + qtr - hr),
                           np.arange(hr, C_ROT), np.arange(C_ROT + qtr - hr, C_DH)])
    split = lambda w, heads: w.reshape(-1, heads, C_DH)[:, :, perm].reshape(-1, heads, 2, qtr)
    q_cols = lambda w: split(w, C_HEADS).reshape(-1, C_HEADS // 2, 2, 2, qtr).transpose(0, 1, 3, 2, 4
                                                                                         ).reshape(-1, nq)
    k_cols = lambda w: jnp.repeat(split(w, C_KV_HEADS)[:, :, :, None, :], 2, axis=3).reshape(-1, 2 * nkv)
    w_in2 = jnp.concatenate([q_cols(w_in[:, :nq]), k_cols(w_in[:, nq:nq + nkv]), w_in[:, nq + nkv:]], axis=1)
    slab_gain = lambda g: jnp.repeat(g[perm].reshape(2, 1, qtr), 2, axis=1).reshape(1, LANES)
    return dict(
        mix_g=mix_g.reshape(1, d), w_in=w_in2.astype(BF16),
        gq=slab_gain(gq) * (C_DH ** -0.5 * LOG2E), gk=slab_gain(gk),
        swa_tab=_split_rope_table(lp, C_THETA, C_ROT),
        sink=sink.astype(F32), w_out=(w_out.astype(BF16),),
    )


def _trunk(x, meta, layers):
    b, seq, d = x.shape
    lp = seq + BLOCK
    tail = jnp.concatenate([jnp.zeros((FRONT_PAD, d), x.dtype), meta.astype(x.dtype)], axis=0)
    tail = _ffn(tail, layers[0]["ffn0"])
    h = _ffn_head(x, tail, layers[0]["ffn0"]).reshape(b * lp, d)
    for i, layer in enumerate(layers):
        h3 = h.reshape(b, lp, d)
        mp = layer["mix"]
        if layer["even"]:
            rq, rk, rv, rg, mq, mk, mv = _even_in(h3, mp)
            xs = [_retention(rq, rk, rv, rg, mp), _mla(mq, mk, mv)]
        else:
            q, k, v = _odd_in(h3, mp)
            xs = [_swa(q, k, v, mp["sink"])]
        if i + 1 == len(layers):
            return _proj_ffn_tail(h3, xs, mp["w_out"], layer["ffn1"])
        xs = [a.reshape(b * lp, a.shape[-1]) for a in xs]
        h = _proj_ffn(h, xs, mp["w_out"], layer["ffn1"])
        h = _ffn(h, layers[i + 1]["ffn0"])


def kernel(x_prompt, x_sample, meta_tokens, ffn_norm, ffn_w_gate, ffn_w_up, ffn_w_down, mix_norm, even_w_in, ret_decay_f, ret_decay_b, ret_out_norm, mla_q_norm, mla_w_qb, mla_kv_norm, mla_w_kvb, mla_qk_norm_q, mla_qk_norm_k, even_w_out, odd_w_in, swa_q_norm, swa_k_norm, swa_sink, odd_w_out):
    depth = ffn_norm.shape[0]
    assert x_prompt.shape[1] == x_sample.shape[1]
    lp = x_prompt.shape[1] + BLOCK
    layers = []
    for layer in range(depth):
        i = layer // 2
        ffn = [_prep_ffn(ffn_norm[layer, s], ffn_w_gate[layer, s], ffn_w_up[layer, s], ffn_w_down[layer, s])
               for s in range(2)]
        if layer % 2 == 0:
            mix = _prep_even(lp, mix_norm[layer], even_w_in[i], ret_decay_f[i], ret_decay_b[i],
                             ret_out_norm[i], mla_q_norm[i], mla_w_qb[i], mla_kv_norm[i], mla_w_kvb[i],
                             mla_qk_norm_q[i], mla_qk_norm_k[i], even_w_out[i])
        else:
            mix = _prep_odd(lp, mix_norm[layer], odd_w_in[i], swa_q_norm[i], swa_k_norm[i],
                            swa_sink[i], odd_w_out[i])
        layers.append(dict(even=layer % 2 == 0, ffn0=ffn[0], ffn1=ffn[1], mix=mix))
    return (_trunk(x_prompt, meta_tokens, layers), _trunk(x_sample, meta_tokens, layers))
```

```python
import functools

import jax
import jax.numpy as jnp
import numpy as np
from jax import lax
from jax.experimental import pallas as pl
from jax.experimental.pallas import tpu as pltpu

F32 = jnp.float32
BF16 = jnp.bfloat16

D_MODEL = 1024
N_META = 16
BLOCK = 128
FRONT_PAD = BLOCK - N_META
EPS = 1e-6
NEG = -1e30
LOG2E = 1.4426950408889634
LANES = 128
HALF = LANES // 2

A_HEADS = 8
A_DK = 64
A_THETA = 10000.0
B_HEADS = 8
B_Q_LORA = 256
B_KV_LORA = 128
B_NOPE = 64
B_ROPE = 32
B_DV = 64
B_QK = B_NOPE + B_ROPE
B_THETA = 10000.0
C_HEADS = 16
C_KV_HEADS = 4
C_DH = 64
C_ROT = 16
C_THETA = 500000.0
C_WINDOW = 128

VMEM_LIMIT = 56 * 1024 * 1024
MXU_TILE = 256

FFN_TILES = (1024, 512, 384, 256, 128)
FFN_CHUNK = MXU_TILE
SEQ_TILES = (384, 128)
SEQS_PER_STEP = (2, 1)
RET_GROUPS = (33, 11, 3, 1)
SWA_BLOCKS = (3, 1)
MLA_ROWS = 16


def _params(*sem):
    return pltpu.CompilerParams(dimension_semantics=sem, vmem_limit_bytes=VMEM_LIMIT)


def _resident(shape):
    zeros = (0,) * len(shape)
    return pl.BlockSpec(shape, lambda *_: zeros, pipeline_mode=pl.Buffered(1))


def _rms_rows(x, gain):
    return x * lax.rsqrt(jnp.mean(x * x, axis=-1, keepdims=True) + EPS) * gain


def _dot(a, b):
    return jnp.dot(a, b, preferred_element_type=F32)


def _dot_nt(a, b):
    return lax.dot_general(a, b, (((1,), (1,)), ((), ())), preferred_element_type=F32)


def _is_pad_row(row, seq):
    return (row >= seq) & (row < seq + FRONT_PAD)


def _positions(lp):
    row = jnp.arange(lp, dtype=F32)
    seq = lp - BLOCK
    return jnp.where(row < seq, row + N_META, row - (seq + FRONT_PAD))


def _pick(n, options):
    for o in options:
        if n % o == 0:
            return o
    raise ValueError(f"no tile in {options} divides {n}")


def _ffn_core(x, g_ref, wg_ref, wu_ref, wd_ref):
    xn = _rms_rows(x, g_ref[...]).astype(BF16)
    acc = jnp.zeros_like(x)
    fc = wd_ref.shape[1]
    for c in range(wd_ref.shape[0]):
        gate = _dot(xn, wg_ref[:, c * fc:(c + 1) * fc])
        up = _dot(xn, wu_ref[:, c * fc:(c + 1) * fc])
        act = (gate * (1.0 / (1.0 + jnp.exp(-gate))) * up).astype(BF16)
        acc = acc + _dot(act, wd_ref[c])
    return x + 0.5 * acc


def _ffn_body(h_ref, g_ref, wg_ref, wu_ref, wd_ref, o_ref):
    o_ref[...] = _ffn_core(h_ref[...], g_ref, wg_ref, wu_ref, wd_ref)


def _ffn_head_body(x_ref, tail_ref, g_ref, wg_ref, wu_ref, wd_ref, o_ref, *, nt):
    t = pl.program_id(1)

    @pl.when(t < nt)
    def _():
        o_ref[0] = _ffn_core(x_ref[0], g_ref, wg_ref, wu_ref, wd_ref)

    @pl.when(t == nt)
    def _():
        o_ref[0, :BLOCK, :] = tail_ref[...]


def _proj_ffn_body(*refs, n_in):
    tile = lambda r: r[...] if len(r.shape) == 2 else r[0]
    h_ref = refs[0]
    xs = refs[1:1 + n_in]
    ws = refs[1 + n_in:1 + 2 * n_in]
    g_ref, wg_ref, wu_ref, wd_ref, o_ref = refs[1 + 2 * n_in:]
    h2 = tile(h_ref)
    for x_ref, w_ref in zip(xs, ws):
        h2 = h2 + _dot(tile(x_ref), w_ref[...])
    out = _ffn_core(h2, g_ref, wg_ref, wu_ref, wd_ref)
    if len(o_ref.shape) == 2:
        o_ref[...] = out
    else:
        o_ref[0] = out


def _ffn_specs(ffn):
    return [_resident(a.shape) for a in ffn]


def _ffn(h2d, ffn):
    t, d = h2d.shape
    tm = _pick(t, FFN_TILES)
    row = pl.BlockSpec((tm, d), lambda i: (i, 0))
    return pl.pallas_call(
        _ffn_body,
        grid=(t // tm,),
        in_specs=[row] + _ffn_specs(ffn),
        out_specs=row,
        out_shape=jax.ShapeDtypeStruct((t, d), F32),
        compiler_params=_params("parallel"),
        name="ffn",
    )(h2d, *ffn)


def _ffn_head(x, tail, ffn):
    b, seq, d = x.shape
    tm = _pick(seq, FFN_TILES)
    nt = seq // tm
    return pl.pallas_call(
        functools.partial(_ffn_head_body, nt=nt),
        grid=(b, nt + 1),
        in_specs=[pl.BlockSpec((1, tm, d), lambda i, t: (i, jnp.minimum(t, nt - 1), 0)),
                  pl.BlockSpec((BLOCK, d), lambda i, t: (0, 0))] + _ffn_specs(ffn),
        out_specs=pl.BlockSpec((1, tm, d), lambda i, t: (i, t, 0)),
        out_shape=jax.ShapeDtypeStruct((b, seq + BLOCK, d), F32),
        compiler_params=_params("parallel", "arbitrary"),
        name="ffn_head",
    )(x, tail, *ffn)


def _proj_ffn_tail(h, xs, ws, ffn):
    b, lp, d = h.shape
    seq = lp - BLOCK
    tm = _pick(seq, FFN_TILES)
    tok = lambda w: pl.BlockSpec((1, tm, w), lambda i, t: (i, t, 0))
    return pl.pallas_call(
        functools.partial(_proj_ffn_body, n_in=len(xs)),
        grid=(b, seq // tm),
        in_specs=[tok(d)] + [tok(x.shape[-1]) for x in xs] + [_resident(w.shape) for w in ws]
        + _ffn_specs(ffn),
        out_specs=tok(d),
        out_shape=jax.ShapeDtypeStruct((b, seq, d), F32),
        compiler_params=_params("parallel", "parallel"),
        name="proj_ffn_tail",
    )(h, *xs, *ws, *ffn)


def _proj_ffn(h2d, xs, ws, ffn):
    t, d = h2d.shape
    tm = _pick(t, FFN_TILES)
    row = pl.BlockSpec((tm, d), lambda i: (i, 0))
    x_specs = [pl.BlockSpec((tm, x.shape[1]), lambda i: (i, 0)) for x in xs]
    w_specs = [_resident(w.shape) for w in ws]
    return pl.pallas_call(
        functools.partial(_proj_ffn_body, n_in=len(xs)),
        grid=(t // tm,),
        in_specs=[row] + x_specs + w_specs + _ffn_specs(ffn),
        out_specs=row,
        out_shape=jax.ShapeDtypeStruct((t, d), F32),
        compiler_params=_params("parallel"),
        name="proj_ffn",
    )(h2d, *xs, *ws, *ffn)


def _even_in_body(h_ref, g_ref, win_ref, qn_ref, wqb_ref, wqr_ref, kvn_ref, wkb_ref, wvb_ref,
                  gq_ref, gk_ref, rtab_ref, mtab_ref,
                  rq_ref, rk_ref, rv_ref, rg_ref, mq_ref, mk_ref, mv_ref, *, tm, seq):
    ns = h_ref.shape[0]
    o = B_Q_LORA + B_KV_LORA
    for s in range(ns):
        z = _dot(_rms_rows(h_ref[s], g_ref[...]).astype(BF16), win_ref[...])
        cq = _rms_rows(z[:, :B_Q_LORA], qn_ref[...]).astype(BF16)
        ckv = _rms_rows(z[:, B_Q_LORA:o], kvn_ref[...]).astype(BF16)
        q_all = _dot(cq, wqb_ref[...])
        q_rot = _dot(cq, wqr_ref[...])
        k_all = _dot(ckv, wkb_ref[...])
        v_all = _dot(ckv, wvb_ref[...])
        _even_in_tail(s, z, q_all, q_rot, k_all, v_all, gq_ref, gk_ref, rtab_ref, mtab_ref,
                      rq_ref, rk_ref, rv_ref, rg_ref, mq_ref, mk_ref, mv_ref, tm=tm, seq=seq)


def _even_in_tail(s, z, q_all, q_rot, k_all, v_all, gq_ref, gk_ref, rtab_ref, mtab_ref,
                  rq_ref, rk_ref, rv_ref, rg_ref, mq_ref, mk_ref, mv_ref, *, tm, seq):
    o = B_Q_LORA + B_KV_LORA
    kr = z[:, o:o + LANES]
    kr_rot = z[:, o + LANES:o + 2 * LANES]
    mcos, msin = mtab_ref[0], mtab_ref[1]
    lane = lax.broadcasted_iota(jnp.int32, (tm, LANES), 1)
    v_ones = jnp.where(lane >= HALF, 1.0, 0.0)
    row = lax.broadcasted_iota(jnp.int32, (tm, LANES), 0) + pl.program_id(1) * tm
    bias_lane = lane == B_QK
    q_bias = jnp.where(bias_lane, 1.0, 0.0)
    k_bias = jnp.where(bias_lane & _is_pad_row(row, seq), NEG, 0.0)
    inv = 1.0 / B_QK
    qcos, qsin = mcos * gq_ref[0:1], msin * gq_ref[1:2]
    kcos, ksin = mcos * gk_ref[0:1], msin * gk_ref[1:2]
    kr_sin = kr_rot * ksin
    heads = [slice(hd * LANES, (hd + 1) * LANES) for hd in range(B_HEADS)]
    qhs = [q_all[:, sl] for sl in heads]
    khs = [k_all[:, sl] + kr for sl in heads]
    rqs = [lax.rsqrt(jnp.sum(qh * qh, axis=-1, keepdims=True) * inv + EPS) for qh in qhs]
    rks = [lax.rsqrt(jnp.sum(kh * kh, axis=-1, keepdims=True) * inv + EPS) for kh in khs]
    for sl, qh, kh, rq, rk in zip(heads, qhs, khs, rqs, rks):
        mq_ref[s, :, sl] = ((qh * qcos + q_rot[:, sl] * qsin) * rq + q_bias).astype(BF16)
        mk_ref[s, :, sl] = ((kh * kcos + kr_sin) * rk + k_bias).astype(BF16)
        mv_ref[s, :, sl] = (v_all[:, sl] + v_ones).astype(BF16)

    hw = A_HEADS * A_DK
    o += 2 * LANES
    rcos, rsin = rtab_ref[0], rtab_ref[1]
    for c in range(hw // LANES):
        sl = slice(c * LANES, (c + 1) * LANES)
        q = z[:, o + c * LANES:o + (c + 1) * LANES]
        k = z[:, o + hw + c * LANES:o + hw + (c + 1) * LANES]
        rq_ref[s, :, sl] = (q * rcos + pltpu.roll(q, HALF, 1) * rsin).astype(BF16)
        rk_ref[s, :, sl] = ((k * rcos + pltpu.roll(k, HALF, 1) * rsin) * (A_DK ** -0.5)).astype(BF16)
    rv_ref[s] = z[:, o + 2 * hw:o + 3 * hw].astype(BF16)
    ga = z[:, o + 3 * hw:o + 4 * hw]
    rg_ref[s] = (ga * (1.0 / (1.0 + jnp.exp(-ga)))).astype(BF16)


def _even_in(h, p):
    b, lp, d = h.shape
    tm = _pick(lp, SEQ_TILES)
    ns = _pick(b, SEQS_PER_STEP)
    tok = lambda w: pl.BlockSpec((ns, tm, w), lambda i, j: (i, j, 0))
    tab = pl.BlockSpec((2, tm, LANES), lambda i, j: (0, j, 0))
    consts = [p["mix_g"], p["w_in"], p["q_norm"], p["w_qb"], p["w_qr"], p["kv_norm"], p["w_kb"], p["w_vb"],
              p["gq"], p["gk"]]
    hw = A_HEADS * A_DK
    widths = [hw, hw, hw, hw, B_HEADS * LANES, B_HEADS * LANES, B_HEADS * LANES]
    return pl.pallas_call(
        functools.partial(_even_in_body, tm=tm, seq=lp - BLOCK),
        grid=(b // ns, lp // tm),
        in_specs=[tok(d)] + [_resident(c.shape) for c in consts] + [tab, tab],
        out_specs=[tok(w) for w in widths],
        out_shape=[jax.ShapeDtypeStruct((b, lp, w), BF16) for w in widths],
        compiler_params=_params("parallel", "parallel"),
        name="even_in",
    )(h, *consts, p["ret_tab"], p["mla_tab"])


def _ret_body(q_ref, k_ref, v_ref, g_ref, dmat_ref, vec_ref, gain_ref, o_ref, acc_ref, kv_ref, st_ref,
              *, nb):
    lane = lax.broadcasted_iota(jnp.int32, (BLOCK, LANES), 1)
    first = lane < HALF
    wkf, wkb, qf, qb = vec_ref[0, 0], vec_ref[0, 1], vec_ref[0, 2], vec_ref[0, 3]
    cf, cb = vec_ref[0, 4][:1], vec_ref[0, 5][:1]
    dcat = jnp.concatenate([dmat_ref[0], dmat_ref[1]], axis=0)
    qk_first = lane % HALF < HALF // 2
    keep0 = jnp.where(qk_first, 1.0, 0.0).astype(BF16)
    keep1 = jnp.where(qk_first, 0.0, 1.0).astype(BF16)
    row2 = lax.broadcasted_iota(jnp.int32, (2 * LANES, LANES), 0) % HALF
    col2 = lax.broadcasted_iota(jnp.int32, (2 * LANES, LANES), 1)
    same2 = (row2 < HALF // 2) == (col2 < HALF)
    unroll = _pick(nb, RET_GROUPS)

    def intra(grp, carry):
        ns = [grp * unroll + u for u in range(unroll)]
        rows = [pl.ds(pl.multiple_of(n * BLOCK, BLOCK), BLOCK) for n in ns]
        qs = [q_ref[0, r, :] for r in rows]
        ks = [k_ref[0, r, :] for r in rows]
        vs = [v_ref[0, r, :] for r in rows]
        ss = [_dot_nt(jnp.concatenate([q * keep0, q * keep1], axis=0), k) for q, k in zip(qs, ks)]
        kws = [jnp.concatenate([k.astype(F32) * wkf, k.astype(F32) * wkb], axis=1).T.astype(BF16)
               for k in ks]
        kvs = [_dot(kw, v) for kw, v in zip(kws, vs)]
        ss = [(s * dcat).astype(BF16) for s in ss]
        os = [_dot(s, v) for s, v in zip(ss, vs)]
        for n, r, kv, o2 in zip(ns, rows, kvs, os):
            kv_ref[n] = jnp.where(same2, kv, 0.0)
            acc_ref[r, :] = jnp.where(first, o2[:BLOCK], o2[BLOCK:])
        return carry

    lax.fori_loop(0, nb // unroll, intra, 0)

    def scan(t, carry):
        sf, sb = carry
        i = lax.rem(t + nb - 1, nb)
        j = lax.rem(2 * nb - 2 - t, nb)
        st_ref[i, :LANES, :] = sf.astype(BF16)
        st_ref[j, LANES:, :] = sb.astype(BF16)
        return sf * cf + kv_ref[i, :LANES, :], sb * cb + kv_ref[j, LANES:, :]

    zero = jnp.zeros((LANES, LANES), F32)
    lax.fori_loop(0, nb, scan, (zero, zero))

    def inter(n, carry):
        rows = pl.ds(pl.multiple_of(n * BLOCK, BLOCK), BLOCK)
        qf32 = q_ref[0, rows, :].astype(F32)
        qq = jnp.concatenate([qf32 * qf, qf32 * qb], axis=1).astype(BF16)
        o = acc_ref[rows, :] + _dot(qq, st_ref[n])
        sq = o * o
        inv = 1.0 / HALF
        r0 = lax.rsqrt(jnp.sum(jnp.where(first, sq, 0.0), axis=-1, keepdims=True) * inv + EPS)
        r1 = lax.rsqrt(jnp.sum(jnp.where(first, 0.0, sq), axis=-1, keepdims=True) * inv + EPS)
        out = o * jnp.where(first, r0, r1) * gain_ref[0] * g_ref[0, rows, :].astype(F32)
        o_ref[0, rows, :] = out.astype(BF16)
        return carry

    lax.fori_loop(0, nb, inter, 0, unroll=unroll)


def _retention(rq, rk, rv, rg, p):
    b, lp, hw = rq.shape
    pairs = hw // LANES
    nb = lp // BLOCK
    seq = pl.BlockSpec((1, lp, LANES), lambda i, j: (i, 0, j))
    return pl.pallas_call(
        functools.partial(_ret_body, nb=nb),
        grid=(b, pairs),
        in_specs=[seq, seq, seq, seq,
                  pl.BlockSpec((2, BLOCK, BLOCK), lambda i, j: (j, 0, 0)),
                  pl.BlockSpec((1, 6, BLOCK, LANES), lambda i, j: (j, 0, 0, 0)),
                  pl.BlockSpec((1, 1, LANES), lambda i, j: (j, 0, 0))],
        out_specs=seq,
        out_shape=jax.ShapeDtypeStruct((b, lp, hw), BF16),
        scratch_shapes=[pltpu.VMEM((lp, LANES), F32), pltpu.VMEM((nb, 2 * LANES, LANES), F32),
                        pltpu.VMEM((nb, 2 * LANES, LANES), BF16)],
        compiler_params=_params("parallel", "parallel"),
        name="retention",
    )(rq, rk, rv, rg, p["ret_dmat"], p["ret_vec"], p["ret_gain"])


def _mla_body(q_ref, k_ref, v_ref, o_ref, s_scr, p_scr, o_scr, *, tq):
    lp = k_ref.shape[1]
    chunks = [(c, min(MXU_TILE, lp - c)) for c in range(0, lp, MXU_TILE)]
    hk = lp // 2 // MXU_TILE * MXU_TILE

    def scores(hd):
        q = q_ref[0, :, hd * LANES:(hd + 1) * LANES]
        for c, w in chunks:
            s_scr[hd, :, c:c + w] = _dot_nt(q, k_ref[0, c:c + w, hd * LANES:(hd + 1) * LANES])

    def probs(hd):
        for r in range(0, tq, MLA_ROWS):
            sb = s_scr[hd, r:r + MLA_ROWS, :]
            m = jnp.max(sb, axis=-1, keepdims=True)
            p_scr[hd, r:r + MLA_ROWS, :] = jnp.exp2(sb - m).astype(BF16)

    def weighted(hd):
        v = v_ref[0, :, hd * LANES:(hd + 1) * LANES]
        if hk == 0:
            return _dot(p_scr[hd], v)
        return _dot(p_scr[hd, :, :hk], v[:hk]) + _dot(p_scr[hd, :, hk:], v[hk:])

    scores(0)
    scores(1)
    probs(0)
    o_scr[...] = weighted(0)
    probs(1)
    oa = o_scr[...]
    ob = weighted(1)
    lane = lax.broadcasted_iota(jnp.int32, (tq, LANES), 1)
    row = lax.broadcasted_iota(jnp.int32, (tq, LANES), 0) + pl.program_id(2) * tq
    out = jnp.where(lane < HALF, oa * pltpu.roll(1.0 / oa, HALF, 1),
                    pltpu.roll(ob, HALF, 1) * (1.0 / ob))
    o_ref[0] = jnp.where(_is_pad_row(row, lp - BLOCK), 0.0, out).astype(BF16)


def _mla(mq, mk, mv):
    b, lp, _ = mq.shape
    pairs = B_HEADS // 2
    tq = _pick(lp, SEQ_TILES)
    return pl.pallas_call(
        functools.partial(_mla_body, tq=tq),
        grid=(b, pairs, lp // tq),
        in_specs=[pl.BlockSpec((1, tq, 2 * LANES), lambda i, j, t: (i, t, j)),
                  pl.BlockSpec((1, lp, 2 * LANES), lambda i, j, t: (i, 0, j)),
                  pl.BlockSpec((1, lp, 2 * LANES), lambda i, j, t: (i, 0, j))],
        out_specs=pl.BlockSpec((1, tq, LANES), lambda i, j, t: (i, t, j)),
        out_shape=jax.ShapeDtypeStruct((b, lp, B_HEADS * B_DV), BF16),
        scratch_shapes=[pltpu.VMEM((2, tq, lp), F32), pltpu.VMEM((2, tq, lp), BF16),
                        pltpu.VMEM((tq, LANES), F32)],
        compiler_params=_params("parallel", "parallel", "parallel"),
        name="mla",
    )(mq, mk, mv)


def _odd_in_body(h_ref, g_ref, win_ref, gq_ref, gk_ref, tab_ref, q_ref, k_ref, vt_ref, *, tm):
    cos, sin = tab_ref[0], tab_ref[1]
    is_a = lax.broadcasted_iota(jnp.int32, (tm, LANES), 1) % HALF < HALF // 2
    nq = C_HEADS * C_DH
    nk = 2 * C_KV_HEADS * C_DH
    inv = 1.0 / C_DH
    rope = lambda y: y * cos + pltpu.roll(y, HALF, 1) * sin
    zs = [_dot(_rms_rows(h_ref[s], g_ref[...]).astype(BF16), win_ref[...]) for s in range(h_ref.shape[0])]
    for s, z in enumerate(zs):
        xqs = [z[:, c * LANES:(c + 1) * LANES] for c in range(nq // LANES)]
        xks = [z[:, nq + c * LANES:nq + (c + 1) * LANES] for c in range(nk // LANES)]
        rqs = []
        for xq in xqs:
            sq = xq * xq
            ra = lax.rsqrt(jnp.sum(jnp.where(is_a, sq, 0.0), axis=-1, keepdims=True) * inv + EPS)
            rb = lax.rsqrt(jnp.sum(jnp.where(is_a, 0.0, sq), axis=-1, keepdims=True) * inv + EPS)
            rqs.append(jnp.where(is_a, ra, rb))
        rks = [lax.rsqrt(jnp.sum(xk * xk, axis=-1, keepdims=True) * (0.5 * inv) + EPS) for xk in xks]
        for c, (xq, r) in enumerate(zip(xqs, rqs)):
            q_ref[s, :, c * LANES:(c + 1) * LANES] = rope(xq * r * gq_ref[...]).astype(BF16)
        for c, (xk, r) in enumerate(zip(xks, rks)):
            k_ref[s, :, c * LANES:(c + 1) * LANES] = rope(xk * r * gk_ref[...]).astype(BF16)
        vt_ref[s] = z[:, nq + nk:].T.astype(BF16)


def _odd_in(h, p):
    b, lp, d = h.shape
    tm = _pick(lp, SEQ_TILES)
    ns = _pick(b, SEQS_PER_STEP)
    tok = lambda w: pl.BlockSpec((ns, tm, w), lambda i, j: (i, j, 0))
    tab = pl.BlockSpec((2, tm, LANES), lambda i, j: (0, j, 0))
    consts = [p["mix_g"], p["w_in"], p["gq"], p["gk"]]
    nq, nk, nv = C_HEADS * C_DH, 2 * C_KV_HEADS * C_DH, C_KV_HEADS * C_DH
    return pl.pallas_call(
        functools.partial(_odd_in_body, tm=tm),
        grid=(b // ns, lp // tm),
        in_specs=[tok(d)] + [_resident(c.shape) for c in consts] + [tab],
        out_specs=[tok(nq), tok(nk), pl.BlockSpec((ns, nv, tm), lambda i, j: (i, 0, j))],
        out_shape=[jax.ShapeDtypeStruct((b, lp, nq), BF16), jax.ShapeDtypeStruct((b, lp, nk), BF16),
                   jax.ShapeDtypeStruct((b, nv, lp), BF16)],
        compiler_params=_params("parallel", "parallel"),
        name="odd_in",
    )(h, *consts, p["swa_tab"])


def _swa_body(sink_ref, q_ref, *refs, nb, nq):
    km_ref, k_refs = refs[0], refs[1:nq + 3]
    vm_ref, v_refs = refs[nq + 3], refs[nq + 4:2 * nq + 6]
    o_ref = refs[-1]
    nband = 3 * BLOCK
    group = C_HEADS // C_KV_HEADS
    kk = lax.broadcasted_iota(jnp.int32, (nband, BLOCK), 0)
    qq = lax.broadcasted_iota(jnp.int32, (nband, BLOCK), 1)
    lane = lax.broadcasted_iota(jnp.int32, (BLOCK, LANES), 1)
    rowi = lax.broadcasted_iota(jnp.int32, (BLOCK, LANES), 0)
    is_a = lane % HALF < HALF // 2
    keep = (jnp.where(is_a, 1.0, 0.0).astype(BF16), jnp.where(is_a, 0.0, 1.0).astype(BF16))
    ones = jnp.ones((C_DH, BLOCK + nband), BF16)
    no_pad_keys = jnp.zeros((FRONT_PAD, BLOCK), BF16)
    seq_len = nb * BLOCK - FRONT_PAD
    masks, rows_ok, meta_scores, band_scores = [], [], [], []
    for i in range(nq):
        stored = pl.program_id(1) * nq + i
        n = jnp.where(stored == nb - 1, 0, stored + 1)
        q_pos = n * BLOCK + qq - FRONT_PAD
        k_pos = (n - 1) * BLOCK + kk - FRONT_PAD
        masks.append((k_pos >= N_META) & (k_pos < seq_len) & (jnp.abs(q_pos - k_pos) <= C_WINDOW))
        rows_ok.append(rowi + n * BLOCK >= FRONT_PAD)
        rows = slice(i * BLOCK, (i + 1) * BLOCK)
        for kv in range(C_KV_HEADS):
            kvs = slice(kv * LANES, (kv + 1) * LANES)
            kcat = jnp.concatenate([r[0, :, kvs] for r in (km_ref, k_refs[i], k_refs[i + 1], k_refs[i + 2])],
                                   axis=0)
            pairs = [q_ref[0, rows, (kv * group // 2 + j) * LANES:(kv * group // 2 + j + 1) * LANES]
                     for j in range(group // 2)]
            qs = jnp.concatenate([p2 * keep[w] for p2 in pairs for w in range(2)], axis=0)
            st = _dot_nt(kcat, qs)
            meta_scores.append(st[:N_META])
            band_scores.append(st[N_META:])
    for i in range(nq):
        rows = slice(i * BLOCK, (i + 1) * BLOCK)
        for kv in range(C_KV_HEADS):
            sm_all, sb_all = meta_scores[i * C_KV_HEADS + kv], band_scores[i * C_KV_HEADS + kv]
            vt = jnp.concatenate([r[0, kv * C_DH:(kv + 1) * C_DH, :]
                                  for r in (vm_ref, v_refs[i], v_refs[i + 1], v_refs[i + 2])], axis=1)
            vt = jnp.concatenate([vt, ones], axis=0)
            normed = []
            for g in range(group):
                sink = sink_ref[kv * group + g] * LOG2E
                sm = sm_all[:, g * BLOCK:(g + 1) * BLOCK]
                sb = jnp.where(masks[i], sb_all[:, g * BLOCK:(g + 1) * BLOCK], NEG)
                m = jnp.maximum(jnp.maximum(jnp.max(sm, axis=0, keepdims=True),
                                            jnp.max(sb, axis=0, keepdims=True)), sink)
                pt = jnp.concatenate([no_pad_keys, jnp.exp2(sm - m).astype(BF16),
                                      jnp.exp2(sb - m).astype(BF16)], axis=0)
                ot = _dot(vt, pt)
                den = ot[C_DH:C_DH + 1, :] + jnp.exp2(sink - m)
                normed.append(ot[:C_DH, :] * (1.0 / den))
            for j in range(group // 2):
                slab = jnp.concatenate([normed[2 * j], normed[2 * j + 1]], axis=0).T
                sl = slice((kv * group // 2 + j) * LANES, (kv * group // 2 + j + 1) * LANES)
                o_ref[0, rows, sl] = jnp.where(rows_ok[i], slab, 0.0).astype(BF16)


def _swa(q, k, vt, sink):
    b, lp, qw = q.shape
    nb = lp // BLOCK
    kw = k.shape[-1]
    vw = vt.shape[1]
    nq = _pick(nb, SWA_BLOCKS)
    meta = nb - 1

    def slot(t):
        def index(n):
            raw = n * nq - 1 + t
            return jnp.where((raw < 0) | (raw >= nb), 0, raw)
        return index

    slots = [slot(t) for t in range(nq + 2)]
    qspec = pl.BlockSpec((1, nq * BLOCK, qw), lambda i, n: (i, n, 0))
    kspecs = [pl.BlockSpec((1, N_META, kw), lambda i, n: (i, lp // N_META - 1, 0))]
    kspecs += [pl.BlockSpec((1, BLOCK, kw), lambda i, n, f=f: (i, f(n), 0)) for f in slots]
    vspecs = [pl.BlockSpec((1, vw, BLOCK), lambda i, n: (i, 0, meta))]
    vspecs += [pl.BlockSpec((1, vw, BLOCK), lambda i, n, f=f: (i, 0, f(n))) for f in slots]
    return pl.pallas_call(
        functools.partial(_swa_body, nb=nb, nq=nq),
        grid=(b, nb // nq),
        in_specs=[pl.BlockSpec(memory_space=pltpu.SMEM), qspec] + kspecs + vspecs,
        out_specs=qspec,
        out_shape=jax.ShapeDtypeStruct((b, lp, qw), BF16),
        compiler_params=_params("parallel", "parallel"),
        name="swa",
    )(sink, q, *([k] * (nq + 3)), *([vt] * (nq + 3)))


def _rope_table(lp, theta, rot, offset):
    half = rot // 2
    lane = np.arange(LANES) - offset
    in_lo = (lane >= 0) & (lane < half)
    in_hi = (lane >= half) & (lane < rot)
    idx = np.where(in_lo | in_hi, lane % half, 0)
    inv = theta ** (-jnp.asarray(idx, F32) * 2.0 / rot)
    ang = _positions(lp)[:, None] * inv[None, :]
    sin = jnp.sin(ang)
    return jnp.stack([jnp.where(in_lo | in_hi, jnp.cos(ang), 1.0),
                      jnp.where(in_lo, -sin, jnp.where(in_hi, sin, 0.0))])


def _split_rope_table(lp, theta, rot):
    lanes = np.arange(LANES)
    r = lanes % (HALF // 2)
    active = jnp.asarray(r < rot // 2)
    inv = theta ** (-jnp.asarray(np.where(r < rot // 2, r, 0), F32) * 2.0 / rot)
    ang = _positions(lp)[:, None] * inv[None, :]
    sign = jnp.where(jnp.asarray(lanes < HALF), -1.0, 1.0)
    return jnp.stack([jnp.where(active, jnp.cos(ang), 1.0), jnp.where(active, sign * jnp.sin(ang), 0.0)])


def _pair_lanes(x):
    h, n = x.shape
    return jnp.repeat(x.reshape(h // 2, 2, n).transpose(0, 2, 1), HALF, axis=2)


def _prep_ffn(gain, wg, wu, wd):
    d, f = wg.shape
    return (gain.reshape(1, d), wg.astype(BF16), wu.astype(BF16),
            wd.reshape(f // FFN_CHUNK, FFN_CHUNK, d).astype(BF16))


def _prep_even(lp, mix_g, w_in, dec_f, dec_b, ret_norm, q_norm, w_qb, kv_norm, w_kvb, gq, gk, w_out):
    d = w_in.shape[0]
    hw = A_HEADS * A_DK
    cut = 4 * hw + B_Q_LORA + B_KV_LORA
    lanes = np.arange(LANES)
    partner = lanes.copy()
    partner[B_NOPE:B_NOPE + B_ROPE // 2] += B_ROPE // 2
    partner[B_NOPE + B_ROPE // 2:B_QK] -= B_ROPE // 2
    is_rope = jnp.asarray((lanes >= B_NOPE) & (lanes < B_QK))
    rot = lambda w: jnp.where(is_rope, w[..., partner], 0.0)
    kr_cols = jnp.concatenate([jnp.zeros((d, B_NOPE), F32), w_in[:, cut:],
                               jnp.zeros((d, LANES - B_QK), F32)], axis=1)
    qk_perm = lambda w: w.reshape(d, A_HEADS // 2, 2, 2, A_DK // 2).transpose(0, 1, 3, 2, 4).reshape(d, hw)
    w_in2 = jnp.concatenate([w_in[:, 4 * hw:cut], kr_cols, rot(kr_cols),
                             qk_perm(w_in[:, :hw]), qk_perm(w_in[:, hw:2 * hw]), w_in[:, 2 * hw:4 * hw]],
                            axis=1).astype(BF16)
    w_qb2 = jnp.pad(w_qb.reshape(B_Q_LORA, B_HEADS, B_QK), ((0, 0), (0, 0), (0, LANES - B_QK)))
    w_kv3 = w_kvb.reshape(B_KV_LORA, B_HEADS, B_NOPE + B_DV)
    w_kb = jnp.pad(w_kv3[:, :, :B_NOPE], ((0, 0), (0, 0), (0, LANES - B_NOPE)))
    w_vb = jnp.pad(w_kv3[:, :, B_NOPE:], ((0, 0), (0, 0), (0, LANES - B_DV)))
    pad96 = lambda g: jnp.pad(g, (0, LANES - B_QK)).reshape(1, LANES)
    with_rot = lambda g: jnp.concatenate([g, rot(g)], axis=0)
    ret_tab = _split_rope_table(lp, A_THETA, A_DK)
    qk_lanes = lambda x: _pair_lanes(x).reshape(x.shape[0] // 2, x.shape[1], 2, 2, A_DK // 2
                                                ).transpose(0, 1, 3, 2, 4).reshape(-1, x.shape[1], LANES)

    lgf = -jnp.exp(dec_f.astype(F32))
    lgb = -jnp.exp(dec_b.astype(F32))
    idx = jnp.arange(BLOCK, dtype=F32)
    diff = idx[:, None] - idx[None, :]
    dmat = (jnp.where(diff >= 0, jnp.exp(lgf[:, None, None] * jnp.maximum(diff, 0.0)), 0.0)
            + jnp.where(diff < 0, jnp.exp(lgb[:, None, None] * jnp.maximum(-diff, 0.0)), 0.0))
    ones = jnp.ones((BLOCK,), F32)
    vec = jnp.stack([
        qk_lanes(jnp.exp(lgf[:, None] * (BLOCK - 1 - idx)[None, :])),
        qk_lanes(jnp.exp(lgb[:, None] * idx[None, :])),
        qk_lanes(jnp.exp(lgf[:, None] * (idx + 1.0)[None, :])),
        qk_lanes(jnp.exp(lgb[:, None] * (BLOCK - idx)[None, :])),
        _pair_lanes(jnp.exp(BLOCK * lgf)[:, None] * ones[None, :]),
        _pair_lanes(jnp.exp(BLOCK * lgb)[:, None] * ones[None, :]),
    ], axis=1)
    half = A_HEADS * A_DK
    return dict(
        mix_g=mix_g.reshape(1, d), w_in=w_in2,
        q_norm=q_norm.reshape(1, -1), w_qb=w_qb2.reshape(B_Q_LORA, -1).astype(BF16),
        w_qr=rot(w_qb2).reshape(B_Q_LORA, -1).astype(BF16),
        kv_norm=kv_norm.reshape(1, -1), w_kb=w_kb.reshape(B_KV_LORA, -1).astype(BF16),
        w_vb=w_vb.reshape(B_KV_LORA, -1).astype(BF16),
        gq=with_rot(pad96(gq) * (B_QK ** -0.5 * LOG2E)), gk=with_rot(pad96(gk)),
        ret_tab=ret_tab,
        mla_tab=_rope_table(lp, B_THETA, B_ROPE, B_NOPE),
        ret_dmat=dmat, ret_vec=vec,
        ret_gain=ret_norm.reshape(A_HEADS // 2, 1, LANES),
        w_out=(w_out[:half].astype(BF16), w_out[half:].astype(BF16)),
    )


def _prep_odd(lp, mix_g, w_in, gq, gk, sink, w_out):
    d = w_in.shape[0]
    nq = C_HEADS * C_DH
    nkv = C_KV_HEADS * C_DH
    qtr, hr = C_DH // 2, C_ROT // 2
    perm = np.concatenate([np.arange(hr), np.arange(C_ROT, C_ROT + qtr - hr),
                           np.arange(hr, C_ROT), np.arange(C_ROT + qtr - hr, C_DH)])
    split = lambda w, heads: w.reshape(-1, heads, C_DH)[:, :, perm].reshape(-1, heads, 2, qtr)
    q_cols = lambda w: split(w, C_HEADS).reshape(-1, C_HEADS // 2, 2, 2, qtr).transpose(0, 1, 3, 2, 4
                                                                                         ).reshape(-1, nq)
    k_cols = lambda w: jnp.repeat(split(w, C_KV_HEADS)[:, :, :, None, :], 2, axis=3).reshape(-1, 2 * nkv)
    w_in2 = jnp.concatenate([q_cols(w_in[:, :nq]), k_cols(w_in[:, nq:nq + nkv]), w_in[:, nq + nkv:]], axis=1)
    slab_gain = lambda g: jnp.repeat(g[perm].reshape(2, 1, qtr), 2, axis=1).reshape(1, LANES)
    return dict(
        mix_g=mix_g.reshape(1, d), w_in=w_in2.astype(BF16),
        gq=slab_gain(gq) * (C_DH ** -0.5 * LOG2E), gk=slab_gain(gk),
        swa_tab=_split_rope_table(lp, C_THETA, C_ROT),
        sink=sink.astype(F32), w_out=(w_out.astype(BF16),),
    )


def _trunk(x, meta, layers):
    b, seq, d = x.shape
    lp = seq + BLOCK
    tail = jnp.concatenate([jnp.zeros((FRONT_PAD, d), x.dtype), meta.astype(x.dtype)], axis=0)
    tail = _ffn(tail, layers[0]["ffn0"])
    h = _ffn_head(x, tail, layers[0]["ffn0"]).reshape(b * lp, d)
    for i, layer in enumerate(layers):
        h3 = h.reshape(b, lp, d)
        mp = layer["mix"]
        if layer["even"]:
            rq, rk, rv, rg, mq, mk, mv = _even_in(h3, mp)
            xs = [_retention(rq, rk, rv, rg, mp), _mla(mq, mk, mv)]
        else:
            q, k, v = _odd_in(h3, mp)
            xs = [_swa(q, k, v, mp["sink"])]
        if i + 1 == len(layers):
            return _proj_ffn_tail(h3, xs, mp["w_out"], layer["ffn1"])
        xs = [a.reshape(b * lp, a.shape[-1]) for a in xs]
        h = _proj_ffn(h, xs, mp["w_out"], layer["ffn1"])
        h = _ffn(h, layers[i + 1]["ffn0"])


def kernel(x_prompt, x_sample, meta_tokens, ffn_norm, ffn_w_gate, ffn_w_up, ffn_w_down, mix_norm, even_w_in, ret_decay_f, ret_decay_b, ret_out_norm, mla_q_norm, mla_w_qb, mla_kv_norm, mla_w_kvb, mla_qk_norm_q, mla_qk_norm_k, even_w_out, odd_w_in, swa_q_norm, swa_k_norm, swa_sink, odd_w_out):
    depth = ffn_norm.shape[0]
    assert x_prompt.shape[1] == x_sample.shape[1]
    lp = x_prompt.shape[1] + BLOCK
    layers = []
    for layer in range(depth):
        i = layer // 2
        ffn = [_prep_ffn(ffn_norm[layer, s], ffn_w_gate[layer, s], ffn_w_up[layer, s], ffn_w_down[layer, s])
               for s in range(2)]
        if layer % 2 == 0:
            mix = _prep_even(lp, mix_norm[layer], even_w_in[i], ret_decay_f[i], ret_decay_b[i],
                             ret_out_norm[i], mla_q_norm[i], mla_w_qb[i], mla_kv_norm[i], mla_w_kvb[i],
                             mla_qk_norm_q[i], mla_qk_norm_k[i], even_w_out[i])
        else:
            mix = _prep_odd(lp, mix_norm[layer], odd_w_in[i], swa_q_norm[i], swa_k_norm[i],
                            swa_sink[i], odd_w_out[i])
        layers.append(dict(even=layer % 2 == 0, ffn0=ffn[0], ffn1=ffn[1], mix=mix))
    return (_trunk(x_prompt, meta_tokens, layers), _trunk(x_sample, meta_tokens, layers))
```

```python
import functools

import jax
import jax.numpy as jnp
import numpy as np
from jax import lax
from jax.experimental import pallas as pl
from jax.experimental.pallas import tpu as pltpu

F32 = jnp.float32
BF16 = jnp.bfloat16

D_MODEL = 1024
N_META = 16
BLOCK = 128
FRONT_PAD = BLOCK - N_META
EPS = 1e-6
NEG = -1e30
LOG2E = 1.4426950408889634
LANES = 128
HALF = LANES // 2

A_HEADS = 8
A_DK = 64
A_THETA = 10000.0
B_HEADS = 8
B_Q_LORA = 256
B_KV_LORA = 128
B_NOPE = 64
B_ROPE = 32
B_DV = 64
B_QK = B_NOPE + B_ROPE
B_THETA = 10000.0
C_HEADS = 16
C_KV_HEADS = 4
C_DH = 64
C_ROT = 16
C_THETA = 500000.0
C_WINDOW = 128

VMEM_LIMIT = 56 * 1024 * 1024
MXU_TILE = 256

FFN_TILES = (1024, 512, 384, 256, 128)
FFN_CHUNK = MXU_TILE
SEQ_TILES = (384, 128)
SEQS_PER_STEP = (2, 1)
RET_GROUPS = (33, 11, 3, 1)
SWA_BLOCKS = (3, 1)
MLA_ROWS = 16
MLA_HEADS = (8, 4, 2)


def _params(*sem):
    return pltpu.CompilerParams(dimension_semantics=sem, vmem_limit_bytes=VMEM_LIMIT)


def _resident(shape):
    zeros = (0,) * len(shape)
    return pl.BlockSpec(shape, lambda *_: zeros, pipeline_mode=pl.Buffered(1))


def _rms_rows(x, gain):
    return x * lax.rsqrt(jnp.mean(x * x, axis=-1, keepdims=True) + EPS) * gain


def _dot(a, b):
    return jnp.dot(a, b, preferred_element_type=F32)


def _dot_nt(a, b):
    return lax.dot_general(a, b, (((1,), (1,)), ((), ())), preferred_element_type=F32)


def _is_pad_row(row, seq):
    return (row >= seq) & (row < seq + FRONT_PAD)


def _positions(lp):
    row = jnp.arange(lp, dtype=F32)
    seq = lp - BLOCK
    return jnp.where(row < seq, row + N_META, row - (seq + FRONT_PAD))


def _pick(n, options):
    for o in options:
        if n % o == 0:
            return o
    raise ValueError(f"no tile in {options} divides {n}")


def _ffn_core(x, g_ref, wg_ref, wu_ref, wd_ref):
    xn = _rms_rows(x, g_ref[...]).astype(BF16)
    acc = jnp.zeros_like(x)
    fc = wd_ref.shape[1]
    for c in range(wd_ref.shape[0]):
        gate = _dot(xn, wg_ref[:, c * fc:(c + 1) * fc])
        up = _dot(xn, wu_ref[:, c * fc:(c + 1) * fc])
        act = (gate * (1.0 / (1.0 + jnp.exp(-gate))) * up).astype(BF16)
        acc = acc + _dot(act, wd_ref[c])
    return x + 0.5 * acc


def _ffn_body(h_ref, g_ref, wg_ref, wu_ref, wd_ref, o_ref):
    o_ref[...] = _ffn_core(h_ref[...], g_ref, wg_ref, wu_ref, wd_ref)


def _ffn_head_body(x_ref, tail_ref, g_ref, wg_ref, wu_ref, wd_ref, o_ref, *, nt):
    t = pl.program_id(1)

    @pl.when(t < nt)
    def _():
        o_ref[0] = _ffn_core(x_ref[0], g_ref, wg_ref, wu_ref, wd_ref)

    @pl.when(t == nt)
    def _():
        o_ref[0, :BLOCK, :] = tail_ref[...]


def _proj_ffn_body(*refs, n_in):
    tile = lambda r: r[...] if len(r.shape) == 2 else r[0]
    h_ref = refs[0]
    xs = refs[1:1 + n_in]
    ws = refs[1 + n_in:1 + 2 * n_in]
    g_ref, wg_ref, wu_ref, wd_ref, o_ref = refs[1 + 2 * n_in:]
    h2 = tile(h_ref)
    for x_ref, w_ref in zip(xs, ws):
        h2 = h2 + _dot(tile(x_ref), w_ref[...])
    out = _ffn_core(h2, g_ref, wg_ref, wu_ref, wd_ref)
    if len(o_ref.shape) == 2:
        o_ref[...] = out
    else:
        o_ref[0] = out


def _ffn_specs(ffn):
    return [_resident(a.shape) for a in ffn]


def _ffn(h2d, ffn):
    t, d = h2d.shape
    tm = _pick(t, FFN_TILES)
    row = pl.BlockSpec((tm, d), lambda i: (i, 0))
    return pl.pallas_call(
        _ffn_body,
        grid=(t // tm,),
        in_specs=[row] + _ffn_specs(ffn),
        out_specs=row,
        out_shape=jax.ShapeDtypeStruct((t, d), F32),
        compiler_params=_params("parallel"),
        name="ffn",
    )(h2d, *ffn)


def _ffn_head(x, tail, ffn):
    b, seq, d = x.shape
    tm = _pick(seq, FFN_TILES)
    nt = seq // tm
    return pl.pallas_call(
        functools.partial(_ffn_head_body, nt=nt),
        grid=(b, nt + 1),
        in_specs=[pl.BlockSpec((1, tm, d), lambda i, t: (i, jnp.minimum(t, nt - 1), 0)),
                  pl.BlockSpec((BLOCK, d), lambda i, t: (0, 0))] + _ffn_specs(ffn),
        out_specs=pl.BlockSpec((1, tm, d), lambda i, t: (i, t, 0)),
        out_shape=jax.ShapeDtypeStruct((b, seq + BLOCK, d), F32),
        compiler_params=_params("parallel", "arbitrary"),
        name="ffn_head",
    )(x, tail, *ffn)


def _proj_ffn_tail(h, xs, ws, ffn):
    b, lp, d = h.shape
    seq = lp - BLOCK
    tm = _pick(seq, FFN_TILES)
    tok = lambda w: pl.BlockSpec((1, tm, w), lambda i, t: (i, t, 0))
    return pl.pallas_call(
        functools.partial(_proj_ffn_body, n_in=len(xs)),
        grid=(b, seq // tm),
        in_specs=[tok(d)] + [tok(x.shape[-1]) for x in xs] + [_resident(w.shape) for w in ws]
        + _ffn_specs(ffn),
        out_specs=tok(d),
        out_shape=jax.ShapeDtypeStruct((b, seq, d), F32),
        compiler_params=_params("parallel", "parallel"),
        name="proj_ffn_tail",
    )(h, *xs, *ws, *ffn)


def _proj_ffn(h2d, xs, ws, ffn):
    t, d = h2d.shape
    tm = _pick(t, FFN_TILES)
    row = pl.BlockSpec((tm, d), lambda i: (i, 0))
    x_specs = [pl.BlockSpec((tm, x.shape[1]), lambda i: (i, 0)) for x in xs]
    w_specs = [_resident(w.shape) for w in ws]
    return pl.pallas_call(
        functools.partial(_proj_ffn_body, n_in=len(xs)),
        grid=(t // tm,),
        in_specs=[row] + x_specs + w_specs + _ffn_specs(ffn),
        out_specs=row,
        out_shape=jax.ShapeDtypeStruct((t, d), F32),
        compiler_params=_params("parallel"),
        name="proj_ffn",
    )(h2d, *xs, *ws, *ffn)


def _even_in_body(h_ref, g_ref, win_ref, qn_ref, wqb_ref, wqr_ref, kvn_ref, wkb_ref, wvb_ref,
                  gq_ref, gk_ref, rtab_ref, mtab_ref,
                  rq_ref, rk_ref, rv_ref, rg_ref, mq_ref, mk_ref, mv_ref, *, tm, seq):
    ns = h_ref.shape[0]
    o = B_Q_LORA + B_KV_LORA
    for s in range(ns):
        z = _dot(_rms_rows(h_ref[s], g_ref[...]).astype(BF16), win_ref[...])
        cq = _rms_rows(z[:, :B_Q_LORA], qn_ref[...]).astype(BF16)
        ckv = _rms_rows(z[:, B_Q_LORA:o], kvn_ref[...]).astype(BF16)
        q_all = _dot(cq, wqb_ref[...])
        q_rot = _dot(cq, wqr_ref[...])
        k_all = _dot(ckv, wkb_ref[...])
        v_all = _dot(ckv, wvb_ref[...])
        _even_in_tail(s, z, q_all, q_rot, k_all, v_all, gq_ref, gk_ref, rtab_ref, mtab_ref,
                      rq_ref, rk_ref, rv_ref, rg_ref, mq_ref, mk_ref, mv_ref, tm=tm, seq=seq)


def _even_in_tail(s, z, q_all, q_rot, k_all, v_all, gq_ref, gk_ref, rtab_ref, mtab_ref,
                  rq_ref, rk_ref, rv_ref, rg_ref, mq_ref, mk_ref, mv_ref, *, tm, seq):
    o = B_Q_LORA + B_KV_LORA
    kr = z[:, o:o + LANES]
    kr_rot = z[:, o + LANES:o + 2 * LANES]
    mcos, msin = mtab_ref[0], mtab_ref[1]
    lane = lax.broadcasted_iota(jnp.int32, (tm, LANES), 1)
    v_ones = jnp.where(lane >= HALF, 1.0, 0.0)
    row = lax.broadcasted_iota(jnp.int32, (tm, LANES), 0) + pl.program_id(1) * tm
    bias_lane = lane == B_QK
    q_bias = jnp.where(bias_lane, 1.0, 0.0)
    k_bias = jnp.where(bias_lane & _is_pad_row(row, seq), NEG, 0.0)
    inv = 1.0 / B_QK
    qcos, qsin = mcos * gq_ref[0:1], msin * gq_ref[1:2]
    kcos, ksin = mcos * gk_ref[0:1], msin * gk_ref[1:2]
    kr_sin = kr_rot * ksin
    heads = [slice(hd * LANES, (hd + 1) * LANES) for hd in range(B_HEADS)]
    qhs = [q_all[:, sl] for sl in heads]
    khs = [k_all[:, sl] + kr for sl in heads]
    rqs = [lax.rsqrt(jnp.sum(qh * qh, axis=-1, keepdims=True) * inv + EPS) for qh in qhs]
    rks = [lax.rsqrt(jnp.sum(kh * kh, axis=-1, keepdims=True) * inv + EPS) for kh in khs]
    for sl, qh, kh, rq, rk in zip(heads, qhs, khs, rqs, rks):
        mq_ref[s, :, sl] = ((qh * qcos + q_rot[:, sl] * qsin) * rq + q_bias).astype(BF16)
        mk_ref[s, :, sl] = ((kh * kcos + kr_sin) * rk + k_bias).astype(BF16)
        mv_ref[s, :, sl] = (v_all[:, sl] + v_ones).astype(BF16)

    hw = A_HEADS * A_DK
    o += 2 * LANES
    rcos, rsin = rtab_ref[0], rtab_ref[1]
    for c in range(hw // LANES):
        sl = slice(c * LANES, (c + 1) * LANES)
        q = z[:, o + c * LANES:o + (c + 1) * LANES]
        k = z[:, o + hw + c * LANES:o + hw + (c + 1) * LANES]
        rq_ref[s, :, sl] = (q * rcos + pltpu.roll(q, HALF, 1) * rsin).astype(BF16)
        rk_ref[s, :, sl] = ((k * rcos + pltpu.roll(k, HALF, 1) * rsin) * (A_DK ** -0.5)).astype(BF16)
    rv_ref[s] = z[:, o + 2 * hw:o + 3 * hw].astype(BF16)
    ga = z[:, o + 3 * hw:o + 4 * hw]
    rg_ref[s] = (ga * (1.0 / (1.0 + jnp.exp(-ga)))).astype(BF16)


def _even_in(h, p):
    b, lp, d = h.shape
    tm = _pick(lp, SEQ_TILES)
    ns = _pick(b, SEQS_PER_STEP)
    tok = lambda w: pl.BlockSpec((ns, tm, w), lambda i, j: (i, j, 0))
    tab = pl.BlockSpec((2, tm, LANES), lambda i, j: (0, j, 0))
    consts = [p["mix_g"], p["w_in"], p["q_norm"], p["w_qb"], p["w_qr"], p["kv_norm"], p["w_kb"], p["w_vb"],
              p["gq"], p["gk"]]
    hw = A_HEADS * A_DK
    widths = [hw, hw, hw, hw, B_HEADS * LANES, B_HEADS * LANES, B_HEADS * LANES]
    return pl.pallas_call(
        functools.partial(_even_in_body, tm=tm, seq=lp - BLOCK),
        grid=(b // ns, lp // tm),
        in_specs=[tok(d)] + [_resident(c.shape) for c in consts] + [tab, tab],
        out_specs=[tok(w) for w in widths],
        out_shape=[jax.ShapeDtypeStruct((b, lp, w), BF16) for w in widths],
        compiler_params=_params("parallel", "parallel"),
        name="even_in",
    )(h, *consts, p["ret_tab"], p["mla_tab"])


def _ret_body(q_ref, k_ref, v_ref, g_ref, dmat_ref, vec_ref, gain_ref, o_ref, acc_ref, kv_ref, st_ref,
              *, nb):
    lane = lax.broadcasted_iota(jnp.int32, (BLOCK, LANES), 1)
    first = lane < HALF
    wkf, wkb, qf, qb = vec_ref[0, 0], vec_ref[0, 1], vec_ref[0, 2], vec_ref[0, 3]
    cf, cb = vec_ref[0, 4][:1], vec_ref[0, 5][:1]
    dcat = jnp.concatenate([dmat_ref[0], dmat_ref[1]], axis=0)
    qk_first = lane % HALF < HALF // 2
    keep0 = jnp.where(qk_first, 1.0, 0.0).astype(BF16)
    keep1 = jnp.where(qk_first, 0.0, 1.0).astype(BF16)
    row2 = lax.broadcasted_iota(jnp.int32, (2 * LANES, LANES), 0) % HALF
    col2 = lax.broadcasted_iota(jnp.int32, (2 * LANES, LANES), 1)
    same2 = (row2 < HALF // 2) == (col2 < HALF)
    unroll = _pick(nb, RET_GROUPS)

    def intra(grp, carry):
        ns = [grp * unroll + u for u in range(unroll)]
        rows = [pl.ds(pl.multiple_of(n * BLOCK, BLOCK), BLOCK) for n in ns]
        qs = [q_ref[0, r, :] for r in rows]
        ks = [k_ref[0, r, :] for r in rows]
        vs = [v_ref[0, r, :] for r in rows]
        ss = [_dot_nt(jnp.concatenate([q * keep0, q * keep1], axis=0), k) for q, k in zip(qs, ks)]
        kws = [jnp.concatenate([k.astype(F32) * wkf, k.astype(F32) * wkb], axis=1).T.astype(BF16)
               for k in ks]
        kvs = [_dot(kw, v) for kw, v in zip(kws, vs)]
        ss = [(s * dcat).astype(BF16) for s in ss]
        os = [_dot(s, v) for s, v in zip(ss, vs)]
        for n, r, kv, o2 in zip(ns, rows, kvs, os):
            kv_ref[n] = jnp.where(same2, kv, 0.0)
            acc_ref[r, :] = jnp.where(first, o2[:BLOCK], o2[BLOCK:])
        return carry

    lax.fori_loop(0, nb // unroll, intra, 0)

    def scan(t, carry):
        sf, sb = carry
        i = lax.rem(t + nb - 1, nb)
        j = lax.rem(2 * nb - 2 - t, nb)
        st_ref[i, :LANES, :] = sf.astype(BF16)
        st_ref[j, LANES:, :] = sb.astype(BF16)
        return sf * cf + kv_ref[i, :LANES, :], sb * cb + kv_ref[j, LANES:, :]

    zero = jnp.zeros((LANES, LANES), F32)
    lax.fori_loop(0, nb, scan, (zero, zero))

    def inter(n, carry):
        rows = pl.ds(pl.multiple_of(n * BLOCK, BLOCK), BLOCK)
        qf32 = q_ref[0, rows, :].astype(F32)
        qq = jnp.concatenate([qf32 * qf, qf32 * qb], axis=1).astype(BF16)
        o = acc_ref[rows, :] + _dot(qq, st_ref[n])
        sq = o * o
        inv = 1.0 / HALF
        r0 = lax.rsqrt(jnp.sum(jnp.where(first, sq, 0.0), axis=-1, keepdims=True) * inv + EPS)
        r1 = lax.rsqrt(jnp.sum(jnp.where(first, 0.0, sq), axis=-1, keepdims=True) * inv + EPS)
        out = o * jnp.where(first, r0, r1) * gain_ref[0] * g_ref[0, rows, :].astype(F32)
        o_ref[0, rows, :] = out.astype(BF16)
        return carry

    lax.fori_loop(0, nb, inter, 0, unroll=unroll)


def _retention(rq, rk, rv, rg, p):
    b, lp, hw = rq.shape
    pairs = hw // LANES
    nb = lp // BLOCK
    seq = pl.BlockSpec((1, lp, LANES), lambda i, j: (i, 0, j))
    return pl.pallas_call(
        functools.partial(_ret_body, nb=nb),
        grid=(b, pairs),
        in_specs=[seq, seq, seq, seq,
                  pl.BlockSpec((2, BLOCK, BLOCK), lambda i, j: (j, 0, 0)),
                  pl.BlockSpec((1, 6, BLOCK, LANES), lambda i, j: (j, 0, 0, 0)),
                  pl.BlockSpec((1, 1, LANES), lambda i, j: (j, 0, 0))],
        out_specs=seq,
        out_shape=jax.ShapeDtypeStruct((b, lp, hw), BF16),
        scratch_shapes=[pltpu.VMEM((lp, LANES), F32), pltpu.VMEM((nb, 2 * LANES, LANES), F32),
                        pltpu.VMEM((nb, 2 * LANES, LANES), BF16)],
        compiler_params=_params("parallel", "parallel"),
        name="retention",
    )(rq, rk, rv, rg, p["ret_dmat"], p["ret_vec"], p["ret_gain"])


def _mla_body(q_ref, k_ref, v_ref, o_ref, *, tq, nh):
    lp = k_ref.shape[1]
    chunks = [(c, min(MXU_TILE, lp - c)) for c in range(0, lp, MXU_TILE)]
    hk = lp // 2 // MXU_TILE * MXU_TILE

    def scores(hd):
        q = q_ref[0, :, hd * LANES:(hd + 1) * LANES]
        return [_dot_nt(q, k_ref[0, c:c + w, hd * LANES:(hd + 1) * LANES]) for c, w in chunks]

    def probs(score_chunks):
        blocks = []
        for r in range(0, tq, MLA_ROWS):
            sb = jnp.concatenate([c[r:r + MLA_ROWS, :] for c in score_chunks], axis=1)
            m = jnp.max(sb, axis=-1, keepdims=True)
            blocks.append(jnp.exp2(sb - m).astype(BF16))
        return jnp.concatenate(blocks, axis=0)

    def weighted(hd, p):
        v = v_ref[0, :, hd * LANES:(hd + 1) * LANES]
        if hk == 0:
            return _dot(p, v)
        return _dot(p[:, :hk], v[:hk]) + _dot(p[:, hk:], v[hk:])

    outs = []
    sc = [scores(0)]
    for hd in range(1, nh):
        sc.append(scores(hd))
        outs.append(weighted(hd - 1, probs(sc[hd - 1])))
    outs.append(weighted(nh - 1, probs(sc[nh - 1])))
    lane = lax.broadcasted_iota(jnp.int32, (tq, LANES), 1)
    row = lax.broadcasted_iota(jnp.int32, (tq, LANES), 0) + pl.program_id(2) * tq
    pad = _is_pad_row(row, lp - BLOCK)
    for j in range(nh // 2):
        oa, ob = outs[2 * j], outs[2 * j + 1]
        out = jnp.where(lane < HALF, oa * pltpu.roll(1.0 / oa, HALF, 1),
                        pltpu.roll(ob, HALF, 1) * (1.0 / ob))
        o_ref[0, :, j * LANES:(j + 1) * LANES] = jnp.where(pad, 0.0, out).astype(BF16)


def _mla(mq, mk, mv):
    b, lp, _ = mq.shape
    nh = _pick(B_HEADS, MLA_HEADS)
    tq = _pick(lp, SEQ_TILES)
    return pl.pallas_call(
        functools.partial(_mla_body, tq=tq, nh=nh),
        grid=(b, B_HEADS // nh, lp // tq),
        in_specs=[pl.BlockSpec((1, tq, nh * LANES), lambda i, j, t: (i, t, j)),
                  pl.BlockSpec((1, lp, nh * LANES), lambda i, j, t: (i, 0, j)),
                  pl.BlockSpec((1, lp, nh * LANES), lambda i, j, t: (i, 0, j))],
        out_specs=pl.BlockSpec((1, tq, nh * B_DV), lambda i, j, t: (i, t, j)),
        out_shape=jax.ShapeDtypeStruct((b, lp, B_HEADS * B_DV), BF16),
        compiler_params=_params("parallel", "parallel", "parallel"),
        name="mla",
    )(mq, mk, mv)


def _odd_in_body(h_ref, g_ref, win_ref, gq_ref, gk_ref, tab_ref, q_ref, k_ref, vt_ref, *, tm):
    cos, sin = tab_ref[0], tab_ref[1]
    is_a = lax.broadcasted_iota(jnp.int32, (tm, LANES), 1) % HALF < HALF // 2
    nq = C_HEADS * C_DH
    nk = 2 * C_KV_HEADS * C_DH
    inv = 1.0 / C_DH
    rope = lambda y: y * cos + pltpu.roll(y, HALF, 1) * sin
    zs = [_dot(_rms_rows(h_ref[s], g_ref[...]).astype(BF16), win_ref[...]) for s in range(h_ref.shape[0])]
    for s, z in enumerate(zs):
        xqs = [z[:, c * LANES:(c + 1) * LANES] for c in range(nq // LANES)]
        xks = [z[:, nq + c * LANES:nq + (c + 1) * LANES] for c in range(nk // LANES)]
        rqs = []
        for xq in xqs:
            sq = xq * xq
            ra = lax.rsqrt(jnp.sum(jnp.where(is_a, sq, 0.0), axis=-1, keepdims=True) * inv + EPS)
            rb = lax.rsqrt(jnp.sum(jnp.where(is_a, 0.0, sq), axis=-1, keepdims=True) * inv + EPS)
            rqs.append(jnp.where(is_a, ra, rb))
        rks = [lax.rsqrt(jnp.sum(xk * xk, axis=-1, keepdims=True) * (0.5 * inv) + EPS) for xk in xks]
        for c, (xq, r) in enumerate(zip(xqs, rqs)):
            q_ref[s, :, c * LANES:(c + 1) * LANES] = rope(xq * r * gq_ref[...]).astype(BF16)
        for c, (xk, r) in enumerate(zip(xks, rks)):
            k_ref[s, :, c * LANES:(c + 1) * LANES] = rope(xk * r * gk_ref[...]).astype(BF16)
        vt_ref[s] = z[:, nq + nk:].T.astype(BF16)


def _odd_in(h, p):
    b, lp, d = h.shape
    tm = _pick(lp, SEQ_TILES)
    ns = _pick(b, SEQS_PER_STEP)
    tok = lambda w: pl.BlockSpec((ns, tm, w), lambda i, j: (i, j, 0))
    tab = pl.BlockSpec((2, tm, LANES), lambda i, j: (0, j, 0))
    consts = [p["mix_g"], p["w_in"], p["gq"], p["gk"]]
    nq, nk, nv = C_HEADS * C_DH, 2 * C_KV_HEADS * C_DH, C_KV_HEADS * C_DH
    return pl.pallas_call(
        functools.partial(_odd_in_body, tm=tm),
        grid=(b // ns, lp // tm),
        in_specs=[tok(d)] + [_resident(c.shape) for c in consts] + [tab],
        out_specs=[tok(nq), tok(nk), pl.BlockSpec((ns, nv, tm), lambda i, j: (i, 0, j))],
        out_shape=[jax.ShapeDtypeStruct((b, lp, nq), BF16), jax.ShapeDtypeStruct((b, lp, nk), BF16),
                   jax.ShapeDtypeStruct((b, nv, lp), BF16)],
        compiler_params=_params("parallel", "parallel"),
        name="odd_in",
    )(h, *consts, p["swa_tab"])


def _swa_body(sink_ref, q_ref, *refs, nb, nq):
    km_ref, k_refs = refs[0], refs[1:nq + 3]
    vm_ref, v_refs = refs[nq + 3], refs[nq + 4:2 * nq + 6]
    o_ref = refs[-1]
    nband = 3 * BLOCK
    group = C_HEADS // C_KV_HEADS
    kk = lax.broadcasted_iota(jnp.int32, (nband, BLOCK), 0)
    qq = lax.broadcasted_iota(jnp.int32, (nband, BLOCK), 1)
    lane = lax.broadcasted_iota(jnp.int32, (BLOCK, LANES), 1)
    rowi = lax.broadcasted_iota(jnp.int32, (BLOCK, LANES), 0)
    is_a = lane % HALF < HALF // 2
    keep = (jnp.where(is_a, 1.0, 0.0).astype(BF16), jnp.where(is_a, 0.0, 1.0).astype(BF16))
    ones = jnp.ones((C_DH, BLOCK + nband), BF16)
    no_pad_keys = jnp.zeros((FRONT_PAD, BLOCK), BF16)
    seq_len = nb * BLOCK - FRONT_PAD
    masks, rows_ok, meta_scores, band_scores = [], [], [], []
    for i in range(nq):
        stored = pl.program_id(1) * nq + i
        n = jnp.where(stored == nb - 1, 0, stored + 1)
        q_pos = n * BLOCK + qq - FRONT_PAD
        k_pos = (n - 1) * BLOCK + kk - FRONT_PAD
        masks.append((k_pos >= N_META) & (k_pos < seq_len) & (jnp.abs(q_pos - k_pos) <= C_WINDOW))
        rows_ok.append(rowi + n * BLOCK >= FRONT_PAD)
        rows = slice(i * BLOCK, (i + 1) * BLOCK)
        for kv in range(C_KV_HEADS):
            kvs = slice(kv * LANES, (kv + 1) * LANES)
            kcat = jnp.concatenate([r[0, :, kvs] for r in (km_ref, k_refs[i], k_refs[i + 1], k_refs[i + 2])],
                                   axis=0)
            pairs = [q_ref[0, rows, (kv * group // 2 + j) * LANES:(kv * group // 2 + j + 1) * LANES]
                     for j in range(group // 2)]
            qs = jnp.concatenate([p2 * keep[w] for p2 in pairs for w in range(2)], axis=0)
            st = _dot_nt(kcat, qs)
            meta_scores.append(st[:N_META])
            band_scores.append(st[N_META:])
    for i in range(nq):
        rows = slice(i * BLOCK, (i + 1) * BLOCK)
        for kv in range(C_KV_HEADS):
            sm_all, sb_all = meta_scores[i * C_KV_HEADS + kv], band_scores[i * C_KV_HEADS + kv]
            vt = jnp.concatenate([r[0, kv * C_DH:(kv + 1) * C_DH, :]
                                  for r in (vm_ref, v_refs[i], v_refs[i + 1], v_refs[i + 2])], axis=1)
            vt = jnp.concatenate([vt, ones], axis=0)
            normed = []
            for g in range(group):
                sink = sink_ref[kv * group + g] * LOG2E
                sm = sm_all[:, g * BLOCK:(g + 1) * BLOCK]
                sb = jnp.where(masks[i], sb_all[:, g * BLOCK:(g + 1) * BLOCK], NEG)
                m = jnp.maximum(jnp.maximum(jnp.max(sm, axis=0, keepdims=True),
                                            jnp.max(sb, axis=0, keepdims=True)), sink)
                pt = jnp.concatenate([no_pad_keys, jnp.exp2(sm - m).astype(BF16),
                                      jnp.exp2(sb - m).astype(BF16)], axis=0)
                ot = _dot(vt, pt)
                den = ot[C_DH:C_DH + 1, :] + jnp.exp2(sink - m)
                normed.append(ot[:C_DH, :] * (1.0 / den))
            for j in range(group // 2):
                slab = jnp.concatenate([normed[2 * j], normed[2 * j + 1]], axis=0).T
                sl = slice((kv * group // 2 + j) * LANES, (kv * group // 2 + j + 1) * LANES)
                o_ref[0, rows, sl] = jnp.where(rows_ok[i], slab, 0.0).astype(BF16)


def _swa(q, k, vt, sink):
    b, lp, qw = q.shape
    nb = lp // BLOCK
    kw = k.shape[-1]
    vw = vt.shape[1]
    nq = _pick(nb, SWA_BLOCKS)
    meta = nb - 1

    def slot(t):
        def index(n):
            raw = n * nq - 1 + t
            return jnp.where((raw < 0) | (raw >= nb), 0, raw)
        return index

    slots = [slot(t) for t in range(nq + 2)]
    qspec = pl.BlockSpec((1, nq * BLOCK, qw), lambda i, n: (i, n, 0))
    kspecs = [pl.BlockSpec((1, N_META, kw), lambda i, n: (i, lp // N_META - 1, 0))]
    kspecs += [pl.BlockSpec((1, BLOCK, kw), lambda i, n, f=f: (i, f(n), 0)) for f in slots]
    vspecs = [pl.BlockSpec((1, vw, BLOCK), lambda i, n: (i, 0, meta))]
    vspecs += [pl.BlockSpec((1, vw, BLOCK), lambda i, n, f=f: (i, 0, f(n))) for f in slots]
    return pl.pallas_call(
        functools.partial(_swa_body, nb=nb, nq=nq),
        grid=(b, nb // nq),
        in_specs=[pl.BlockSpec(memory_space=pltpu.SMEM), qspec] + kspecs + vspecs,
        out_specs=qspec,
        out_shape=jax.ShapeDtypeStruct((b, lp, qw), BF16),
        compiler_params=_params("parallel", "parallel"),
        name="swa",
    )(sink, q, *([k] * (nq + 3)), *([vt] * (nq + 3)))


def _rope_table(lp, theta, rot, offset):
    half = rot // 2
    lane = np.arange(LANES) - offset
    in_lo = (lane >= 0) & (lane < half)
    in_hi = (lane >= half) & (lane < rot)
    idx = np.where(in_lo | in_hi, lane % half, 0)
    inv = theta ** (-jnp.asarray(idx, F32) * 2.0 / rot)
    ang = _positions(lp)[:, None] * inv[None, :]
    sin = jnp.sin(ang)
    return jnp.stack([jnp.where(in_lo | in_hi, jnp.cos(ang), 1.0),
                      jnp.where(in_lo, -sin, jnp.where(in_hi, sin, 0.0))])


def _split_rope_table(lp, theta, rot):
    lanes = np.arange(LANES)
    r = lanes % (HALF // 2)
    active = jnp.asarray(r < rot // 2)
    inv = theta ** (-jnp.asarray(np.where(r < rot // 2, r, 0), F32) * 2.0 / rot)
    ang = _positions(lp)[:, None] * inv[None, :]
    sign = jnp.where(jnp.asarray(lanes < HALF), -1.0, 1.0)
    return jnp.stack([jnp.where(active, jnp.cos(ang), 1.0), jnp.where(active, sign * jnp.sin(ang), 0.0)])


def _pair_lanes(x):
    h, n = x.shape
    return jnp.repeat(x.reshape(h // 2, 2, n).transpose(0, 2, 1), HALF, axis=2)


def _prep_ffn(gain, wg, wu, wd):
    d, f = wg.shape
    return (gain.reshape(1, d), wg.astype(BF16), wu.astype(BF16),
            wd.reshape(f // FFN_CHUNK, FFN_CHUNK, d).astype(BF16))


def _prep_even(lp, mix_g, w_in, dec_f, dec_b, ret_norm, q_norm, w_qb, kv_norm, w_kvb, gq, gk, w_out):
    d = w_in.shape[0]
    hw = A_HEADS * A_DK
    cut = 4 * hw + B_Q_LORA + B_KV_LORA
    lanes = np.arange(LANES)
    partner = lanes.copy()
    partner[B_NOPE:B_NOPE + B_ROPE // 2] += B_ROPE // 2
    partner[B_NOPE + B_ROPE // 2:B_QK] -= B_ROPE // 2
    is_rope = jnp.asarray((lanes >= B_NOPE) & (lanes < B_QK))
    rot = lambda w: jnp.where(is_rope, w[..., partner], 0.0)
    kr_cols = jnp.concatenate([jnp.zeros((d, B_NOPE), F32), w_in[:, cut:],
                               jnp.zeros((d, LANES - B_QK), F32)], axis=1)
    qk_perm = lambda w: w.reshape(d, A_HEADS // 2, 2, 2, A_DK // 2).transpose(0, 1, 3, 2, 4).reshape(d, hw)
    w_in2 = jnp.concatenate([w_in[:, 4 * hw:cut], kr_cols, rot(kr_cols),
                             qk_perm(w_in[:, :hw]), qk_perm(w_in[:, hw:2 * hw]), w_in[:, 2 * hw:4 * hw]],
                            axis=1).astype(BF16)
    w_qb2 = jnp.pad(w_qb.reshape(B_Q_LORA, B_HEADS, B_QK), ((0, 0), (0, 0), (0, LANES - B_QK)))
    w_kv3 = w_kvb.reshape(B_KV_LORA, B_HEADS, B_NOPE + B_DV)
    w_kb = jnp.pad(w_kv3[:, :, :B_NOPE], ((0, 0), (0, 0), (0, LANES - B_NOPE)))
    w_vb = jnp.pad(w_kv3[:, :, B_NOPE:], ((0, 0), (0, 0), (0, LANES - B_DV)))
    pad96 = lambda g: jnp.pad(g, (0, LANES - B_QK)).reshape(1, LANES)
    with_rot = lambda g: jnp.concatenate([g, rot(g)], axis=0)
    ret_tab = _split_rope_table(lp, A_THETA, A_DK)
    qk_lanes = lambda x: _pair_lanes(x).reshape(x.shape[0] // 2, x.shape[1], 2, 2, A_DK // 2
                                                ).transpose(0, 1, 3, 2, 4).reshape(-1, x.shape[1], LANES)

    lgf = -jnp.exp(dec_f.astype(F32))
    lgb = -jnp.exp(dec_b.astype(F32))
    idx = jnp.arange(BLOCK, dtype=F32)
    diff = idx[:, None] - idx[None, :]
    dmat = (jnp.where(diff >= 0, jnp.exp(lgf[:, None, None] * jnp.maximum(diff, 0.0)), 0.0)
            + jnp.where(diff < 0, jnp.exp(lgb[:, None, None] * jnp.maximum(-diff, 0.0)), 0.0))
    ones = jnp.ones((BLOCK,), F32)
    vec = jnp.stack([
        qk_lanes(jnp.exp(lgf[:, None] * (BLOCK - 1 - idx)[None, :])),
        qk_lanes(jnp.exp(lgb[:, None] * idx[None, :])),
        qk_lanes(jnp.exp(lgf[:, None] * (idx + 1.0)[None, :])),
        qk_lanes(jnp.exp(lgb[:, None] * (BLOCK - idx)[None, :])),
        _pair_lanes(jnp.exp(BLOCK * lgf)[:, None] * ones[None, :]),
        _pair_lanes(jnp.exp(BLOCK * lgb)[:, None] * ones[None, :]),
    ], axis=1)
    half = A_HEADS * A_DK
    return dict(
        mix_g=mix_g.reshape(1, d), w_in=w_in2,
        q_norm=q_norm.reshape(1, -1), w_qb=w_qb2.reshape(B_Q_LORA, -1).astype(BF16),
        w_qr=rot(w_qb2).reshape(B_Q_LORA, -1).astype(BF16),
        kv_norm=kv_norm.reshape(1, -1), w_kb=w_kb.reshape(B_KV_LORA, -1).astype(BF16),
        w_vb=w_vb.reshape(B_KV_LORA, -1).astype(BF16),
        gq=with_rot(pad96(gq) * (B_QK ** -0.5 * LOG2E)), gk=with_rot(pad96(gk)),
        ret_tab=ret_tab,
        mla_tab=_rope_table(lp, B_THETA, B_ROPE, B_NOPE),
        ret_dmat=dmat, ret_vec=vec,
        ret_gain=ret_norm.reshape(A_HEADS // 2, 1, LANES),
        w_out=(w_out[:half].astype(BF16), w_out[half:].astype(BF16)),
    )


def _prep_odd(lp, mix_g, w_in, gq, gk, sink, w_out):
    d = w_in.shape[0]
    nq = C_HEADS * C_DH
    nkv = C_KV_HEADS * C_DH
    qtr, hr = C_DH // 2, C_ROT // 2
    perm = np.concatenate([np.arange(hr), np.arange(C_ROT, C_ROT + qtr - hr),
                           np.arange(hr, C_ROT), np.arange(C_ROT + qtr - hr, C_DH)])
    split = lambda w, heads: w.reshape(-1, heads, C_DH)[:, :, perm].reshape(-1, heads, 2, qtr)
    q_cols = lambda w: split(w, C_HEADS).reshape(-1, C_HEADS // 2, 2, 2, qtr).transpose(0, 1, 3, 2, 4
                                                                                         ).reshape(-1, nq)
    k_cols = lambda w: jnp.repeat(split(w, C_KV_HEADS)[:, :, :, None, :], 2, axis=3).reshape(-1, 2 * nkv)
    w_in2 = jnp.concatenate([q_cols(w_in[:, :nq]), k_cols(w_in[:, nq:nq + nkv]), w_in[:, nq + nkv:]], axis=1)
    slab_gain = lambda g: jnp.repeat(g[perm].reshape(2, 1, qtr), 2, axis=1).reshape(1, LANES)
    return dict(
        mix_g=mix_g.reshape(1, d), w_in=w_in2.astype(BF16),
        gq=slab_gain(gq) * (C_DH ** -0.5 * LOG2E), gk=slab_gain(gk),
        swa_tab=_split_rope_table(lp, C_THETA, C_ROT),
        sink=sink.astype(F32), w_out=(w_out.astype(BF16),),
    )


def _trunk(x, meta, layers):
    b, seq, d = x.shape
    lp = seq + BLOCK
    tail = jnp.concatenate([jnp.zeros((FRONT_PAD, d), x.dtype), meta.astype(x.dtype)], axis=0)
    tail = _ffn(tail, layers[0]["ffn0"])
    h = _ffn_head(x, tail, layers[0]["ffn0"]).reshape(b * lp, d)
    for i, layer in enumerate(layers):
        h3 = h.reshape(b, lp, d)
        mp = layer["mix"]
        if layer["even"]:
            rq, rk, rv, rg, mq, mk, mv = _even_in(h3, mp)
            xs = [_retention(rq, rk, rv, rg, mp), _mla(mq, mk, mv)]
        else:
            q, k, v = _odd_in(h3, mp)
            xs = [_swa(q, k, v, mp["sink"])]
        if i + 1 == len(layers):
            return _proj_ffn_tail(h3, xs, mp["w_out"], layer["ffn1"])
        xs = [a.reshape(b * lp, a.shape[-1]) for a in xs]
        h = _proj_ffn(h, xs, mp["w_out"], layer["ffn1"])
        h = _ffn(h, layers[i + 1]["ffn0"])


def kernel(x_prompt, x_sample, meta_tokens, ffn_norm, ffn_w_gate, ffn_w_up, ffn_w_down, mix_norm, even_w_in, ret_decay_f, ret_decay_b, ret_out_norm, mla_q_norm, mla_w_qb, mla_kv_norm, mla_w_kvb, mla_qk_norm_q, mla_qk_norm_k, even_w_out, odd_w_in, swa_q_norm, swa_k_norm, swa_sink, odd_w_out):
    depth = ffn_norm.shape[0]
    assert x_prompt.shape[1] == x_sample.shape[1]
    lp = x_prompt.shape[1] + BLOCK
    layers = []
    for layer in range(depth):
        i = layer // 2
        ffn = [_prep_ffn(ffn_norm[layer, s], ffn_w_gate[layer, s], ffn_w_up[layer, s], ffn_w_down[layer, s])
               for s in range(2)]
        if layer % 2 == 0:
            mix = _prep_even(lp, mix_norm[layer], even_w_in[i], ret_decay_f[i], ret_decay_b[i],
                             ret_out_norm[i], mla_q_norm[i], mla_w_qb[i], mla_kv_norm[i], mla_w_kvb[i],
                             mla_qk_norm_q[i], mla_qk_norm_k[i], even_w_out[i])
        else:
            mix = _prep_odd(lp, mix_norm[layer], odd_w_in[i], swa_q_norm[i], swa_k_norm[i],
                            swa_sink[i], odd_w_out[i])
        layers.append(dict(even=layer % 2 == 0, ffn0=ffn[0], ffn1=ffn[1], mix=mix))
    return (_trunk(x_prompt, meta_tokens, layers), _trunk(x_sample, meta_tokens, layers))
```

```python
import functools

import jax
import jax.numpy as jnp
import numpy as np
from jax import lax
from jax.experimental import pallas as pl
from jax.experimental.pallas import tpu as pltpu

F32 = jnp.float32
BF16 = jnp.bfloat16

D_MODEL = 1024
N_META = 16
BLOCK = 128
FRONT_PAD = BLOCK - N_META
EPS = 1e-6
NEG = -1e30
LOG2E = 1.4426950408889634
LANES = 128
HALF = LANES // 2

A_HEADS = 8
A_DK = 64
A_THETA = 10000.0
B_HEADS = 8
B_Q_LORA = 256
B_KV_LORA = 128
B_NOPE = 64
B_ROPE = 32
B_DV = 64
B_QK = B_NOPE + B_ROPE
B_THETA = 10000.0
C_HEADS = 16
C_KV_HEADS = 4
C_DH = 64
C_ROT = 16
C_THETA = 500000.0
C_WINDOW = 128

VMEM_LIMIT = 56 * 1024 * 1024
MXU_TILE = 256

FFN_TILES = (1024, 512, 384, 256, 128)
FFN_CHUNK = MXU_TILE
SEQ_TILES = (384, 128)
SEQS_PER_STEP = (2, 1)
RET_GROUPS = (33, 11, 3, 1)
SWA_BLOCKS = (11, 3, 1)
MLA_ROWS = 16
MLA_HEADS = (8, 4, 2)
MLA_LEAD = 2


def _params(*sem):
    return pltpu.CompilerParams(dimension_semantics=sem, vmem_limit_bytes=VMEM_LIMIT)


def _resident(shape):
    zeros = (0,) * len(shape)
    return pl.BlockSpec(shape, lambda *_: zeros, pipeline_mode=pl.Buffered(1))


def _rms_rows(x, gain):
    return x * lax.rsqrt(jnp.mean(x * x, axis=-1, keepdims=True) + EPS) * gain


def _dot(a, b):
    return jnp.dot(a, b, preferred_element_type=F32)


def _dot_nt(a, b):
    return lax.dot_general(a, b, (((1,), (1,)), ((), ())), preferred_element_type=F32)


def _is_pad_row(row, seq):
    return (row >= seq) & (row < seq + FRONT_PAD)


def _positions(lp):
    row = jnp.arange(lp, dtype=F32)
    seq = lp - BLOCK
    return jnp.where(row < seq, row + N_META, row - (seq + FRONT_PAD))


def _pick(n, options):
    for o in options:
        if n % o == 0:
            return o
    raise ValueError(f"no tile in {options} divides {n}")


def _ffn_core(x, g_ref, wg_ref, wu_ref, wd_ref):
    xn = _rms_rows(x, g_ref[...]).astype(BF16)
    acc = jnp.zeros_like(x)
    fc = wd_ref.shape[1]
    for c in range(wd_ref.shape[0]):
        gate = _dot(xn, wg_ref[:, c * fc:(c + 1) * fc])
        up = _dot(xn, wu_ref[:, c * fc:(c + 1) * fc])
        act = (gate * (1.0 / (1.0 + jnp.exp(-gate))) * up).astype(BF16)
        acc = acc + _dot(act, wd_ref[c])
    return x + 0.5 * acc


def _ffn_body(h_ref, g_ref, wg_ref, wu_ref, wd_ref, o_ref):
    o_ref[...] = _ffn_core(h_ref[...], g_ref, wg_ref, wu_ref, wd_ref)


def _ffn_head_body(x_ref, tail_ref, g_ref, wg_ref, wu_ref, wd_ref, o_ref, *, nt):
    t = pl.program_id(1)

    @pl.when(t < nt)
    def _():
        o_ref[0] = _ffn_core(x_ref[0], g_ref, wg_ref, wu_ref, wd_ref)

    @pl.when(t == nt)
    def _():
        o_ref[0, :BLOCK, :] = tail_ref[...]


def _proj_ffn_body(*refs, n_in):
    tile = lambda r: r[...] if len(r.shape) == 2 else r[0]
    h_ref = refs[0]
    xs = refs[1:1 + n_in]
    ws = refs[1 + n_in:1 + 2 * n_in]
    g_ref, wg_ref, wu_ref, wd_ref, o_ref = refs[1 + 2 * n_in:]
    h2 = tile(h_ref)
    for x_ref, w_ref in zip(xs, ws):
        h2 = h2 + _dot(tile(x_ref), w_ref[...])
    out = _ffn_core(h2, g_ref, wg_ref, wu_ref, wd_ref)
    if len(o_ref.shape) == 2:
        o_ref[...] = out
    else:
        o_ref[0] = out


def _ffn_specs(ffn):
    return [_resident(a.shape) for a in ffn]


def _ffn(h2d, ffn):
    t, d = h2d.shape
    tm = _pick(t, FFN_TILES)
    row = pl.BlockSpec((tm, d), lambda i: (i, 0))
    return pl.pallas_call(
        _ffn_body,
        grid=(t // tm,),
        in_specs=[row] + _ffn_specs(ffn),
        out_specs=row,
        out_shape=jax.ShapeDtypeStruct((t, d), F32),
        compiler_params=_params("parallel"),
        name="ffn",
    )(h2d, *ffn)


def _ffn_head(x, tail, ffn):
    b, seq, d = x.shape
    tm = _pick(seq, FFN_TILES)
    nt = seq // tm
    return pl.pallas_call(
        functools.partial(_ffn_head_body, nt=nt),
        grid=(b, nt + 1),
        in_specs=[pl.BlockSpec((1, tm, d), lambda i, t: (i, jnp.minimum(t, nt - 1), 0)),
                  pl.BlockSpec((BLOCK, d), lambda i, t: (0, 0))] + _ffn_specs(ffn),
        out_specs=pl.BlockSpec((1, tm, d), lambda i, t: (i, t, 0)),
        out_shape=jax.ShapeDtypeStruct((b, seq + BLOCK, d), F32),
        compiler_params=_params("parallel", "arbitrary"),
        name="ffn_head",
    )(x, tail, *ffn)


def _proj_ffn_tail(h, xs, ws, ffn):
    b, lp, d = h.shape
    seq = lp - BLOCK
    tm = _pick(seq, FFN_TILES)
    tok = lambda w: pl.BlockSpec((1, tm, w), lambda i, t: (i, t, 0))
    return pl.pallas_call(
        functools.partial(_proj_ffn_body, n_in=len(xs)),
        grid=(b, seq // tm),
        in_specs=[tok(d)] + [tok(x.shape[-1]) for x in xs] + [_resident(w.shape) for w in ws]
        + _ffn_specs(ffn),
        out_specs=tok(d),
        out_shape=jax.ShapeDtypeStruct((b, seq, d), F32),
        compiler_params=_params("parallel", "parallel"),
        name="proj_ffn_tail",
    )(h, *xs, *ws, *ffn)


def _proj_ffn(h2d, xs, ws, ffn):
    t, d = h2d.shape
    tm = _pick(t, FFN_TILES)
    row = pl.BlockSpec((tm, d), lambda i: (i, 0))
    x_specs = [pl.BlockSpec((tm, x.shape[1]), lambda i: (i, 0)) for x in xs]
    w_specs = [_resident(w.shape) for w in ws]
    return pl.pallas_call(
        functools.partial(_proj_ffn_body, n_in=len(xs)),
        grid=(t // tm,),
        in_specs=[row] + x_specs + w_specs + _ffn_specs(ffn),
        out_specs=row,
        out_shape=jax.ShapeDtypeStruct((t, d), F32),
        compiler_params=_params("parallel"),
        name="proj_ffn",
    )(h2d, *xs, *ws, *ffn)


def _even_in_body(h_ref, g_ref, win_ref, qn_ref, wqb_ref, wqr_ref, kvn_ref, wkb_ref, wvb_ref,
                  gq_ref, gk_ref, rtab_ref, mtab_ref,
                  rq_ref, rk_ref, rv_ref, rg_ref, mq_ref, mk_ref, mv_ref, *, tm, seq):
    ns = h_ref.shape[0]
    o = B_Q_LORA + B_KV_LORA
    for s in range(ns):
        z = _dot(_rms_rows(h_ref[s], g_ref[...]).astype(BF16), win_ref[...])
        cq = _rms_rows(z[:, :B_Q_LORA], qn_ref[...]).astype(BF16)
        ckv = _rms_rows(z[:, B_Q_LORA:o], kvn_ref[...]).astype(BF16)
        q_all = _dot(cq, wqb_ref[...])
        q_rot = _dot(cq, wqr_ref[...])
        k_all = _dot(ckv, wkb_ref[...])
        v_all = _dot(ckv, wvb_ref[...])
        _even_in_tail(s, z, q_all, q_rot, k_all, v_all, gq_ref, gk_ref, rtab_ref, mtab_ref,
                      rq_ref, rk_ref, rv_ref, rg_ref, mq_ref, mk_ref, mv_ref, tm=tm, seq=seq)


def _even_in_tail(s, z, q_all, q_rot, k_all, v_all, gq_ref, gk_ref, rtab_ref, mtab_ref,
                  rq_ref, rk_ref, rv_ref, rg_ref, mq_ref, mk_ref, mv_ref, *, tm, seq):
    o = B_Q_LORA + B_KV_LORA
    kr = z[:, o:o + LANES]
    kr_rot = z[:, o + LANES:o + 2 * LANES]
    mcos, msin = mtab_ref[0], mtab_ref[1]
    lane = lax.broadcasted_iota(jnp.int32, (tm, LANES), 1)
    v_ones = jnp.where(lane >= HALF, 1.0, 0.0)
    row = lax.broadcasted_iota(jnp.int32, (tm, LANES), 0) + pl.program_id(1) * tm
    bias_lane = lane == B_QK
    q_bias = jnp.where(bias_lane, 1.0, 0.0)
    k_bias = jnp.where(bias_lane & _is_pad_row(row, seq), NEG, 0.0)
    inv = 1.0 / B_QK
    qcos, qsin = mcos * gq_ref[0:1], msin * gq_ref[1:2]
    kcos, ksin = mcos * gk_ref[0:1], msin * gk_ref[1:2]
    kr_sin = kr_rot * ksin
    heads = [slice(hd * LANES, (hd + 1) * LANES) for hd in range(B_HEADS)]
    qhs = [q_all[:, sl] for sl in heads]
    khs = [k_all[:, sl] + kr for sl in heads]
    rqs = [lax.rsqrt(jnp.sum(qh * qh, axis=-1, keepdims=True) * inv + EPS) for qh in qhs]
    rks = [lax.rsqrt(jnp.sum(kh * kh, axis=-1, keepdims=True) * inv + EPS) for kh in khs]
    for sl, qh, kh, rq, rk in zip(heads, qhs, khs, rqs, rks):
        mq_ref[s, :, sl] = ((qh * qcos + q_rot[:, sl] * qsin) * rq + q_bias).astype(BF16)
        mk_ref[s, :, sl] = ((kh * kcos + kr_sin) * rk + k_bias).astype(BF16)
        mv_ref[s, :, sl] = (v_all[:, sl] + v_ones).astype(BF16)

    hw = A_HEADS * A_DK
    o += 2 * LANES
    rcos, rsin = rtab_ref[0], rtab_ref[1]
    for c in range(hw // LANES):
        sl = slice(c * LANES, (c + 1) * LANES)
        q = z[:, o + c * LANES:o + (c + 1) * LANES]
        k = z[:, o + hw + c * LANES:o + hw + (c + 1) * LANES]
        rq_ref[s, :, sl] = (q * rcos + pltpu.roll(q, HALF, 1) * rsin).astype(BF16)
        rk_ref[s, :, sl] = ((k * rcos + pltpu.roll(k, HALF, 1) * rsin) * (A_DK ** -0.5)).astype(BF16)
    rv_ref[s] = z[:, o + 2 * hw:o + 3 * hw].astype(BF16)
    ga = z[:, o + 3 * hw:o + 4 * hw]
    rg_ref[s] = (ga * (1.0 / (1.0 + jnp.exp(-ga)))).astype(BF16)


def _even_in(h, p):
    b, lp, d = h.shape
    tm = _pick(lp, SEQ_TILES)
    ns = _pick(b, SEQS_PER_STEP)
    tok = lambda w: pl.BlockSpec((ns, tm, w), lambda i, j: (i, j, 0))
    tab = pl.BlockSpec((2, tm, LANES), lambda i, j: (0, j, 0))
    consts = [p["mix_g"], p["w_in"], p["q_norm"], p["w_qb"], p["w_qr"], p["kv_norm"], p["w_kb"], p["w_vb"],
              p["gq"], p["gk"]]
    hw = A_HEADS * A_DK
    widths = [hw, hw, hw, hw, B_HEADS * LANES, B_HEADS * LANES, B_HEADS * LANES]
    return pl.pallas_call(
        functools.partial(_even_in_body, tm=tm, seq=lp - BLOCK),
        grid=(b // ns, lp // tm),
        in_specs=[tok(d)] + [_resident(c.shape) for c in consts] + [tab, tab],
        out_specs=[tok(w) for w in widths],
        out_shape=[jax.ShapeDtypeStruct((b, lp, w), BF16) for w in widths],
        compiler_params=_params("parallel", "parallel"),
        name="even_in",
    )(h, *consts, p["ret_tab"], p["mla_tab"])


def _ret_body(q_ref, k_ref, v_ref, g_ref, dmat_ref, vec_ref, gain_ref, o_ref, acc_ref, kv_ref, st_ref,
              *, nb):
    lane = lax.broadcasted_iota(jnp.int32, (BLOCK, LANES), 1)
    first = lane < HALF
    wkf, wkb, qf, qb = vec_ref[0, 0], vec_ref[0, 1], vec_ref[0, 2], vec_ref[0, 3]
    cf, cb = vec_ref[0, 4][:1], vec_ref[0, 5][:1]
    dcat = jnp.concatenate([dmat_ref[0], dmat_ref[1]], axis=0)
    qk_first = lane % HALF < HALF // 2
    keep0 = jnp.where(qk_first, 1.0, 0.0).astype(BF16)
    keep1 = jnp.where(qk_first, 0.0, 1.0).astype(BF16)
    row2 = lax.broadcasted_iota(jnp.int32, (2 * LANES, LANES), 0) % HALF
    col2 = lax.broadcasted_iota(jnp.int32, (2 * LANES, LANES), 1)
    same2 = (row2 < HALF // 2) == (col2 < HALF)
    unroll = _pick(nb, RET_GROUPS)

    def intra(grp, carry):
        ns = [grp * unroll + u for u in range(unroll)]
        rows = [pl.ds(pl.multiple_of(n * BLOCK, BLOCK), BLOCK) for n in ns]
        qs = [q_ref[0, r, :] for r in rows]
        ks = [k_ref[0, r, :] for r in rows]
        vs = [v_ref[0, r, :] for r in rows]
        ss = [_dot_nt(jnp.concatenate([q * keep0, q * keep1], axis=0), k) for q, k in zip(qs, ks)]
        kws = [jnp.concatenate([k.astype(F32) * wkf, k.astype(F32) * wkb], axis=1).T.astype(BF16)
               for k in ks]
        kvs = [_dot(kw, v) for kw, v in zip(kws, vs)]
        ss = [(s * dcat).astype(BF16) for s in ss]
        os = [_dot(s, v) for s, v in zip(ss, vs)]
        for n, r, kv, o2 in zip(ns, rows, kvs, os):
            kv_ref[n] = jnp.where(same2, kv, 0.0)
            acc_ref[r, :] = jnp.where(first, o2[:BLOCK], o2[BLOCK:])
        return carry

    lax.fori_loop(0, nb // unroll, intra, 0)

    def scan(t, carry):
        sf, sb = carry
        i = lax.rem(t + nb - 1, nb)
        j = lax.rem(2 * nb - 2 - t, nb)
        st_ref[i, :LANES, :] = sf.astype(BF16)
        st_ref[j, LANES:, :] = sb.astype(BF16)
        return sf * cf + kv_ref[i, :LANES, :], sb * cb + kv_ref[j, LANES:, :]

    zero = jnp.zeros((LANES, LANES), F32)
    lax.fori_loop(0, nb, scan, (zero, zero))

    def inter(n, carry):
        rows = pl.ds(pl.multiple_of(n * BLOCK, BLOCK), BLOCK)
        qf32 = q_ref[0, rows, :].astype(F32)
        qq = jnp.concatenate([qf32 * qf, qf32 * qb], axis=1).astype(BF16)
        o = acc_ref[rows, :] + _dot(qq, st_ref[n])
        sq = o * o
        inv = 1.0 / HALF
        r0 = lax.rsqrt(jnp.sum(jnp.where(first, sq, 0.0), axis=-1, keepdims=True) * inv + EPS)
        r1 = lax.rsqrt(jnp.sum(jnp.where(first, 0.0, sq), axis=-1, keepdims=True) * inv + EPS)
        out = o * jnp.where(first, r0, r1) * gain_ref[0] * g_ref[0, rows, :].astype(F32)
        o_ref[0, rows, :] = out.astype(BF16)
        return carry

    lax.fori_loop(0, nb, inter, 0, unroll=unroll)


def _retention(rq, rk, rv, rg, p):
    b, lp, hw = rq.shape
    pairs = hw // LANES
    nb = lp // BLOCK
    seq = pl.BlockSpec((1, lp, LANES), lambda i, j: (i, 0, j))
    return pl.pallas_call(
        functools.partial(_ret_body, nb=nb),
        grid=(b, pairs),
        in_specs=[seq, seq, seq, seq,
                  pl.BlockSpec((2, BLOCK, BLOCK), lambda i, j: (j, 0, 0)),
                  pl.BlockSpec((1, 6, BLOCK, LANES), lambda i, j: (j, 0, 0, 0)),
                  pl.BlockSpec((1, 1, LANES), lambda i, j: (j, 0, 0))],
        out_specs=seq,
        out_shape=jax.ShapeDtypeStruct((b, lp, hw), BF16),
        scratch_shapes=[pltpu.VMEM((lp, LANES), F32), pltpu.VMEM((nb, 2 * LANES, LANES), F32),
                        pltpu.VMEM((nb, 2 * LANES, LANES), BF16)],
        compiler_params=_params("parallel", "parallel"),
        name="retention",
    )(rq, rk, rv, rg, p["ret_dmat"], p["ret_vec"], p["ret_gain"])


def _mla_body(q_ref, k_ref, v_ref, o_ref, *, tq, nh):
    lp = k_ref.shape[1]
    chunks = [(c, min(MXU_TILE, lp - c)) for c in range(0, lp, MXU_TILE)]
    hk = lp // 2 // MXU_TILE * MXU_TILE

    def scores(hd, part):
        q = q_ref[0, :, hd * LANES:(hd + 1) * LANES]
        return [_dot_nt(q, k_ref[0, c:c + w, hd * LANES:(hd + 1) * LANES]) for c, w in part]

    def probs(score_chunks):
        blocks = []
        for r in range(0, tq, MLA_ROWS):
            sb = jnp.concatenate([c[r:r + MLA_ROWS, :] for c in score_chunks], axis=1)
            m = jnp.max(sb, axis=-1, keepdims=True)
            blocks.append(jnp.exp2(sb - m).astype(BF16))
        return jnp.concatenate(blocks, axis=0)

    def weighted(hd, p):
        v = v_ref[0, :, hd * LANES:(hd + 1) * LANES]
        if hk == 0:
            return _dot(p, v)
        return _dot(p[:, :hk], v[:hk]) + _dot(p[:, hk:], v[hk:])

    lead = chunks[:MLA_LEAD]
    outs = []
    sc = scores(0, chunks)
    for hd in range(nh):
        nxt = scores(hd + 1, lead) if hd + 1 < nh else []
        outs.append(weighted(hd, probs(sc)))
        sc = nxt + scores(hd + 1, chunks[MLA_LEAD:]) if hd + 1 < nh else []
    lane = lax.broadcasted_iota(jnp.int32, (tq, LANES), 1)
    row = lax.broadcasted_iota(jnp.int32, (tq, LANES), 0) + pl.program_id(2) * tq
    pad = _is_pad_row(row, lp - BLOCK)
    for j in range(nh // 2):
        oa, ob = outs[2 * j], outs[2 * j + 1]
        out = jnp.where(lane < HALF, oa * pltpu.roll(1.0 / oa, HALF, 1),
                        pltpu.roll(ob, HALF, 1) * (1.0 / ob))
        o_ref[0, :, j * LANES:(j + 1) * LANES] = jnp.where(pad, 0.0, out).astype(BF16)


def _mla(mq, mk, mv):
    b, lp, _ = mq.shape
    nh = _pick(B_HEADS, MLA_HEADS)
    tq = _pick(lp, SEQ_TILES)
    return pl.pallas_call(
        functools.partial(_mla_body, tq=tq, nh=nh),
        grid=(b, B_HEADS // nh, lp // tq),
        in_specs=[pl.BlockSpec((1, tq, nh * LANES), lambda i, j, t: (i, t, j)),
                  pl.BlockSpec((1, lp, nh * LANES), lambda i, j, t: (i, 0, j)),
                  pl.BlockSpec((1, lp, nh * LANES), lambda i, j, t: (i, 0, j))],
        out_specs=pl.BlockSpec((1, tq, nh * B_DV), lambda i, j, t: (i, t, j)),
        out_shape=jax.ShapeDtypeStruct((b, lp, B_HEADS * B_DV), BF16),
        compiler_params=_params("parallel", "parallel", "parallel"),
        name="mla",
    )(mq, mk, mv)


def _odd_in_body(h_ref, g_ref, win_ref, gq_ref, gk_ref, tab_ref, q_ref, k_ref, vt_ref, *, tm):
    cos, sin = tab_ref[0], tab_ref[1]
    is_a = lax.broadcasted_iota(jnp.int32, (tm, LANES), 1) % HALF < HALF // 2
    nq = C_HEADS * C_DH
    nk = 2 * C_KV_HEADS * C_DH
    inv = 1.0 / C_DH
    rope = lambda y: y * cos + pltpu.roll(y, HALF, 1) * sin
    zs = [_dot(_rms_rows(h_ref[s], g_ref[...]).astype(BF16), win_ref[...]) for s in range(h_ref.shape[0])]
    for s, z in enumerate(zs):
        xqs = [z[:, c * LANES:(c + 1) * LANES] for c in range(nq // LANES)]
        xks = [z[:, nq + c * LANES:nq + (c + 1) * LANES] for c in range(nk // LANES)]
        rqs = []
        for xq in xqs:
            sq = xq * xq
            ra = lax.rsqrt(jnp.sum(jnp.where(is_a, sq, 0.0), axis=-1, keepdims=True) * inv + EPS)
            rb = lax.rsqrt(jnp.sum(jnp.where(is_a, 0.0, sq), axis=-1, keepdims=True) * inv + EPS)
            rqs.append(jnp.where(is_a, ra, rb))
        rks = [lax.rsqrt(jnp.sum(xk * xk, axis=-1, keepdims=True) * (0.5 * inv) + EPS) for xk in xks]
        for c, (xq, r) in enumerate(zip(xqs, rqs)):
            q_ref[s, :, c * LANES:(c + 1) * LANES] = rope(xq * r * gq_ref[...]).astype(BF16)
        for c, (xk, r) in enumerate(zip(xks, rks)):
            k_ref[s, :, c * LANES:(c + 1) * LANES] = rope(xk * r * gk_ref[...]).astype(BF16)
        vt_ref[s] = z[:, nq + nk:].T.astype(BF16)


def _odd_in(h, p):
    b, lp, d = h.shape
    tm = _pick(lp, SEQ_TILES)
    ns = _pick(b, SEQS_PER_STEP)
    tok = lambda w: pl.BlockSpec((ns, tm, w), lambda i, j: (i, j, 0))
    tab = pl.BlockSpec((2, tm, LANES), lambda i, j: (0, j, 0))
    consts = [p["mix_g"], p["w_in"], p["gq"], p["gk"]]
    nq, nk, nv = C_HEADS * C_DH, 2 * C_KV_HEADS * C_DH, C_KV_HEADS * C_DH
    return pl.pallas_call(
        functools.partial(_odd_in_body, tm=tm),
        grid=(b // ns, lp // tm),
        in_specs=[tok(d)] + [_resident(c.shape) for c in consts] + [tab],
        out_specs=[tok(nq), tok(nk), pl.BlockSpec((ns, nv, tm), lambda i, j: (i, 0, j))],
        out_shape=[jax.ShapeDtypeStruct((b, lp, nq), BF16), jax.ShapeDtypeStruct((b, lp, nk), BF16),
                   jax.ShapeDtypeStruct((b, nv, lp), BF16)],
        compiler_params=_params("parallel", "parallel"),
        name="odd_in",
    )(h, *consts, p["swa_tab"])


def _swa_body(sink_ref, q_ref, *refs, nb, nq):
    km_ref, k_refs = refs[0], refs[1:nq + 3]
    vm_ref, v_refs = refs[nq + 3], refs[nq + 4:2 * nq + 6]
    o_ref = refs[-1]
    nband = 3 * BLOCK
    group = C_HEADS // C_KV_HEADS
    kk = lax.broadcasted_iota(jnp.int32, (nband, BLOCK), 0)
    qq = lax.broadcasted_iota(jnp.int32, (nband, BLOCK), 1)
    lane = lax.broadcasted_iota(jnp.int32, (BLOCK, LANES), 1)
    rowi = lax.broadcasted_iota(jnp.int32, (BLOCK, LANES), 0)
    is_a = lane % HALF < HALF // 2
    keep = (jnp.where(is_a, 1.0, 0.0).astype(BF16), jnp.where(is_a, 0.0, 1.0).astype(BF16))
    ones = jnp.ones((C_DH, BLOCK + nband), BF16)
    no_pad_keys = jnp.zeros((FRONT_PAD, BLOCK), BF16)
    seq_len = nb * BLOCK - FRONT_PAD
    masks, rows_ok, meta_scores, band_scores = [], [], [], []
    for i in range(nq):
        stored = pl.program_id(1) * nq + i
        n = jnp.where(stored == nb - 1, 0, stored + 1)
        q_pos = n * BLOCK + qq - FRONT_PAD
        k_pos = (n - 1) * BLOCK + kk - FRONT_PAD
        masks.append((k_pos >= N_META) & (k_pos < seq_len) & (jnp.abs(q_pos - k_pos) <= C_WINDOW))
        rows_ok.append(rowi + n * BLOCK >= FRONT_PAD)
        rows = slice(i * BLOCK, (i + 1) * BLOCK)
        for kv in range(C_KV_HEADS):
            kvs = slice(kv * LANES, (kv + 1) * LANES)
            kcat = jnp.concatenate([r[0, :, kvs] for r in (km_ref, k_refs[i], k_refs[i + 1], k_refs[i + 2])],
                                   axis=0)
            pairs = [q_ref[0, rows, (kv * group // 2 + j) * LANES:(kv * group // 2 + j + 1) * LANES]
                     for j in range(group // 2)]
            qs = jnp.concatenate([p2 * keep[w] for p2 in pairs for w in range(2)], axis=0)
            st = _dot_nt(kcat, qs)
            meta_scores.append(st[:N_META])
            band_scores.append(st[N_META:])
    for i in range(nq):
        rows = slice(i * BLOCK, (i + 1) * BLOCK)
        for kv in range(C_KV_HEADS):
            sm_all, sb_all = meta_scores[i * C_KV_HEADS + kv], band_scores[i * C_KV_HEADS + kv]
            vt = jnp.concatenate([r[0, kv * C_DH:(kv + 1) * C_DH, :]
                                  for r in (vm_ref, v_refs[i], v_refs[i + 1], v_refs[i + 2])], axis=1)
            vt = jnp.concatenate([vt, ones], axis=0)
            normed = []
            for g in range(group):
                sink = sink_ref[kv * group + g] * LOG2E
                sm = sm_all[:, g * BLOCK:(g + 1) * BLOCK]
                sb = jnp.where(masks[i], sb_all[:, g * BLOCK:(g + 1) * BLOCK], NEG)
                m = jnp.maximum(jnp.maximum(jnp.max(sm, axis=0, keepdims=True),
                                            jnp.max(sb, axis=0, keepdims=True)), sink)
                pt = jnp.concatenate([no_pad_keys, jnp.exp2(sm - m).astype(BF16),
                                      jnp.exp2(sb - m).astype(BF16)], axis=0)
                ot = _dot(vt, pt)
                den = ot[C_DH:C_DH + 1, :] + jnp.exp2(sink - m)
                normed.append(ot[:C_DH, :] * (1.0 / den))
            for j in range(group // 2):
                slab = jnp.concatenate([normed[2 * j], normed[2 * j + 1]], axis=0).T
                sl = slice((kv * group // 2 + j) * LANES, (kv * group // 2 + j + 1) * LANES)
                o_ref[0, rows, sl] = jnp.where(rows_ok[i], slab, 0.0).astype(BF16)


def _swa(q, k, vt, sink):
    b, lp, qw = q.shape
    nb = lp // BLOCK
    kw = k.shape[-1]
    vw = vt.shape[1]
    nq = _pick(nb, SWA_BLOCKS)
    meta = nb - 1

    def slot(t):
        def index(n):
            raw = n * nq - 1 + t
            return jnp.where((raw < 0) | (raw >= nb), 0, raw)
        return index

    slots = [slot(t) for t in range(nq + 2)]
    qspec = pl.BlockSpec((1, nq * BLOCK, qw), lambda i, n: (i, n, 0))
    kspecs = [pl.BlockSpec((1, N_META, kw), lambda i, n: (i, lp // N_META - 1, 0))]
    kspecs += [pl.BlockSpec((1, BLOCK, kw), lambda i, n, f=f: (i, f(n), 0)) for f in slots]
    vspecs = [pl.BlockSpec((1, vw, BLOCK), lambda i, n: (i, 0, meta))]
    vspecs += [pl.BlockSpec((1, vw, BLOCK), lambda i, n, f=f: (i, 0, f(n))) for f in slots]
    return pl.pallas_call(
        functools.partial(_swa_body, nb=nb, nq=nq),
        grid=(b, nb // nq),
        in_specs=[pl.BlockSpec(memory_space=pltpu.SMEM), qspec] + kspecs + vspecs,
        out_specs=qspec,
        out_shape=jax.ShapeDtypeStruct((b, lp, qw), BF16),
        compiler_params=_params("parallel", "parallel"),
        name="swa",
    )(sink, q, *([k] * (nq + 3)), *([vt] * (nq + 3)))


def _rope_table(lp, theta, rot, offset):
    half = rot // 2
    lane = np.arange(LANES) - offset
    in_lo = (lane >= 0) & (lane < half)
    in_hi = (lane >= half) & (lane < rot)
    idx = np.where(in_lo | in_hi, lane % half, 0)
    inv = theta ** (-jnp.asarray(idx, F32) * 2.0 / rot)
    ang = _positions(lp)[:, None] * inv[None, :]
    sin = jnp.sin(ang)
    return jnp.stack([jnp.where(in_lo | in_hi, jnp.cos(ang), 1.0),
                      jnp.where(in_lo, -sin, jnp.where(in_hi, sin, 0.0))])


def _split_rope_table(lp, theta, rot):
    lanes = np.arange(LANES)
    r = lanes % (HALF // 2)
    active = jnp.asarray(r < rot // 2)
    inv = theta ** (-jnp.asarray(np.where(r < rot // 2, r, 0), F32) * 2.0 / rot)
    ang = _positions(lp)[:, None] * inv[None, :]
    sign = jnp.where(jnp.asarray(lanes < HALF), -1.0, 1.0)
    return jnp.stack([jnp.where(active, jnp.cos(ang), 1.0), jnp.where(active, sign * jnp.sin(ang), 0.0)])


def _pair_lanes(x):
    h, n = x.shape
    return jnp.repeat(x.reshape(h // 2, 2, n).transpose(0, 2, 1), HALF, axis=2)


def _prep_ffn(gain, wg, wu, wd):
    d, f = wg.shape
    return (gain.reshape(1, d), wg.astype(BF16), wu.astype(BF16),
            wd.reshape(f // FFN_CHUNK, FFN_CHUNK, d).astype(BF16))


def _prep_even(lp, mix_g, w_in, dec_f, dec_b, ret_norm, q_norm, w_qb, kv_norm, w_kvb, gq, gk, w_out):
    d = w_in.shape[0]
    hw = A_HEADS * A_DK
    cut = 4 * hw + B_Q_LORA + B_KV_LORA
    lanes = np.arange(LANES)
    partner = lanes.copy()
    partner[B_NOPE:B_NOPE + B_ROPE // 2] += B_ROPE // 2
    partner[B_NOPE + B_ROPE // 2:B_QK] -= B_ROPE // 2
    is_rope = jnp.asarray((lanes >= B_NOPE) & (lanes < B_QK))
    rot = lambda w: jnp.where(is_rope, w[..., partner], 0.0)
    kr_cols = jnp.concatenate([jnp.zeros((d, B_NOPE), F32), w_in[:, cut:],
                               jnp.zeros((d, LANES - B_QK), F32)], axis=1)
    qk_perm = lambda w: w.reshape(d, A_HEADS // 2, 2, 2, A_DK // 2).transpose(0, 1, 3, 2, 4).reshape(d, hw)
    w_in2 = jnp.concatenate([w_in[:, 4 * hw:cut], kr_cols, rot(kr_cols),
                             qk_perm(w_in[:, :hw]), qk_perm(w_in[:, hw:2 * hw]), w_in[:, 2 * hw:4 * hw]],
                            axis=1).astype(BF16)
    w_qb2 = jnp.pad(w_qb.reshape(B_Q_LORA, B_HEADS, B_QK), ((0, 0), (0, 0), (0, LANES - B_QK)))
    w_kv3 = w_kvb.reshape(B_KV_LORA, B_HEADS, B_NOPE + B_DV)
    w_kb = jnp.pad(w_kv3[:, :, :B_NOPE], ((0, 0), (0, 0), (0, LANES - B_NOPE)))
    w_vb = jnp.pad(w_kv3[:, :, B_NOPE:], ((0, 0), (0, 0), (0, LANES - B_DV)))
    pad96 = lambda g: jnp.pad(g, (0, LANES - B_QK)).reshape(1, LANES)
    with_rot = lambda g: jnp.concatenate([g, rot(g)], axis=0)
    ret_tab = _split_rope_table(lp, A_THETA, A_DK)
    qk_lanes = lambda x: _pair_lanes(x).reshape(x.shape[0] // 2, x.shape[1], 2, 2, A_DK // 2
                                                ).transpose(0, 1, 3, 2, 4).reshape(-1, x.shape[1], LANES)

    lgf = -jnp.exp(dec_f.astype(F32))
    lgb = -jnp.exp(dec_b.astype(F32))
    idx = jnp.arange(BLOCK, dtype=F32)
    diff = idx[:, None] - idx[None, :]
    dmat = (jnp.where(diff >= 0, jnp.exp(lgf[:, None, None] * jnp.maximum(diff, 0.0)), 0.0)
            + jnp.where(diff < 0, jnp.exp(lgb[:, None, None] * jnp.maximum(-diff, 0.0)), 0.0))
    ones = jnp.ones((BLOCK,), F32)
    vec = jnp.stack([
        qk_lanes(jnp.exp(lgf[:, None] * (BLOCK - 1 - idx)[None, :])),
        qk_lanes(jnp.exp(lgb[:, None] * idx[None, :])),
        qk_lanes(jnp.exp(lgf[:, None] * (idx + 1.0)[None, :])),
        qk_lanes(jnp.exp(lgb[:, None] * (BLOCK - idx)[None, :])),
        _pair_lanes(jnp.exp(BLOCK * lgf)[:, None] * ones[None, :]),
        _pair_lanes(jnp.exp(BLOCK * lgb)[:, None] * ones[None, :]),
    ], axis=1)
    half = A_HEADS * A_DK
    return dict(
        mix_g=mix_g.reshape(1, d), w_in=w_in2,
        q_norm=q_norm.reshape(1, -1), w_qb=w_qb2.reshape(B_Q_LORA, -1).astype(BF16),
        w_qr=rot(w_qb2).reshape(B_Q_LORA, -1).astype(BF16),
        kv_norm=kv_norm.reshape(1, -1), w_kb=w_kb.reshape(B_KV_LORA, -1).astype(BF16),
        w_vb=w_vb.reshape(B_KV_LORA, -1).astype(BF16),
        gq=with_rot(pad96(gq) * (B_QK ** -0.5 * LOG2E)), gk=with_rot(pad96(gk)),
        ret_tab=ret_tab,
        mla_tab=_rope_table(lp, B_THETA, B_ROPE, B_NOPE),
        ret_dmat=dmat, ret_vec=vec,
        ret_gain=ret_norm.reshape(A_HEADS // 2, 1, LANES),
        w_out=(w_out[:half].astype(BF16), w_out[half:].astype(BF16)),
    )


def _prep_odd(lp, mix_g, w_in, gq, gk, sink, w_out):
    d = w_in.shape[0]
    nq = C_HEADS * C_DH
    nkv = C_KV_HEADS * C_DH
    qtr, hr = C_DH // 2, C_ROT // 2
    perm = np.concatenate([np.arange(hr), np.arange(C_ROT, C_ROT + qtr - hr),
                           np.arange(hr, C_ROT), np.arange(C_ROT + qtr - hr, C_DH)])
    split = lambda w, heads: w.reshape(-1, heads, C_DH)[:, :, perm].reshape(-1, heads, 2, qtr)
    q_cols = lambda w: split(w, C_HEADS).reshape(-1, C_HEADS // 2, 2, 2, qtr).transpose(0, 1, 3, 2, 4
                                                                                         ).reshape(-1, nq)
    k_cols = lambda w: jnp.repeat(split(w, C_KV_HEADS)[:, :, :, None, :], 2, axis=3).reshape(-1, 2 * nkv)
    w_in2 = jnp.concatenate([q_cols(w_in[:, :nq]), k_cols(w_in[:, nq:nq + nkv]), w_in[:, nq + nkv:]], axis=1)
    slab_gain = lambda g: jnp.repeat(g[perm].reshape(2, 1, qtr), 2, axis=1).reshape(1, LANES)
    return dict(
        mix_g=mix_g.reshape(1, d), w_in=w_in2.astype(BF16),
        gq=slab_gain(gq) * (C_DH ** -0.5 * LOG2E), gk=slab_gain(gk),
        swa_tab=_split_rope_table(lp, C_THETA, C_ROT),
        sink=sink.astype(F32), w_out=(w_out.astype(BF16),),
    )


def _trunk(x, meta, layers):
    b, seq, d = x.shape
    lp = seq + BLOCK
    tail = jnp.concatenate([jnp.zeros((FRONT_PAD, d), x.dtype), meta.astype(x.dtype)], axis=0)
    tail = _ffn(tail, layers[0]["ffn0"])
    h = _ffn_head(x, tail, layers[0]["ffn0"]).reshape(b * lp, d)
    for i, layer in enumerate(layers):
        h3 = h.reshape(b, lp, d)
        mp = layer["mix"]
        if layer["even"]:
            rq, rk, rv, rg, mq, mk, mv = _even_in(h3, mp)
            xs = [_retention(rq, rk, rv, rg, mp), _mla(mq, mk, mv)]
        else:
            q, k, v = _odd_in(h3, mp)
            xs = [_swa(q, k, v, mp["sink"])]
        if i + 1 == len(layers):
            return _proj_ffn_tail(h3, xs, mp["w_out"], layer["ffn1"])
        xs = [a.reshape(b * lp, a.shape[-1]) for a in xs]
        h = _proj_ffn(h, xs, mp["w_out"], layer["ffn1"])
        h = _ffn(h, layers[i + 1]["ffn0"])


def kernel(x_prompt, x_sample, meta_tokens, ffn_norm, ffn_w_gate, ffn_w_up, ffn_w_down, mix_norm, even_w_in, ret_decay_f, ret_decay_b, ret_out_norm, mla_q_norm, mla_w_qb, mla_kv_norm, mla_w_kvb, mla_qk_norm_q, mla_qk_norm_k, even_w_out, odd_w_in, swa_q_norm, swa_k_norm, swa_sink, odd_w_out):
    depth = ffn_norm.shape[0]
    assert x_prompt.shape[1] == x_sample.shape[1]
    lp = x_prompt.shape[1] + BLOCK
    layers = []
    for layer in range(depth):
        i = layer // 2
        ffn = [_prep_ffn(ffn_norm[layer, s], ffn_w_gate[layer, s], ffn_w_up[layer, s], ffn_w_down[layer, s])
               for s in range(2)]
        if layer % 2 == 0:
            mix = _prep_even(lp, mix_norm[layer], even_w_in[i], ret_decay_f[i], ret_decay_b[i],
                             ret_out_norm[i], mla_q_norm[i], mla_w_qb[i], mla_kv_norm[i], mla_w_kvb[i],
                             mla_qk_norm_q[i], mla_qk_norm_k[i], even_w_out[i])
        else:
            mix = _prep_odd(lp, mix_norm[layer], odd_w_in[i], swa_q_norm[i], swa_k_norm[i],
                            swa_sink[i], odd_w_out[i])
        layers.append(dict(even=layer % 2 == 0, ffn0=ffn[0], ffn1=ffn[1], mix=mix))
    return (_trunk(x_prompt, meta_tokens, layers), _trunk(x_sample, meta_tokens, layers))
```

```python
import functools

import jax
import jax.numpy as jnp
import numpy as np
from jax import lax
from jax.experimental import pallas as pl
from jax.experimental.pallas import tpu as pltpu

F32 = jnp.float32
BF16 = jnp.bfloat16

D_MODEL = 1024
N_META = 16
BLOCK = 128
FRONT_PAD = BLOCK - N_META
EPS = 1e-6
NEG = -1e30
LOG2E = 1.4426950408889634
LANES = 128
HALF = LANES // 2

A_HEADS = 8
A_DK = 64
A_THETA = 10000.0
B_HEADS = 8
B_Q_LORA = 256
B_KV_LORA = 128
B_NOPE = 64
B_ROPE = 32
B_DV = 64
B_QK = B_NOPE + B_ROPE
B_THETA = 10000.0
C_HEADS = 16
C_KV_HEADS = 4
C_DH = 64
C_ROT = 16
C_THETA = 500000.0
C_WINDOW = 128

VMEM_LIMIT = 56 * 1024 * 1024
MXU_TILE = 256

FFN_TILES = (1024, 512, 384, 256, 128)
FFN_CHUNK = MXU_TILE
SEQ_TILES = (384, 128)
SEQS_PER_STEP = (2, 1)
RET_GROUPS = (33, 11, 3, 1)
SWA_BLOCKS = (11, 3, 1)
MLA_ROWS = 16
MLA_HEADS = (8, 4, 2)
MLA_LEAD = 3


def _params(*sem):
    return pltpu.CompilerParams(dimension_semantics=sem, vmem_limit_bytes=VMEM_LIMIT)


def _resident(shape):
    zeros = (0,) * len(shape)
    return pl.BlockSpec(shape, lambda *_: zeros, pipeline_mode=pl.Buffered(1))


def _rms_rows(x, gain):
    return x * lax.rsqrt(jnp.mean(x * x, axis=-1, keepdims=True) + EPS) * gain


def _dot(a, b):
    return jnp.dot(a, b, preferred_element_type=F32)


def _dot_nt(a, b):
    return lax.dot_general(a, b, (((1,), (1,)), ((), ())), preferred_element_type=F32)


def _is_pad_row(row, seq):
    return (row >= seq) & (row < seq + FRONT_PAD)


def _positions(lp):
    row = jnp.arange(lp, dtype=F32)
    seq = lp - BLOCK
    return jnp.where(row < seq, row + N_META, row - (seq + FRONT_PAD))


def _pick(n, options):
    for o in options:
        if n % o == 0:
            return o
    raise ValueError(f"no tile in {options} divides {n}")


def _ffn_core(x, g_ref, wg_ref, wu_ref, wd_ref):
    xn = _rms_rows(x, g_ref[...]).astype(BF16)
    acc = jnp.zeros_like(x)
    fc = wd_ref.shape[1]
    for c in range(wd_ref.shape[0]):
        gate = _dot(xn, wg_ref[:, c * fc:(c + 1) * fc])
        up = _dot(xn, wu_ref[:, c * fc:(c + 1) * fc])
        act = (gate * (1.0 / (1.0 + jnp.exp(-gate))) * up).astype(BF16)
        acc = acc + _dot(act, wd_ref[c])
    return x + 0.5 * acc


def _ffn_body(h_ref, g_ref, wg_ref, wu_ref, wd_ref, o_ref):
    o_ref[...] = _ffn_core(h_ref[...], g_ref, wg_ref, wu_ref, wd_ref)


def _ffn_head_body(x_ref, tail_ref, g_ref, wg_ref, wu_ref, wd_ref, o_ref, *, nt):
    t = pl.program_id(1)

    @pl.when(t < nt)
    def _():
        o_ref[0] = _ffn_core(x_ref[0], g_ref, wg_ref, wu_ref, wd_ref)

    @pl.when(t == nt)
    def _():
        o_ref[0, :BLOCK, :] = tail_ref[...]


def _proj_ffn_body(*refs, n_in):
    tile = lambda r: r[...] if len(r.shape) == 2 else r[0]
    h_ref = refs[0]
    xs = refs[1:1 + n_in]
    ws = refs[1 + n_in:1 + 2 * n_in]
    g_ref, wg_ref, wu_ref, wd_ref, o_ref = refs[1 + 2 * n_in:]
    h2 = tile(h_ref)
    for x_ref, w_ref in zip(xs, ws):
        h2 = h2 + _dot(tile(x_ref), w_ref[...])
    out = _ffn_core(h2, g_ref, wg_ref, wu_ref, wd_ref)
    if len(o_ref.shape) == 2:
        o_ref[...] = out
    else:
        o_ref[0] = out


def _ffn_specs(ffn):
    return [_resident(a.shape) for a in ffn]


def _ffn(h2d, ffn):
    t, d = h2d.shape
    tm = _pick(t, FFN_TILES)
    row = pl.BlockSpec((tm, d), lambda i: (i, 0))
    return pl.pallas_call(
        _ffn_body,
        grid=(t // tm,),
        in_specs=[row] + _ffn_specs(ffn),
        out_specs=row,
        out_shape=jax.ShapeDtypeStruct((t, d), F32),
        compiler_params=_params("parallel"),
        name="ffn",
    )(h2d, *ffn)


def _ffn_head(x, tail, ffn):
    b, seq, d = x.shape
    tm = _pick(seq, FFN_TILES)
    nt = seq // tm
    return pl.pallas_call(
        functools.partial(_ffn_head_body, nt=nt),
        grid=(b, nt + 1),
        in_specs=[pl.BlockSpec((1, tm, d), lambda i, t: (i, jnp.minimum(t, nt - 1), 0)),
                  pl.BlockSpec((BLOCK, d), lambda i, t: (0, 0))] + _ffn_specs(ffn),
        out_specs=pl.BlockSpec((1, tm, d), lambda i, t: (i, t, 0)),
        out_shape=jax.ShapeDtypeStruct((b, seq + BLOCK, d), F32),
        compiler_params=_params("parallel", "arbitrary"),
        name="ffn_head",
    )(x, tail, *ffn)


def _proj_ffn_tail(h, xs, ws, ffn):
    b, lp, d = h.shape
    seq = lp - BLOCK
    tm = _pick(seq, FFN_TILES)
    tok = lambda w: pl.BlockSpec((1, tm, w), lambda i, t: (i, t, 0))
    return pl.pallas_call(
        functools.partial(_proj_ffn_body, n_in=len(xs)),
        grid=(b, seq // tm),
        in_specs=[tok(d)] + [tok(x.shape[-1]) for x in xs] + [_resident(w.shape) for w in ws]
        + _ffn_specs(ffn),
        out_specs=tok(d),
        out_shape=jax.ShapeDtypeStruct((b, seq, d), F32),
        compiler_params=_params("parallel", "parallel"),
        name="proj_ffn_tail",
    )(h, *xs, *ws, *ffn)


def _proj_ffn(h2d, xs, ws, ffn):
    t, d = h2d.shape
    tm = _pick(t, FFN_TILES)
    row = pl.BlockSpec((tm, d), lambda i: (i, 0))
    x_specs = [pl.BlockSpec((tm, x.shape[1]), lambda i: (i, 0)) for x in xs]
    w_specs = [_resident(w.shape) for w in ws]
    return pl.pallas_call(
        functools.partial(_proj_ffn_body, n_in=len(xs)),
        grid=(t // tm,),
        in_specs=[row] + x_specs + w_specs + _ffn_specs(ffn),
        out_specs=row,
        out_shape=jax.ShapeDtypeStruct((t, d), F32),
        compiler_params=_params("parallel"),
        name="proj_ffn",
    )(h2d, *xs, *ws, *ffn)


def _even_in_body(h_ref, g_ref, win_ref, qn_ref, wqb_ref, wqr_ref, kvn_ref, wkb_ref, wvb_ref,
                  gq_ref, gk_ref, rtab_ref, mtab_ref,
                  rq_ref, rk_ref, rv_ref, rg_ref, mq_ref, mk_ref, mv_ref, *, tm, seq):
    ns = h_ref.shape[0]
    o = B_Q_LORA + B_KV_LORA
    for s in range(ns):
        z = _dot(_rms_rows(h_ref[s], g_ref[...]).astype(BF16), win_ref[...])
        cq = _rms_rows(z[:, :B_Q_LORA], qn_ref[...]).astype(BF16)
        ckv = _rms_rows(z[:, B_Q_LORA:o], kvn_ref[...]).astype(BF16)
        q_all = _dot(cq, wqb_ref[...])
        q_rot = _dot(cq, wqr_ref[...])
        k_all = _dot(ckv, wkb_ref[...])
        v_all = _dot(ckv, wvb_ref[...])
        _even_in_tail(s, z, q_all, q_rot, k_all, v_all, gq_ref, gk_ref, rtab_ref, mtab_ref,
                      rq_ref, rk_ref, rv_ref, rg_ref, mq_ref, mk_ref, mv_ref, tm=tm, seq=seq)


def _even_in_tail(s, z, q_all, q_rot, k_all, v_all, gq_ref, gk_ref, rtab_ref, mtab_ref,
                  rq_ref, rk_ref, rv_ref, rg_ref, mq_ref, mk_ref, mv_ref, *, tm, seq):
    o = B_Q_LORA + B_KV_LORA
    kr = z[:, o:o + LANES]
    kr_rot = z[:, o + LANES:o + 2 * LANES]
    mcos, msin = mtab_ref[0], mtab_ref[1]
    lane = lax.broadcasted_iota(jnp.int32, (tm, LANES), 1)
    v_ones = jnp.where(lane >= HALF, 1.0, 0.0)
    row = lax.broadcasted_iota(jnp.int32, (tm, LANES), 0) + pl.program_id(1) * tm
    bias_lane = lane == B_QK
    q_bias = jnp.where(bias_lane, 1.0, 0.0)
    k_bias = jnp.where(bias_lane & _is_pad_row(row, seq), NEG, 0.0)
    inv = 1.0 / B_QK
    qcos, qsin = mcos * gq_ref[0:1], msin * gq_ref[1:2]
    kcos, ksin = mcos * gk_ref[0:1], msin * gk_ref[1:2]
    kr_sin = kr_rot * ksin
    heads = [slice(hd * LANES, (hd + 1) * LANES) for hd in range(B_HEADS)]
    qhs = [q_all[:, sl] for sl in heads]
    khs = [k_all[:, sl] + kr for sl in heads]
    rqs = [lax.rsqrt(jnp.sum(qh * qh, axis=-1, keepdims=True) * inv + EPS) for qh in qhs]
    rks = [lax.rsqrt(jnp.sum(kh * kh, axis=-1, keepdims=True) * inv + EPS) for kh in khs]
    for sl, qh, kh, rq, rk in zip(heads, qhs, khs, rqs, rks):
        mq_ref[s, :, sl] = ((qh * qcos + q_rot[:, sl] * qsin) * rq + q_bias).astype(BF16)
        mk_ref[s, :, sl] = ((kh * kcos + kr_sin) * rk + k_bias).astype(BF16)
        mv_ref[s, :, sl] = (v_all[:, sl] + v_ones).astype(BF16)

    hw = A_HEADS * A_DK
    o += 2 * LANES
    rcos, rsin = rtab_ref[0], rtab_ref[1]
    for c in range(hw // LANES):
        sl = slice(c * LANES, (c + 1) * LANES)
        q = z[:, o + c * LANES:o + (c + 1) * LANES]
        k = z[:, o + hw + c * LANES:o + hw + (c + 1) * LANES]
        rq_ref[s, :, sl] = (q * rcos + pltpu.roll(q, HALF, 1) * rsin).astype(BF16)
        rk_ref[s, :, sl] = ((k * rcos + pltpu.roll(k, HALF, 1) * rsin) * (A_DK ** -0.5)).astype(BF16)
    rv_ref[s] = z[:, o + 2 * hw:o + 3 * hw].astype(BF16)
    ga = z[:, o + 3 * hw:o + 4 * hw]
    rg_ref[s] = (ga * (1.0 / (1.0 + jnp.exp(-ga)))).astype(BF16)


def _even_in(h, p):
    b, lp, d = h.shape
    tm = _pick(lp, SEQ_TILES)
    ns = _pick(b, SEQS_PER_STEP)
    tok = lambda w: pl.BlockSpec((ns, tm, w), lambda i, j: (i, j, 0))
    tab = pl.BlockSpec((2, tm, LANES), lambda i, j: (0, j, 0))
    consts = [p["mix_g"], p["w_in"], p["q_norm"], p["w_qb"], p["w_qr"], p["kv_norm"], p["w_kb"], p["w_vb"],
              p["gq"], p["gk"]]
    hw = A_HEADS * A_DK
    widths = [hw, hw, hw, hw, B_HEADS * LANES, B_HEADS * LANES, B_HEADS * LANES]
    return pl.pallas_call(
        functools.partial(_even_in_body, tm=tm, seq=lp - BLOCK),
        grid=(b // ns, lp // tm),
        in_specs=[tok(d)] + [_resident(c.shape) for c in consts] + [tab, tab],
        out_specs=[tok(w) for w in widths],
        out_shape=[jax.ShapeDtypeStruct((b, lp, w), BF16) for w in widths],
        compiler_params=_params("parallel", "parallel"),
        name="even_in",
    )(h, *consts, p["ret_tab"], p["mla_tab"])


def _ret_body(q_ref, k_ref, v_ref, g_ref, dmat_ref, vec_ref, gain_ref, o_ref, acc_ref, kv_ref, st_ref,
              *, nb):
    lane = lax.broadcasted_iota(jnp.int32, (BLOCK, LANES), 1)
    first = lane < HALF
    wkf, wkb, qf, qb = vec_ref[0, 0], vec_ref[0, 1], vec_ref[0, 2], vec_ref[0, 3]
    cf, cb = vec_ref[0, 4][:1], vec_ref[0, 5][:1]
    dcat = jnp.concatenate([dmat_ref[0], dmat_ref[1]], axis=0)
    qk_first = lane % HALF < HALF // 2
    keep0 = jnp.where(qk_first, 1.0, 0.0).astype(BF16)
    keep1 = jnp.where(qk_first, 0.0, 1.0).astype(BF16)
    row2 = lax.broadcasted_iota(jnp.int32, (2 * LANES, LANES), 0) % HALF
    col2 = lax.broadcasted_iota(jnp.int32, (2 * LANES, LANES), 1)
    same2 = (row2 < HALF // 2) == (col2 < HALF)
    unroll = _pick(nb, RET_GROUPS)

    def intra(grp, carry):
        ns = [grp * unroll + u for u in range(unroll)]
        rows = [pl.ds(pl.multiple_of(n * BLOCK, BLOCK), BLOCK) for n in ns]
        qs = [q_ref[0, r, :] for r in rows]
        ks = [k_ref[0, r, :] for r in rows]
        vs = [v_ref[0, r, :] for r in rows]
        ss = [_dot_nt(jnp.concatenate([q * keep0, q * keep1], axis=0), k) for q, k in zip(qs, ks)]
        kws = [jnp.concatenate([k.astype(F32) * wkf, k.astype(F32) * wkb], axis=1).T.astype(BF16)
               for k in ks]
        kvs = [_dot(kw, v) for kw, v in zip(kws, vs)]
        ss = [(s * dcat).astype(BF16) for s in ss]
        os = [_dot(s, v) for s, v in zip(ss, vs)]
        for n, r, kv, o2 in zip(ns, rows, kvs, os):
            kv_ref[n] = jnp.where(same2, kv, 0.0)
            acc_ref[r, :] = jnp.where(first, o2[:BLOCK], o2[BLOCK:])
        return carry

    lax.fori_loop(0, nb // unroll, intra, 0)

    def scan(t, carry):
        sf, sb = carry
        i = lax.rem(t + nb - 1, nb)
        j = lax.rem(2 * nb - 2 - t, nb)
        st_ref[i, :LANES, :] = sf.astype(BF16)
        st_ref[j, LANES:, :] = sb.astype(BF16)
        return sf * cf + kv_ref[i, :LANES, :], sb * cb + kv_ref[j, LANES:, :]

    zero = jnp.zeros((LANES, LANES), F32)
    lax.fori_loop(0, nb, scan, (zero, zero))

    def inter(n, carry):
        rows = pl.ds(pl.multiple_of(n * BLOCK, BLOCK), BLOCK)
        qf32 = q_ref[0, rows, :].astype(F32)
        qq = jnp.concatenate([qf32 * qf, qf32 * qb], axis=1).astype(BF16)
        o = acc_ref[rows, :] + _dot(qq, st_ref[n])
        sq = o * o
        inv = 1.0 / HALF
        r0 = lax.rsqrt(jnp.sum(jnp.where(first, sq, 0.0), axis=-1, keepdims=True) * inv + EPS)
        r1 = lax.rsqrt(jnp.sum(jnp.where(first, 0.0, sq), axis=-1, keepdims=True) * inv + EPS)
        out = o * jnp.where(first, r0, r1) * gain_ref[0] * g_ref[0, rows, :].astype(F32)
        o_ref[0, rows, :] = out.astype(BF16)
        return carry

    lax.fori_loop(0, nb, inter, 0, unroll=unroll)


def _retention(rq, rk, rv, rg, p):
    b, lp, hw = rq.shape
    pairs = hw // LANES
    nb = lp // BLOCK
    seq = pl.BlockSpec((1, lp, LANES), lambda i, j: (i, 0, j))
    return pl.pallas_call(
        functools.partial(_ret_body, nb=nb),
        grid=(b, pairs),
        in_specs=[seq, seq, seq, seq,
                  pl.BlockSpec((2, BLOCK, BLOCK), lambda i, j: (j, 0, 0)),
                  pl.BlockSpec((1, 6, BLOCK, LANES), lambda i, j: (j, 0, 0, 0)),
                  pl.BlockSpec((1, 1, LANES), lambda i, j: (j, 0, 0))],
        out_specs=seq,
        out_shape=jax.ShapeDtypeStruct((b, lp, hw), BF16),
        scratch_shapes=[pltpu.VMEM((lp, LANES), F32), pltpu.VMEM((nb, 2 * LANES, LANES), F32),
                        pltpu.VMEM((nb, 2 * LANES, LANES), BF16)],
        compiler_params=_params("parallel", "parallel"),
        name="retention",
    )(rq, rk, rv, rg, p["ret_dmat"], p["ret_vec"], p["ret_gain"])


def _mla_body(q_ref, k_ref, v_ref, o_ref, *, tq, nh):
    lp = k_ref.shape[1]
    chunks = [(c, min(MXU_TILE, lp - c)) for c in range(0, lp, MXU_TILE)]
    hk = lp // 2 // MXU_TILE * MXU_TILE

    def scores(hd, part):
        q = q_ref[0, :, hd * LANES:(hd + 1) * LANES]
        return [_dot_nt(q, k_ref[0, c:c + w, hd * LANES:(hd + 1) * LANES]) for c, w in part]

    def probs(score_chunks):
        blocks = []
        for r in range(0, tq, MLA_ROWS):
            sb = jnp.concatenate([c[r:r + MLA_ROWS, :] for c in score_chunks], axis=1)
            m = jnp.max(sb, axis=-1, keepdims=True)
            blocks.append(jnp.exp2(sb - m).astype(BF16))
        return jnp.concatenate(blocks, axis=0)

    def weighted(hd, p):
        v = v_ref[0, :, hd * LANES:(hd + 1) * LANES]
        if hk == 0:
            return _dot(p, v)
        return _dot(p[:, :hk], v[:hk]) + _dot(p[:, hk:], v[hk:])

    lead = chunks[:MLA_LEAD]
    outs = []
    sc = scores(0, chunks)
    for hd in range(nh):
        nxt = scores(hd + 1, lead) if hd + 1 < nh else []
        outs.append(weighted(hd, probs(sc)))
        sc = nxt + scores(hd + 1, chunks[MLA_LEAD:]) if hd + 1 < nh else []
    lane = lax.broadcasted_iota(jnp.int32, (tq, LANES), 1)
    row = lax.broadcasted_iota(jnp.int32, (tq, LANES), 0) + pl.program_id(2) * tq
    pad = _is_pad_row(row, lp - BLOCK)
    for j in range(nh // 2):
        oa, ob = outs[2 * j], outs[2 * j + 1]
        out = jnp.where(lane < HALF, oa * pltpu.roll(1.0 / oa, HALF, 1),
                        pltpu.roll(ob, HALF, 1) * (1.0 / ob))
        o_ref[0, :, j * LANES:(j + 1) * LANES] = jnp.where(pad, 0.0, out).astype(BF16)


def _mla(mq, mk, mv):
    b, lp, _ = mq.shape
    nh = _pick(B_HEADS, MLA_HEADS)
    tq = _pick(lp, SEQ_TILES)
    return pl.pallas_call(
        functools.partial(_mla_body, tq=tq, nh=nh),
        grid=(b, B_HEADS // nh, lp // tq),
        in_specs=[pl.BlockSpec((1, tq, nh * LANES), lambda i, j, t: (i, t, j)),
                  pl.BlockSpec((1, lp, nh * LANES), lambda i, j, t: (i, 0, j)),
                  pl.BlockSpec((1, lp, nh * LANES), lambda i, j, t: (i, 0, j))],
        out_specs=pl.BlockSpec((1, tq, nh * B_DV), lambda i, j, t: (i, t, j)),
        out_shape=jax.ShapeDtypeStruct((b, lp, B_HEADS * B_DV), BF16),
        compiler_params=_params("parallel", "parallel", "parallel"),
        name="mla",
    )(mq, mk, mv)


def _odd_in_body(h_ref, g_ref, win_ref, gq_ref, gk_ref, tab_ref, q_ref, k_ref, vt_ref, *, tm):
    cos, sin = tab_ref[0], tab_ref[1]
    is_a = lax.broadcasted_iota(jnp.int32, (tm, LANES), 1) % HALF < HALF // 2
    nq = C_HEADS * C_DH
    nk = 2 * C_KV_HEADS * C_DH
    inv = 1.0 / C_DH
    rope = lambda y: y * cos + pltpu.roll(y, HALF, 1) * sin
    zs = [_dot(_rms_rows(h_ref[s], g_ref[...]).astype(BF16), win_ref[...]) for s in range(h_ref.shape[0])]
    for s, z in enumerate(zs):
        xqs = [z[:, c * LANES:(c + 1) * LANES] for c in range(nq // LANES)]
        xks = [z[:, nq + c * LANES:nq + (c + 1) * LANES] for c in range(nk // LANES)]
        rqs = []
        for xq in xqs:
            sq = xq * xq
            ra = lax.rsqrt(jnp.sum(jnp.where(is_a, sq, 0.0), axis=-1, keepdims=True) * inv + EPS)
            rb = lax.rsqrt(jnp.sum(jnp.where(is_a, 0.0, sq), axis=-1, keepdims=True) * inv + EPS)
            rqs.append(jnp.where(is_a, ra, rb))
        rks = [lax.rsqrt(jnp.sum(xk * xk, axis=-1, keepdims=True) * (0.5 * inv) + EPS) for xk in xks]
        for c, (xq, r) in enumerate(zip(xqs, rqs)):
            q_ref[s, :, c * LANES:(c + 1) * LANES] = rope(xq * r * gq_ref[...]).astype(BF16)
        for c, (xk, r) in enumerate(zip(xks, rks)):
            k_ref[s, :, c * LANES:(c + 1) * LANES] = rope(xk * r * gk_ref[...]).astype(BF16)
        vt_ref[s] = z[:, nq + nk:].T.astype(BF16)


def _odd_in(h, p):
    b, lp, d = h.shape
    tm = _pick(lp, SEQ_TILES)
    ns = _pick(b, SEQS_PER_STEP)
    tok = lambda w: pl.BlockSpec((ns, tm, w), lambda i, j: (i, j, 0))
    tab = pl.BlockSpec((2, tm, LANES), lambda i, j: (0, j, 0))
    consts = [p["mix_g"], p["w_in"], p["gq"], p["gk"]]
    nq, nk, nv = C_HEADS * C_DH, 2 * C_KV_HEADS * C_DH, C_KV_HEADS * C_DH
    return pl.pallas_call(
        functools.partial(_odd_in_body, tm=tm),
        grid=(b // ns, lp // tm),
        in_specs=[tok(d)] + [_resident(c.shape) for c in consts] + [tab],
        out_specs=[tok(nq), tok(nk), pl.BlockSpec((ns, nv, tm), lambda i, j: (i, 0, j))],
        out_shape=[jax.ShapeDtypeStruct((b, lp, nq), BF16), jax.ShapeDtypeStruct((b, lp, nk), BF16),
                   jax.ShapeDtypeStruct((b, nv, lp), BF16)],
        compiler_params=_params("parallel", "parallel"),
        name="odd_in",
    )(h, *consts, p["swa_tab"])


def _swa_body(sink_ref, q_ref, *refs, nb, nq):
    km_ref, k_refs = refs[0], refs[1:nq + 3]
    vm_ref, v_refs = refs[nq + 3], refs[nq + 4:2 * nq + 6]
    o_ref = refs[-1]
    nband = 3 * BLOCK
    group = C_HEADS // C_KV_HEADS
    kk = lax.broadcasted_iota(jnp.int32, (nband, BLOCK), 0)
    qq = lax.broadcasted_iota(jnp.int32, (nband, BLOCK), 1)
    lane = lax.broadcasted_iota(jnp.int32, (BLOCK, LANES), 1)
    rowi = lax.broadcasted_iota(jnp.int32, (BLOCK, LANES), 0)
    is_a = lane % HALF < HALF // 2
    keep = (jnp.where(is_a, 1.0, 0.0).astype(BF16), jnp.where(is_a, 0.0, 1.0).astype(BF16))
    ones = jnp.ones((C_DH, BLOCK + nband), BF16)
    no_pad_keys = jnp.zeros((FRONT_PAD, BLOCK), BF16)
    seq_len = nb * BLOCK - FRONT_PAD
    masks, rows_ok, meta_scores, band_scores = [], [], [], []
    for i in range(nq):
        stored = pl.program_id(1) * nq + i
        n = jnp.where(stored == nb - 1, 0, stored + 1)
        q_pos = n * BLOCK + qq - FRONT_PAD
        k_pos = (n - 1) * BLOCK + kk - FRONT_PAD
        masks.append((k_pos >= N_META) & (k_pos < seq_len) & (jnp.abs(q_pos - k_pos) <= C_WINDOW))
        rows_ok.append(rowi + n * BLOCK >= FRONT_PAD)
        rows = slice(i * BLOCK, (i + 1) * BLOCK)
        for kv in range(C_KV_HEADS):
            kvs = slice(kv * LANES, (kv + 1) * LANES)
            kcat = jnp.concatenate([r[0, :, kvs] for r in (km_ref, k_refs[i], k_refs[i + 1], k_refs[i + 2])],
                                   axis=0)
            pairs = [q_ref[0, rows, (kv * group // 2 + j) * LANES:(kv * group // 2 + j + 1) * LANES]
                     for j in range(group // 2)]
            qs = jnp.concatenate([p2 * keep[w] for p2 in pairs for w in range(2)], axis=0)
            st = _dot_nt(kcat, qs)
            meta_scores.append(st[:N_META])
            band_scores.append(st[N_META:])
    for i in range(nq):
        rows = slice(i * BLOCK, (i + 1) * BLOCK)
        for kv in range(C_KV_HEADS):
            sm_all, sb_all = meta_scores[i * C_KV_HEADS + kv], band_scores[i * C_KV_HEADS + kv]
            vt = jnp.concatenate([r[0, kv * C_DH:(kv + 1) * C_DH, :]
                                  for r in (vm_ref, v_refs[i], v_refs[i + 1], v_refs[i + 2])], axis=1)
            vt = jnp.concatenate([vt, ones], axis=0)
            normed = []
            for g in range(group):
                sink = sink_ref[kv * group + g] * LOG2E
                sm = sm_all[:, g * BLOCK:(g + 1) * BLOCK]
                sb = jnp.where(masks[i], sb_all[:, g * BLOCK:(g + 1) * BLOCK], NEG)
                m = jnp.maximum(jnp.maximum(jnp.max(sm, axis=0, keepdims=True),
                                            jnp.max(sb, axis=0, keepdims=True)), sink)
                pt = jnp.concatenate([no_pad_keys, jnp.exp2(sm - m).astype(BF16),
                                      jnp.exp2(sb - m).astype(BF16)], axis=0)
                ot = _dot(vt, pt)
                den = ot[C_DH:C_DH + 1, :] + jnp.exp2(sink - m)
                normed.append(ot[:C_DH, :] * (1.0 / den))
            for j in range(group // 2):
                slab = jnp.concatenate([normed[2 * j], normed[2 * j + 1]], axis=0).T
                sl = slice((kv * group // 2 + j) * LANES, (kv * group // 2 + j + 1) * LANES)
                o_ref[0, rows, sl] = jnp.where(rows_ok[i], slab, 0.0).astype(BF16)


def _swa(q, k, vt, sink):
    b, lp, qw = q.shape
    nb = lp // BLOCK
    kw = k.shape[-1]
    vw = vt.shape[1]
    nq = _pick(nb, SWA_BLOCKS)
    meta = nb - 1

    def slot(t):
        def index(n):
            raw = n * nq - 1 + t
            return jnp.where((raw < 0) | (raw >= nb), 0, raw)
        return index

    slots = [slot(t) for t in range(nq + 2)]
    qspec = pl.BlockSpec((1, nq * BLOCK, qw), lambda i, n: (i, n, 0))
    kspecs = [pl.BlockSpec((1, N_META, kw), lambda i, n: (i, lp // N_META - 1, 0))]
    kspecs += [pl.BlockSpec((1, BLOCK, kw), lambda i, n, f=f: (i, f(n), 0)) for f in slots]
    vspecs = [pl.BlockSpec((1, vw, BLOCK), lambda i, n: (i, 0, meta))]
    vspecs += [pl.BlockSpec((1, vw, BLOCK), lambda i, n, f=f: (i, 0, f(n))) for f in slots]
    return pl.pallas_call(
        functools.partial(_swa_body, nb=nb, nq=nq),
        grid=(b, nb // nq),
        in_specs=[pl.BlockSpec(memory_space=pltpu.SMEM), qspec] + kspecs + vspecs,
        out_specs=qspec,
        out_shape=jax.ShapeDtypeStruct((b, lp, qw), BF16),
        compiler_params=_params("parallel", "parallel"),
        name="swa",
    )(sink, q, *([k] * (nq + 3)), *([vt] * (nq + 3)))


def _rope_table(lp, theta, rot, offset):
    half = rot // 2
    lane = np.arange(LANES) - offset
    in_lo = (lane >= 0) & (lane < half)
    in_hi = (lane >= half) & (lane < rot)
    idx = np.where(in_lo | in_hi, lane % half, 0)
    inv = theta ** (-jnp.asarray(idx, F32) * 2.0 / rot)
    ang = _positions(lp)[:, None] * inv[None, :]
    sin = jnp.sin(ang)
    return jnp.stack([jnp.where(in_lo | in_hi, jnp.cos(ang), 1.0),
                      jnp.where(in_lo, -sin, jnp.where(in_hi, sin, 0.0))])


def _split_rope_table(lp, theta, rot):
    lanes = np.arange(LANES)
    r = lanes % (HALF // 2)
    active = jnp.asarray(r < rot // 2)
    inv = theta ** (-jnp.asarray(np.where(r < rot // 2, r, 0), F32) * 2.0 / rot)
    ang = _positions(lp)[:, None] * inv[None, :]
    sign = jnp.where(jnp.asarray(lanes < HALF), -1.0, 1.0)
    return jnp.stack([jnp.where(active, jnp.cos(ang), 1.0), jnp.where(active, sign * jnp.sin(ang), 0.0)])


def _pair_lanes(x):
    h, n = x.shape
    return jnp.repeat(x.reshape(h // 2, 2, n).transpose(0, 2, 1), HALF, axis=2)


def _prep_ffn(gain, wg, wu, wd):
    d, f = wg.shape
    return (gain.reshape(1, d), wg.astype(BF16), wu.astype(BF16),
            wd.reshape(f // FFN_CHUNK, FFN_CHUNK, d).astype(BF16))


def _prep_even(lp, mix_g, w_in, dec_f, dec_b, ret_norm, q_norm, w_qb, kv_norm, w_kvb, gq, gk, w_out):
    d = w_in.shape[0]
    hw = A_HEADS * A_DK
    cut = 4 * hw + B_Q_LORA + B_KV_LORA
    lanes = np.arange(LANES)
    partner = lanes.copy()
    partner[B_NOPE:B_NOPE + B_ROPE // 2] += B_ROPE // 2
    partner[B_NOPE + B_ROPE // 2:B_QK] -= B_ROPE // 2
    is_rope = jnp.asarray((lanes >= B_NOPE) & (lanes < B_QK))
    rot = lambda w: jnp.where(is_rope, w[..., partner], 0.0)
    kr_cols = jnp.concatenate([jnp.zeros((d, B_NOPE), F32), w_in[:, cut:],
                               jnp.zeros((d, LANES - B_QK), F32)], axis=1)
    qk_perm = lambda w: w.reshape(d, A_HEADS // 2, 2, 2, A_DK // 2).transpose(0, 1, 3, 2, 4).reshape(d, hw)
    w_in2 = jnp.concatenate([w_in[:, 4 * hw:cut], kr_cols, rot(kr_cols),
                             qk_perm(w_in[:, :hw]), qk_perm(w_in[:, hw:2 * hw]), w_in[:, 2 * hw:4 * hw]],
                            axis=1).astype(BF16)
    w_qb2 = jnp.pad(w_qb.reshape(B_Q_LORA, B_HEADS, B_QK), ((0, 0), (0, 0), (0, LANES - B_QK)))
    w_kv3 = w_kvb.reshape(B_KV_LORA, B_HEADS, B_NOPE + B_DV)
    w_kb = jnp.pad(w_kv3[:, :, :B_NOPE], ((0, 0), (0, 0), (0, LANES - B_NOPE)))
    w_vb = jnp.pad(w_kv3[:, :, B_NOPE:], ((0, 0), (0, 0), (0, LANES - B_DV)))
    pad96 = lambda g: jnp.pad(g, (0, LANES - B_QK)).reshape(1, LANES)
    with_rot = lambda g: jnp.concatenate([g, rot(g)], axis=0)
    ret_tab = _split_rope_table(lp, A_THETA, A_DK)
    qk_lanes = lambda x: _pair_lanes(x).reshape(x.shape[0] // 2, x.shape[1], 2, 2, A_DK // 2
                                                ).transpose(0, 1, 3, 2, 4).reshape(-1, x.shape[1], LANES)

    lgf = -jnp.exp(dec_f.astype(F32))
    lgb = -jnp.exp(dec_b.astype(F32))
    idx = jnp.arange(BLOCK, dtype=F32)
    diff = idx[:, None] - idx[None, :]
    dmat = (jnp.where(diff >= 0, jnp.exp(lgf[:, None, None] * jnp.maximum(diff, 0.0)), 0.0)
            + jnp.where(diff < 0, jnp.exp(lgb[:, None, None] * jnp.maximum(-diff, 0.0)), 0.0))
    ones = jnp.ones((BLOCK,), F32)
    vec = jnp.stack([
        qk_lanes(jnp.exp(lgf[:, None] * (BLOCK - 1 - idx)[None, :])),
        qk_lanes(jnp.exp(lgb[:, None] * idx[None, :])),
        qk_lanes(jnp.exp(lgf[:, None] * (idx + 1.0)[None, :])),
        qk_lanes(jnp.exp(lgb[:, None] * (BLOCK - idx)[None, :])),
        _pair_lanes(jnp.exp(BLOCK * lgf)[:, None] * ones[None, :]),
        _pair_lanes(jnp.exp(BLOCK * lgb)[:, None] * ones[None, :]),
    ], axis=1)
    half = A_HEADS * A_DK
    return dict(
        mix_g=mix_g.reshape(1, d), w_in=w_in2,
        q_norm=q_norm.reshape(1, -1), w_qb=w_qb2.reshape(B_Q_LORA, -1).astype(BF16),
        w_qr=rot(w_qb2).reshape(B_Q_LORA, -1).astype(BF16),
        kv_norm=kv_norm.reshape(1, -1), w_kb=w_kb.reshape(B_KV_LORA, -1).astype(BF16),
        w_vb=w_vb.reshape(B_KV_LORA, -1).astype(BF16),
        gq=with_rot(pad96(gq) * (B_QK ** -0.5 * LOG2E)), gk=with_rot(pad96(gk)),
        ret_tab=ret_tab,
        mla_tab=_rope_table(lp, B_THETA, B_ROPE, B_NOPE),
        ret_dmat=dmat, ret_vec=vec,
        ret_gain=ret_norm.reshape(A_HEADS // 2, 1, LANES),
        w_out=(w_out[:half].astype(BF16), w_out[half:].astype(BF16)),
    )


def _prep_odd(lp, mix_g, w_in, gq, gk, sink, w_out):
    d = w_in.shape[0]
    nq = C_HEADS * C_DH
    nkv = C_KV_HEADS * C_DH
    qtr, hr = C_DH // 2, C_ROT // 2
    perm = np.concatenate([np.arange(hr), np.arange(C_ROT, C_ROT + qtr - hr),
                           np.arange(hr, C_ROT), np.arange(C_ROT + qtr - hr, C_DH)])
    split = lambda w, heads: w.reshape(-1, heads, C_DH)[:, :, perm].reshape(-1, heads, 2, qtr)
    q_cols = lambda w: split(w, C_HEADS).reshape(-1, C_HEADS // 2, 2, 2, qtr).transpose(0, 1, 3, 2, 4
                                                                                         ).reshape(-1, nq)
    k_cols = lambda w: jnp.repeat(split(w, C_KV_HEADS)[:, :, :, None, :], 2, axis=3).reshape(-1, 2 * nkv)
    w_in2 = jnp.concatenate([q_cols(w_in[:, :nq]), k_cols(w_in[:, nq:nq + nkv]), w_in[:, nq + nkv:]], axis=1)
    slab_gain = lambda g: jnp.repeat(g[perm].reshape(2, 1, qtr), 2, axis=1).reshape(1, LANES)
    return dict(
        mix_g=mix_g.reshape(1, d), w_in=w_in2.astype(BF16),
        gq=slab_gain(gq) * (C_DH ** -0.5 * LOG2E), gk=slab_gain(gk),
        swa_tab=_split_rope_table(lp, C_THETA, C_ROT),
        sink=sink.astype(F32), w_out=(w_out.astype(BF16),),
    )


def _trunk(x, meta, layers):
    b, seq, d = x.shape
    lp = seq + BLOCK
    tail = jnp.concatenate([jnp.zeros((FRONT_PAD, d), x.dtype), meta.astype(x.dtype)], axis=0)
    tail = _ffn(tail, layers[0]["ffn0"])
    h = _ffn_head(x, tail, layers[0]["ffn0"]).reshape(b * lp, d)
    for i, layer in enumerate(layers):
        h3 = h.reshape(b, lp, d)
        mp = layer["mix"]
        if layer["even"]:
            rq, rk, rv, rg, mq, mk, mv = _even_in(h3, mp)
            xs = [_retention(rq, rk, rv, rg, mp), _mla(mq, mk, mv)]
        else:
            q, k, v = _odd_in(h3, mp)
            xs = [_swa(q, k, v, mp["sink"])]
        if i + 1 == len(layers):
            return _proj_ffn_tail(h3, xs, mp["w_out"], layer["ffn1"])
        xs = [a.reshape(b * lp, a.shape[-1]) for a in xs]
        h = _proj_ffn(h, xs, mp["w_out"], layer["ffn1"])
        h = _ffn(h, layers[i + 1]["ffn0"])


def kernel(x_prompt, x_sample, meta_tokens, ffn_norm, ffn_w_gate, ffn_w_up, ffn_w_down, mix_norm, even_w_in, ret_decay_f, ret_decay_b, ret_out_norm, mla_q_norm, mla_w_qb, mla_kv_norm, mla_w_kvb, mla_qk_norm_q, mla_qk_norm_k, even_w_out, odd_w_in, swa_q_norm, swa_k_norm, swa_sink, odd_w_out):
    depth = ffn_norm.shape[0]
    assert x_prompt.shape[1] == x_sample.shape[1]
    lp = x_prompt.shape[1] + BLOCK
    layers = []
    for layer in range(depth):
        i = layer // 2
        ffn = [_prep_ffn(ffn_norm[layer, s], ffn_w_gate[layer, s], ffn_w_up[layer, s], ffn_w_down[layer, s])
               for s in range(2)]
        if layer % 2 == 0:
            mix = _prep_even(lp, mix_norm[layer], even_w_in[i], ret_decay_f[i], ret_decay_b[i],
                             ret_out_norm[i], mla_q_norm[i], mla_w_qb[i], mla_kv_norm[i], mla_w_kvb[i],
                             mla_qk_norm_q[i], mla_qk_norm_k[i], even_w_out[i])
        else:
            mix = _prep_odd(lp, mix_norm[layer], odd_w_in[i], swa_q_norm[i], swa_k_norm[i],
                            swa_sink[i], odd_w_out[i])
        layers.append(dict(even=layer % 2 == 0, ffn0=ffn[0], ffn1=ffn[1], mix=mix))
    return (_trunk(x_prompt, meta_tokens, layers), _trunk(x_sample, meta_tokens, layers))
```

```python
import functools

import jax
import jax.numpy as jnp
import numpy as np
from jax import lax
from jax.experimental import pallas as pl
from jax.experimental.pallas import tpu as pltpu

F32 = jnp.float32
BF16 = jnp.bfloat16

D_MODEL = 1024
N_META = 16
BLOCK = 128
FRONT_PAD = BLOCK - N_META
EPS = 1e-6
NEG = -1e30
LOG2E = 1.4426950408889634
LANES = 128
HALF = LANES // 2

A_HEADS = 8
A_DK = 64
A_THETA = 10000.0
B_HEADS = 8
B_Q_LORA = 256
B_KV_LORA = 128
B_NOPE = 64
B_ROPE = 32
B_DV = 64
B_QK = B_NOPE + B_ROPE
B_THETA = 10000.0
C_HEADS = 16
C_KV_HEADS = 4
C_DH = 64
C_ROT = 16
C_THETA = 500000.0
C_WINDOW = 128

VMEM_LIMIT = 56 * 1024 * 1024
MXU_TILE = 256

FFN_TILES = (1024, 512, 384, 256, 128)
FFN_CHUNK = MXU_TILE
SEQ_TILES = (384, 128)
SEQS_PER_STEP = (2, 1)
RET_GROUPS = (33, 11, 3, 1)
SWA_BLOCKS = (11, 3, 1)
SWA_AHEAD = 2
MLA_ROWS = 16
MLA_HEADS = (8, 4, 2)
MLA_LEAD = 2


def _params(*sem):
    return pltpu.CompilerParams(dimension_semantics=sem, vmem_limit_bytes=VMEM_LIMIT)


def _resident(shape):
    zeros = (0,) * len(shape)
    return pl.BlockSpec(shape, lambda *_: zeros, pipeline_mode=pl.Buffered(1))


def _rms_rows(x, gain):
    return x * lax.rsqrt(jnp.mean(x * x, axis=-1, keepdims=True) + EPS) * gain


def _dot(a, b):
    return jnp.dot(a, b, preferred_element_type=F32)


def _dot_nt(a, b):
    return lax.dot_general(a, b, (((1,), (1,)), ((), ())), preferred_element_type=F32)


def _is_pad_row(row, seq):
    return (row >= seq) & (row < seq + FRONT_PAD)


def _positions(lp):
    row = jnp.arange(lp, dtype=F32)
    seq = lp - BLOCK
    return jnp.where(row < seq, row + N_META, row - (seq + FRONT_PAD))


def _pick(n, options):
    for o in options:
        if n % o == 0:
            return o
    raise ValueError(f"no tile in {options} divides {n}")


def _ffn_core(x, g_ref, wg_ref, wu_ref, wd_ref):
    xn = _rms_rows(x, g_ref[...]).astype(BF16)
    acc = jnp.zeros_like(x)
    fc = wd_ref.shape[1]
    for c in range(wd_ref.shape[0]):
        gate = _dot(xn, wg_ref[:, c * fc:(c + 1) * fc])
        up = _dot(xn, wu_ref[:, c * fc:(c + 1) * fc])
        act = (gate * (1.0 / (1.0 + jnp.exp(-gate))) * up).astype(BF16)
        acc = acc + _dot(act, wd_ref[c])
    return x + 0.5 * acc


def _ffn_body(h_ref, g_ref, wg_ref, wu_ref, wd_ref, o_ref):
    o_ref[...] = _ffn_core(h_ref[...], g_ref, wg_ref, wu_ref, wd_ref)


def _ffn_head_body(x_ref, tail_ref, g_ref, wg_ref, wu_ref, wd_ref, o_ref, *, nt):
    t = pl.program_id(1)

    @pl.when(t < nt)
    def _():
        o_ref[0] = _ffn_core(x_ref[0], g_ref, wg_ref, wu_ref, wd_ref)

    @pl.when(t == nt)
    def _():
        o_ref[0, :BLOCK, :] = tail_ref[...]


def _proj_ffn_body(*refs, n_in):
    tile = lambda r: r[...] if len(r.shape) == 2 else r[0]
    h_ref = refs[0]
    xs = refs[1:1 + n_in]
    ws = refs[1 + n_in:1 + 2 * n_in]
    g_ref, wg_ref, wu_ref, wd_ref, o_ref = refs[1 + 2 * n_in:]
    h2 = tile(h_ref)
    for x_ref, w_ref in zip(xs, ws):
        h2 = h2 + _dot(tile(x_ref), w_ref[...])
    out = _ffn_core(h2, g_ref, wg_ref, wu_ref, wd_ref)
    if len(o_ref.shape) == 2:
        o_ref[...] = out
    else:
        o_ref[0] = out


def _ffn_specs(ffn):
    return [_resident(a.shape) for a in ffn]


def _ffn(h2d, ffn):
    t, d = h2d.shape
    tm = _pick(t, FFN_TILES)
    row = pl.BlockSpec((tm, d), lambda i: (i, 0))
    return pl.pallas_call(
        _ffn_body,
        grid=(t // tm,),
        in_specs=[row] + _ffn_specs(ffn),
        out_specs=row,
        out_shape=jax.ShapeDtypeStruct((t, d), F32),
        compiler_params=_params("parallel"),
        name="ffn",
    )(h2d, *ffn)


def _ffn_head(x, tail, ffn):
    b, seq, d = x.shape
    tm = _pick(seq, FFN_TILES)
    nt = seq // tm
    return pl.pallas_call(
        functools.partial(_ffn_head_body, nt=nt),
        grid=(b, nt + 1),
        in_specs=[pl.BlockSpec((1, tm, d), lambda i, t: (i, jnp.minimum(t, nt - 1), 0)),
                  pl.BlockSpec((BLOCK, d), lambda i, t: (0, 0))] + _ffn_specs(ffn),
        out_specs=pl.BlockSpec((1, tm, d), lambda i, t: (i, t, 0)),
        out_shape=jax.ShapeDtypeStruct((b, seq + BLOCK, d), F32),
        compiler_params=_params("parallel", "arbitrary"),
        name="ffn_head",
    )(x, tail, *ffn)


def _proj_ffn_tail(h, xs, ws, ffn):
    b, lp, d = h.shape
    seq = lp - BLOCK
    tm = _pick(seq, FFN_TILES)
    tok = lambda w: pl.BlockSpec((1, tm, w), lambda i, t: (i, t, 0))
    return pl.pallas_call(
        functools.partial(_proj_ffn_body, n_in=len(xs)),
        grid=(b, seq // tm),
        in_specs=[tok(d)] + [tok(x.shape[-1]) for x in xs] + [_resident(w.shape) for w in ws]
        + _ffn_specs(ffn),
        out_specs=tok(d),
        out_shape=jax.ShapeDtypeStruct((b, seq, d), F32),
        compiler_params=_params("parallel", "parallel"),
        name="proj_ffn_tail",
    )(h, *xs, *ws, *ffn)


def _proj_ffn(h2d, xs, ws, ffn):
    t, d = h2d.shape
    tm = _pick(t, FFN_TILES)
    row = pl.BlockSpec((tm, d), lambda i: (i, 0))
    x_specs = [pl.BlockSpec((tm, x.shape[1]), lambda i: (i, 0)) for x in xs]
    w_specs = [_resident(w.shape) for w in ws]
    return pl.pallas_call(
        functools.partial(_proj_ffn_body, n_in=len(xs)),
        grid=(t // tm,),
        in_specs=[row] + x_specs + w_specs + _ffn_specs(ffn),
        out_specs=row,
        out_shape=jax.ShapeDtypeStruct((t, d), F32),
        compiler_params=_params("parallel"),
        name="proj_ffn",
    )(h2d, *xs, *ws, *ffn)


def _even_in_body(h_ref, g_ref, win_ref, qn_ref, wqb_ref, wqr_ref, kvn_ref, wkb_ref, wvb_ref,
                  gq_ref, gk_ref, rtab_ref, mtab_ref,
                  rq_ref, rk_ref, rv_ref, rg_ref, mq_ref, mk_ref, mv_ref, *, tm, seq):
    ns = h_ref.shape[0]
    o = B_Q_LORA + B_KV_LORA
    for s in range(ns):
        z = _dot(_rms_rows(h_ref[s], g_ref[...]).astype(BF16), win_ref[...])
        cq = _rms_rows(z[:, :B_Q_LORA], qn_ref[...]).astype(BF16)
        ckv = _rms_rows(z[:, B_Q_LORA:o], kvn_ref[...]).astype(BF16)
        q_all = _dot(cq, wqb_ref[...])
        q_rot = _dot(cq, wqr_ref[...])
        k_all = _dot(ckv, wkb_ref[...])
        v_all = _dot(ckv, wvb_ref[...])
        _even_in_tail(s, z, q_all, q_rot, k_all, v_all, gq_ref, gk_ref, rtab_ref, mtab_ref,
                      rq_ref, rk_ref, rv_ref, rg_ref, mq_ref, mk_ref, mv_ref, tm=tm, seq=seq)


def _even_in_tail(s, z, q_all, q_rot, k_all, v_all, gq_ref, gk_ref, rtab_ref, mtab_ref,
                  rq_ref, rk_ref, rv_ref, rg_ref, mq_ref, mk_ref, mv_ref, *, tm, seq):
    o = B_Q_LORA + B_KV_LORA
    kr = z[:, o:o + LANES]
    kr_rot = z[:, o + LANES:o + 2 * LANES]
    mcos, msin = mtab_ref[0], mtab_ref[1]
    lane = lax.broadcasted_iota(jnp.int32, (tm, LANES), 1)
    v_ones = jnp.where(lane >= HALF, 1.0, 0.0)
    row = lax.broadcasted_iota(jnp.int32, (tm, LANES), 0) + pl.program_id(1) * tm
    bias_lane = lane == B_QK
    q_bias = jnp.where(bias_lane, 1.0, 0.0)
    k_bias = jnp.where(bias_lane & _is_pad_row(row, seq), NEG, 0.0)
    inv = 1.0 / B_QK
    qcos, qsin = mcos * gq_ref[0:1], msin * gq_ref[1:2]
    kcos, ksin = mcos * gk_ref[0:1], msin * gk_ref[1:2]
    kr_sin = kr_rot * ksin
    heads = [slice(hd * LANES, (hd + 1) * LANES) for hd in range(B_HEADS)]
    qhs = [q_all[:, sl] for sl in heads]
    khs = [k_all[:, sl] + kr for sl in heads]
    rqs = [lax.rsqrt(jnp.sum(qh * qh, axis=-1, keepdims=True) * inv + EPS) for qh in qhs]
    rks = [lax.rsqrt(jnp.sum(kh * kh, axis=-1, keepdims=True) * inv + EPS) for kh in khs]
    for sl, qh, kh, rq, rk in zip(heads, qhs, khs, rqs, rks):
        mq_ref[s, :, sl] = ((qh * qcos + q_rot[:, sl] * qsin) * rq + q_bias).astype(BF16)
        mk_ref[s, :, sl] = ((kh * kcos + kr_sin) * rk + k_bias).astype(BF16)
        mv_ref[s, :, sl] = (v_all[:, sl] + v_ones).astype(BF16)

    hw = A_HEADS * A_DK
    o += 2 * LANES
    rcos, rsin = rtab_ref[0], rtab_ref[1]
    for c in range(hw // LANES):
        sl = slice(c * LANES, (c + 1) * LANES)
        q = z[:, o + c * LANES:o + (c + 1) * LANES]
        k = z[:, o + hw + c * LANES:o + hw + (c + 1) * LANES]
        rq_ref[s, :, sl] = (q * rcos + pltpu.roll(q, HALF, 1) * rsin).astype(BF16)
        rk_ref[s, :, sl] = ((k * rcos + pltpu.roll(k, HALF, 1) * rsin) * (A_DK ** -0.5)).astype(BF16)
    rv_ref[s] = z[:, o + 2 * hw:o + 3 * hw].astype(BF16)
    ga = z[:, o + 3 * hw:o + 4 * hw]
    rg_ref[s] = (ga * (1.0 / (1.0 + jnp.exp(-ga)))).astype(BF16)


def _even_in(h, p):
    b, lp, d = h.shape
    tm = _pick(lp, SEQ_TILES)
    ns = _pick(b, SEQS_PER_STEP)
    tok = lambda w: pl.BlockSpec((ns, tm, w), lambda i, j: (i, j, 0))
    tab = pl.BlockSpec((2, tm, LANES), lambda i, j: (0, j, 0))
    consts = [p["mix_g"], p["w_in"], p["q_norm"], p["w_qb"], p["w_qr"], p["kv_norm"], p["w_kb"], p["w_vb"],
              p["gq"], p["gk"]]
    hw = A_HEADS * A_DK
    widths = [hw, hw, hw, hw, B_HEADS * LANES, B_HEADS * LANES, B_HEADS * LANES]
    return pl.pallas_call(
        functools.partial(_even_in_body, tm=tm, seq=lp - BLOCK),
        grid=(b // ns, lp // tm),
        in_specs=[tok(d)] + [_resident(c.shape) for c in consts] + [tab, tab],
        out_specs=[tok(w) for w in widths],
        out_shape=[jax.ShapeDtypeStruct((b, lp, w), BF16) for w in widths],
        compiler_params=_params("parallel", "parallel"),
        name="even_in",
    )(h, *consts, p["ret_tab"], p["mla_tab"])


def _ret_body(q_ref, k_ref, v_ref, g_ref, dmat_ref, vec_ref, gain_ref, o_ref, acc_ref, kv_ref, st_ref,
              *, nb):
    lane = lax.broadcasted_iota(jnp.int32, (BLOCK, LANES), 1)
    first = lane < HALF
    wkf, wkb, qf, qb = vec_ref[0, 0], vec_ref[0, 1], vec_ref[0, 2], vec_ref[0, 3]
    cf, cb = vec_ref[0, 4][:1], vec_ref[0, 5][:1]
    dcat = jnp.concatenate([dmat_ref[0], dmat_ref[1]], axis=0)
    qk_first = lane % HALF < HALF // 2
    keep0 = jnp.where(qk_first, 1.0, 0.0).astype(BF16)
    keep1 = jnp.where(qk_first, 0.0, 1.0).astype(BF16)
    row2 = lax.broadcasted_iota(jnp.int32, (2 * LANES, LANES), 0) % HALF
    col2 = lax.broadcasted_iota(jnp.int32, (2 * LANES, LANES), 1)
    same2 = (row2 < HALF // 2) == (col2 < HALF)
    unroll = _pick(nb, RET_GROUPS)

    def intra(grp, carry):
        ns = [grp * unroll + u for u in range(unroll)]
        rows = [pl.ds(pl.multiple_of(n * BLOCK, BLOCK), BLOCK) for n in ns]
        qs = [q_ref[0, r, :] for r in rows]
        ks = [k_ref[0, r, :] for r in rows]
        vs = [v_ref[0, r, :] for r in rows]
        ss = [_dot_nt(jnp.concatenate([q * keep0, q * keep1], axis=0), k) for q, k in zip(qs, ks)]
        kws = [jnp.concatenate([k.astype(F32) * wkf, k.astype(F32) * wkb], axis=1).T.astype(BF16)
               for k in ks]
        kvs = [_dot(kw, v) for kw, v in zip(kws, vs)]
        ss = [(s * dcat).astype(BF16) for s in ss]
        os = [_dot(s, v) for s, v in zip(ss, vs)]
        for n, r, kv, o2 in zip(ns, rows, kvs, os):
            kv_ref[n] = jnp.where(same2, kv, 0.0)
            acc_ref[r, :] = jnp.where(first, o2[:BLOCK], o2[BLOCK:])
        return carry

    lax.fori_loop(0, nb // unroll, intra, 0)

    def scan(t, carry):
        sf, sb = carry
        i = lax.rem(t + nb - 1, nb)
        j = lax.rem(2 * nb - 2 - t, nb)
        st_ref[i, :LANES, :] = sf.astype(BF16)
        st_ref[j, LANES:, :] = sb.astype(BF16)
        return sf * cf + kv_ref[i, :LANES, :], sb * cb + kv_ref[j, LANES:, :]

    zero = jnp.zeros((LANES, LANES), F32)
    lax.fori_loop(0, nb, scan, (zero, zero))

    def inter(n, carry):
        rows = pl.ds(pl.multiple_of(n * BLOCK, BLOCK), BLOCK)
        qf32 = q_ref[0, rows, :].astype(F32)
        qq = jnp.concatenate([qf32 * qf, qf32 * qb], axis=1).astype(BF16)
        o = acc_ref[rows, :] + _dot(qq, st_ref[n])
        sq = o * o
        inv = 1.0 / HALF
        r0 = lax.rsqrt(jnp.sum(jnp.where(first, sq, 0.0), axis=-1, keepdims=True) * inv + EPS)
        r1 = lax.rsqrt(jnp.sum(jnp.where(first, 0.0, sq), axis=-1, keepdims=True) * inv + EPS)
        out = o * jnp.where(first, r0, r1) * gain_ref[0] * g_ref[0, rows, :].astype(F32)
        o_ref[0, rows, :] = out.astype(BF16)
        return carry

    lax.fori_loop(0, nb, inter, 0, unroll=unroll)


def _retention(rq, rk, rv, rg, p):
    b, lp, hw = rq.shape
    pairs = hw // LANES
    nb = lp // BLOCK
    seq = pl.BlockSpec((1, lp, LANES), lambda i, j: (i, 0, j))
    return pl.pallas_call(
        functools.partial(_ret_body, nb=nb),
        grid=(b, pairs),
        in_specs=[seq, seq, seq, seq,
                  pl.BlockSpec((2, BLOCK, BLOCK), lambda i, j: (j, 0, 0)),
                  pl.BlockSpec((1, 6, BLOCK, LANES), lambda i, j: (j, 0, 0, 0)),
                  pl.BlockSpec((1, 1, LANES), lambda i, j: (j, 0, 0))],
        out_specs=seq,
        out_shape=jax.ShapeDtypeStruct((b, lp, hw), BF16),
        scratch_shapes=[pltpu.VMEM((lp, LANES), F32), pltpu.VMEM((nb, 2 * LANES, LANES), F32),
                        pltpu.VMEM((nb, 2 * LANES, LANES), BF16)],
        compiler_params=_params("parallel", "parallel"),
        name="retention",
    )(rq, rk, rv, rg, p["ret_dmat"], p["ret_vec"], p["ret_gain"])


def _mla_body(q_ref, k_ref, v_ref, o_ref, *, tq, nh):
    lp = k_ref.shape[1]
    chunks = [(c, min(MXU_TILE, lp - c)) for c in range(0, lp, MXU_TILE)]
    hk = lp // 2 // MXU_TILE * MXU_TILE

    def scores(hd, part):
        q = q_ref[0, :, hd * LANES:(hd + 1) * LANES]
        return [_dot_nt(q, k_ref[0, c:c + w, hd * LANES:(hd + 1) * LANES]) for c, w in part]

    def probs(score_chunks):
        blocks = []
        for r in range(0, tq, MLA_ROWS):
            sb = jnp.concatenate([c[r:r + MLA_ROWS, :] for c in score_chunks], axis=1)
            m = jnp.max(sb, axis=-1, keepdims=True)
            blocks.append(jnp.exp2(sb - m).astype(BF16))
        return jnp.concatenate(blocks, axis=0)

    def weighted(hd, p):
        v = v_ref[0, :, hd * LANES:(hd + 1) * LANES]
        if hk == 0:
            return _dot(p, v)
        return _dot(p[:, :hk], v[:hk]) + _dot(p[:, hk:], v[hk:])

    lead = chunks[:MLA_LEAD]
    outs = []
    sc = scores(0, chunks)
    for hd in range(nh):
        nxt = scores(hd + 1, lead) if hd + 1 < nh else []
        outs.append(weighted(hd, probs(sc)))
        sc = nxt + scores(hd + 1, chunks[MLA_LEAD:]) if hd + 1 < nh else []
    lane = lax.broadcasted_iota(jnp.int32, (tq, LANES), 1)
    row = lax.broadcasted_iota(jnp.int32, (tq, LANES), 0) + pl.program_id(2) * tq
    pad = _is_pad_row(row, lp - BLOCK)
    for j in range(nh // 2):
        oa, ob = outs[2 * j], outs[2 * j + 1]
        out = jnp.where(lane < HALF, oa * pltpu.roll(1.0 / oa, HALF, 1),
                        pltpu.roll(ob, HALF, 1) * (1.0 / ob))
        o_ref[0, :, j * LANES:(j + 1) * LANES] = jnp.where(pad, 0.0, out).astype(BF16)


def _mla(mq, mk, mv):
    b, lp, _ = mq.shape
    nh = _pick(B_HEADS, MLA_HEADS)
    tq = _pick(lp, SEQ_TILES)
    return pl.pallas_call(
        functools.partial(_mla_body, tq=tq, nh=nh),
        grid=(b, B_HEADS // nh, lp // tq),
        in_specs=[pl.BlockSpec((1, tq, nh * LANES), lambda i, j, t: (i, t, j)),
                  pl.BlockSpec((1, lp, nh * LANES), lambda i, j, t: (i, 0, j)),
                  pl.BlockSpec((1, lp, nh * LANES), lambda i, j, t: (i, 0, j))],
        out_specs=pl.BlockSpec((1, tq, nh * B_DV), lambda i, j, t: (i, t, j)),
        out_shape=jax.ShapeDtypeStruct((b, lp, B_HEADS * B_DV), BF16),
        compiler_params=_params("parallel", "parallel", "parallel"),
        name="mla",
    )(mq, mk, mv)


def _odd_in_body(h_ref, g_ref, win_ref, gq_ref, gk_ref, tab_ref, q_ref, k_ref, vt_ref, *, tm):
    cos, sin = tab_ref[0], tab_ref[1]
    is_a = lax.broadcasted_iota(jnp.int32, (tm, LANES), 1) % HALF < HALF // 2
    nq = C_HEADS * C_DH
    nk = 2 * C_KV_HEADS * C_DH
    inv = 1.0 / C_DH
    rope = lambda y: y * cos + pltpu.roll(y, HALF, 1) * sin
    zs = [_dot(_rms_rows(h_ref[s], g_ref[...]).astype(BF16), win_ref[...]) for s in range(h_ref.shape[0])]
    for s, z in enumerate(zs):
        xqs = [z[:, c * LANES:(c + 1) * LANES] for c in range(nq // LANES)]
        xks = [z[:, nq + c * LANES:nq + (c + 1) * LANES] for c in range(nk // LANES)]
        rqs = []
        for xq in xqs:
            sq = xq * xq
            ra = lax.rsqrt(jnp.sum(jnp.where(is_a, sq, 0.0), axis=-1, keepdims=True) * inv + EPS)
            rb = lax.rsqrt(jnp.sum(jnp.where(is_a, 0.0, sq), axis=-1, keepdims=True) * inv + EPS)
            rqs.append(jnp.where(is_a, ra, rb))
        rks = [lax.rsqrt(jnp.sum(xk * xk, axis=-1, keepdims=True) * (0.5 * inv) + EPS) for xk in xks]
        for c, (xq, r) in enumerate(zip(xqs, rqs)):
            q_ref[s, :, c * LANES:(c + 1) * LANES] = rope(xq * r * gq_ref[...]).astype(BF16)
        for c, (xk, r) in enumerate(zip(xks, rks)):
            k_ref[s, :, c * LANES:(c + 1) * LANES] = rope(xk * r * gk_ref[...]).astype(BF16)
        vt_ref[s] = z[:, nq + nk:].T.astype(BF16)


def _odd_in(h, p):
    b, lp, d = h.shape
    tm = _pick(lp, SEQ_TILES)
    ns = _pick(b, SEQS_PER_STEP)
    tok = lambda w: pl.BlockSpec((ns, tm, w), lambda i, j: (i, j, 0))
    tab = pl.BlockSpec((2, tm, LANES), lambda i, j: (0, j, 0))
    consts = [p["mix_g"], p["w_in"], p["gq"], p["gk"]]
    nq, nk, nv = C_HEADS * C_DH, 2 * C_KV_HEADS * C_DH, C_KV_HEADS * C_DH
    return pl.pallas_call(
        functools.partial(_odd_in_body, tm=tm),
        grid=(b // ns, lp // tm),
        in_specs=[tok(d)] + [_resident(c.shape) for c in consts] + [tab],
        out_specs=[tok(nq), tok(nk), pl.BlockSpec((ns, nv, tm), lambda i, j: (i, 0, j))],
        out_shape=[jax.ShapeDtypeStruct((b, lp, nq), BF16), jax.ShapeDtypeStruct((b, lp, nk), BF16),
                   jax.ShapeDtypeStruct((b, nv, lp), BF16)],
        compiler_params=_params("parallel", "parallel"),
        name="odd_in",
    )(h, *consts, p["swa_tab"])


def _swa_body(sink_ref, q_ref, *refs, nb, nq):
    km_ref, k_refs = refs[0], refs[1:nq + 3]
    vm_ref, v_refs = refs[nq + 3], refs[nq + 4:2 * nq + 6]
    o_ref = refs[-1]
    nband = 3 * BLOCK
    group = C_HEADS // C_KV_HEADS
    kk = lax.broadcasted_iota(jnp.int32, (nband, BLOCK), 0)
    qq = lax.broadcasted_iota(jnp.int32, (nband, BLOCK), 1)
    lane = lax.broadcasted_iota(jnp.int32, (BLOCK, LANES), 1)
    rowi = lax.broadcasted_iota(jnp.int32, (BLOCK, LANES), 0)
    is_a = lane % HALF < HALF // 2
    keep = (jnp.where(is_a, 1.0, 0.0).astype(BF16), jnp.where(is_a, 0.0, 1.0).astype(BF16))
    ones = jnp.ones((C_DH, BLOCK + nband), BF16)
    no_pad_keys = jnp.zeros((FRONT_PAD, BLOCK), BF16)
    seq_len = nb * BLOCK - FRONT_PAD
    masks, rows_ok = [], []
    for i in range(nq):
        stored = pl.program_id(1) * nq + i
        n = jnp.where(stored == nb - 1, 0, stored + 1)
        q_pos = n * BLOCK + qq - FRONT_PAD
        k_pos = (n - 1) * BLOCK + kk - FRONT_PAD
        masks.append((k_pos >= N_META) & (k_pos < seq_len) & (jnp.abs(q_pos - k_pos) <= C_WINDOW))
        rows_ok.append(rowi + n * BLOCK >= FRONT_PAD)

    def score_unit(i, kv):
        rows = slice(i * BLOCK, (i + 1) * BLOCK)
        kvs = slice(kv * LANES, (kv + 1) * LANES)
        kcat = jnp.concatenate([r[0, :, kvs] for r in (km_ref, k_refs[i], k_refs[i + 1], k_refs[i + 2])],
                               axis=0)
        pairs = [q_ref[0, rows, (kv * group // 2 + j) * LANES:(kv * group // 2 + j + 1) * LANES]
                 for j in range(group // 2)]
        qs = jnp.concatenate([p2 * keep[w] for p2 in pairs for w in range(2)], axis=0)
        return _dot_nt(kcat, qs)

    units = [(i, kv) for i in range(nq) for kv in range(C_KV_HEADS)]
    scores = {}
    for idx, (i, kv) in enumerate(units):
        for u in units[len(scores):idx + SWA_AHEAD + 1]:
            scores[u] = score_unit(*u)
        st = scores[(i, kv)]
        rows = slice(i * BLOCK, (i + 1) * BLOCK)
        if True:
            sm_all, sb_all = st[:N_META], st[N_META:]
            vt = jnp.concatenate([r[0, kv * C_DH:(kv + 1) * C_DH, :]
                                  for r in (vm_ref, v_refs[i], v_refs[i + 1], v_refs[i + 2])], axis=1)
            vt = jnp.concatenate([vt, ones], axis=0)
            normed = []
            for g in range(group):
                sink = sink_ref[kv * group + g] * LOG2E
                sm = sm_all[:, g * BLOCK:(g + 1) * BLOCK]
                sb = jnp.where(masks[i], sb_all[:, g * BLOCK:(g + 1) * BLOCK], NEG)
                m = jnp.maximum(jnp.maximum(jnp.max(sm, axis=0, keepdims=True),
                                            jnp.max(sb, axis=0, keepdims=True)), sink)
                pt = jnp.concatenate([no_pad_keys, jnp.exp2(sm - m).astype(BF16),
                                      jnp.exp2(sb - m).astype(BF16)], axis=0)
                ot = _dot(vt, pt)
                den = ot[C_DH:C_DH + 1, :] + jnp.exp2(sink - m)
                normed.append(ot[:C_DH, :] * (1.0 / den))
            for j in range(group // 2):
                slab = jnp.concatenate([normed[2 * j], normed[2 * j + 1]], axis=0).T
                sl = slice((kv * group // 2 + j) * LANES, (kv * group // 2 + j + 1) * LANES)
                o_ref[0, rows, sl] = jnp.where(rows_ok[i], slab, 0.0).astype(BF16)


def _swa(q, k, vt, sink):
    b, lp, qw = q.shape
    nb = lp // BLOCK
    kw = k.shape[-1]
    vw = vt.shape[1]
    nq = _pick(nb, SWA_BLOCKS)
    meta = nb - 1

    def slot(t):
        def index(n):
            raw = n * nq - 1 + t
            return jnp.where((raw < 0) | (raw >= nb), 0, raw)
        return index

    slots = [slot(t) for t in range(nq + 2)]
    qspec = pl.BlockSpec((1, nq * BLOCK, qw), lambda i, n: (i, n, 0))
    kspecs = [pl.BlockSpec((1, N_META, kw), lambda i, n: (i, lp // N_META - 1, 0))]
    kspecs += [pl.BlockSpec((1, BLOCK, kw), lambda i, n, f=f: (i, f(n), 0)) for f in slots]
    vspecs = [pl.BlockSpec((1, vw, BLOCK), lambda i, n: (i, 0, meta))]
    vspecs += [pl.BlockSpec((1, vw, BLOCK), lambda i, n, f=f: (i, 0, f(n))) for f in slots]
    return pl.pallas_call(
        functools.partial(_swa_body, nb=nb, nq=nq),
        grid=(b, nb // nq),
        in_specs=[pl.BlockSpec(memory_space=pltpu.SMEM), qspec] + kspecs + vspecs,
        out_specs=qspec,
        out_shape=jax.ShapeDtypeStruct((b, lp, qw), BF16),
        compiler_params=_params("parallel", "parallel"),
        name="swa",
    )(sink, q, *([k] * (nq + 3)), *([vt] * (nq + 3)))


def _rope_table(lp, theta, rot, offset):
    half = rot // 2
    lane = np.arange(LANES) - offset
    in_lo = (lane >= 0) & (lane < half)
    in_hi = (lane >= half) & (lane < rot)
    idx = np.where(in_lo | in_hi, lane % half, 0)
    inv = theta ** (-jnp.asarray(idx, F32) * 2.0 / rot)
    ang = _positions(lp)[:, None] * inv[None, :]
    sin = jnp.sin(ang)
    return jnp.stack([jnp.where(in_lo | in_hi, jnp.cos(ang), 1.0),
                      jnp.where(in_lo, -sin, jnp.where(in_hi, sin, 0.0))])


def _split_rope_table(lp, theta, rot):
    lanes = np.arange(LANES)
    r = lanes % (HALF // 2)
    active = jnp.asarray(r < rot // 2)
    inv = theta ** (-jnp.asarray(np.where(r < rot // 2, r, 0), F32) * 2.0 / rot)
    ang = _positions(lp)[:, None] * inv[None, :]
    sign = jnp.where(jnp.asarray(lanes < HALF), -1.0, 1.0)
    return jnp.stack([jnp.where(active, jnp.cos(ang), 1.0), jnp.where(active, sign * jnp.sin(ang), 0.0)])


def _pair_lanes(x):
    h, n = x.shape
    return jnp.repeat(x.reshape(h // 2, 2, n).transpose(0, 2, 1), HALF, axis=2)


def _prep_ffn(gain, wg, wu, wd):
    d, f = wg.shape
    return (gain.reshape(1, d), wg.astype(BF16), wu.astype(BF16),
            wd.reshape(f // FFN_CHUNK, FFN_CHUNK, d).astype(BF16))


def _prep_even(lp, mix_g, w_in, dec_f, dec_b, ret_norm, q_norm, w_qb, kv_norm, w_kvb, gq, gk, w_out):
    d = w_in.shape[0]
    hw = A_HEADS * A_DK
    cut = 4 * hw + B_Q_LORA + B_KV_LORA
    lanes = np.arange(LANES)
    partner = lanes.copy()
    partner[B_NOPE:B_NOPE + B_ROPE // 2] += B_ROPE // 2
    partner[B_NOPE + B_ROPE // 2:B_QK] -= B_ROPE // 2
    is_rope = jnp.asarray((lanes >= B_NOPE) & (lanes < B_QK))
    rot = lambda w: jnp.where(is_rope, w[..., partner], 0.0)
    kr_cols = jnp.concatenate([jnp.zeros((d, B_NOPE), F32), w_in[:, cut:],
                               jnp.zeros((d, LANES - B_QK), F32)], axis=1)
    qk_perm = lambda w: w.reshape(d, A_HEADS // 2, 2, 2, A_DK // 2).transpose(0, 1, 3, 2, 4).reshape(d, hw)
    w_in2 = jnp.concatenate([w_in[:, 4 * hw:cut], kr_cols, rot(kr_cols),
                             qk_perm(w_in[:, :hw]), qk_perm(w_in[:, hw:2 * hw]), w_in[:, 2 * hw:4 * hw]],
                            axis=1).astype(BF16)
    w_qb2 = jnp.pad(w_qb.reshape(B_Q_LORA, B_HEADS, B_QK), ((0, 0), (0, 0), (0, LANES - B_QK)))
    w_kv3 = w_kvb.reshape(B_KV_LORA, B_HEADS, B_NOPE + B_DV)
    w_kb = jnp.pad(w_kv3[:, :, :B_NOPE], ((0, 0), (0, 0), (0, LANES - B_NOPE)))
    w_vb = jnp.pad(w_kv3[:, :, B_NOPE:], ((0, 0), (0, 0), (0, LANES - B_DV)))
    pad96 = lambda g: jnp.pad(g, (0, LANES - B_QK)).reshape(1, LANES)
    with_rot = lambda g: jnp.concatenate([g, rot(g)], axis=0)
    ret_tab = _split_rope_table(lp, A_THETA, A_DK)
    qk_lanes = lambda x: _pair_lanes(x).reshape(x.shape[0] // 2, x.shape[1], 2, 2, A_DK // 2
                                                ).transpose(0, 1, 3, 2, 4).reshape(-1, x.shape[1], LANES)

    lgf = -jnp.exp(dec_f.astype(F32))
    lgb = -jnp.exp(dec_b.astype(F32))
    idx = jnp.arange(BLOCK, dtype=F32)
    diff = idx[:, None] - idx[None, :]
    dmat = (jnp.where(diff >= 0, jnp.exp(lgf[:, None, None] * jnp.maximum(diff, 0.0)), 0.0)
            + jnp.where(diff < 0, jnp.exp(lgb[:, None, None] * jnp.maximum(-diff, 0.0)), 0.0))
    ones = jnp.ones((BLOCK,), F32)
    vec = jnp.stack([
        qk_lanes(jnp.exp(lgf[:, None] * (BLOCK - 1 - idx)[None, :])),
        qk_lanes(jnp.exp(lgb[:, None] * idx[None, :])),
        qk_lanes(jnp.exp(lgf[:, None] * (idx + 1.0)[None, :])),
        qk_lanes(jnp.exp(lgb[:, None] * (BLOCK - idx)[None, :])),
        _pair_lanes(jnp.exp(BLOCK * lgf)[:, None] * ones[None, :]),
        _pair_lanes(jnp.exp(BLOCK * lgb)[:, None] * ones[None, :]),
    ], axis=1)
    half = A_HEADS * A_DK
    return dict(
        mix_g=mix_g.reshape(1, d), w_in=w_in2,
        q_norm=q_norm.reshape(1, -1), w_qb=w_qb2.reshape(B_Q_LORA, -1).astype(BF16),
        w_qr=rot(w_qb2).reshape(B_Q_LORA, -1).astype(BF16),
        kv_norm=kv_norm.reshape(1, -1), w_kb=w_kb.reshape(B_KV_LORA, -1).astype(BF16),
        w_vb=w_vb.reshape(B_KV_LORA, -1).astype(BF16),
        gq=with_rot(pad96(gq) * (B_QK ** -0.5 * LOG2E)), gk=with_rot(pad96(gk)),
        ret_tab=ret_tab,
        mla_tab=_rope_table(lp, B_THETA, B_ROPE, B_NOPE),
        ret_dmat=dmat, ret_vec=vec,
        ret_gain=ret_norm.reshape(A_HEADS // 2, 1, LANES),
        w_out=(w_out[:half].astype(BF16), w_out[half:].astype(BF16)),
    )


def _prep_odd(lp, mix_g, w_in, gq, gk, sink, w_out):
    d = w_in.shape[0]
    nq = C_HEADS * C_DH
    nkv = C_KV_HEADS * C_DH
    qtr, hr = C_DH // 2, C_ROT // 2
    perm = np.concatenate([np.arange(hr), np.arange(C_ROT, C_ROT + qtr - hr),
                           np.arange(hr, C_ROT), np.arange(C_ROT + qtr - hr, C_DH)])
    split = lambda w, heads: w.reshape(-1, heads, C_DH)[:, :, perm].reshape(-1, heads, 2, qtr)
    q_cols = lambda w: split(w, C_HEADS).reshape(-1, C_HEADS // 2, 2, 2, qtr).transpose(0, 1, 3, 2, 4
                                                                                         ).reshape(-1, nq)
    k_cols = lambda w: jnp.repeat(split(w, C_KV_HEADS)[:, :, :, None, :], 2, axis=3).reshape(-1, 2 * nkv)
    w_in2 = jnp.concatenate([q_cols(w_in[:, :nq]), k_cols(w_in[:, nq:nq + nkv]), w_in[:, nq + nkv:]], axis=1)
    slab_gain = lambda g: jnp.repeat(g[perm].reshape(2, 1, qtr), 2, axis=1).reshape(1, LANES)
    return dict(
        mix_g=mix_g.reshape(1, d), w_in=w_in2.astype(BF16),
        gq=slab_gain(gq) * (C_DH ** -0.5 * LOG2E), gk=slab_gain(gk),
        swa_tab=_split_rope_table(lp, C_THETA, C_ROT),
        sink=sink.astype(F32), w_out=(w_out.astype(BF16),),
    )


def _trunk(x, meta, layers):
    b, seq, d = x.shape
    lp = seq + BLOCK
    tail = jnp.concatenate([jnp.zeros((FRONT_PAD, d), x.dtype), meta.astype(x.dtype)], axis=0)
    tail = _ffn(tail, layers[0]["ffn0"])
    h = _ffn_head(x, tail, layers[0]["ffn0"]).reshape(b * lp, d)
    for i, layer in enumerate(layers):
        h3 = h.reshape(b, lp, d)
        mp = layer["mix"]
        if layer["even"]:
            rq, rk, rv, rg, mq, mk, mv = _even_in(h3, mp)
            xs = [_retention(rq, rk, rv, rg, mp), _mla(mq, mk, mv)]
        else:
            q, k, v = _odd_in(h3, mp)
            xs = [_swa(q, k, v, mp["sink"])]
        if i + 1 == len(layers):
            return _proj_ffn_tail(h3, xs, mp["w_out"], layer["ffn1"])
        xs = [a.reshape(b * lp, a.shape[-1]) for a in xs]
        h = _proj_ffn(h, xs, mp["w_out"], layer["ffn1"])
        h = _ffn(h, layers[i + 1]["ffn0"])


def kernel(x_prompt, x_sample, meta_tokens, ffn_norm, ffn_w_gate, ffn_w_up, ffn_w_down, mix_norm, even_w_in, ret_decay_f, ret_decay_b, ret_out_norm, mla_q_norm, mla_w_qb, mla_kv_norm, mla_w_kvb, mla_qk_norm_q, mla_qk_norm_k, even_w_out, odd_w_in, swa_q_norm, swa_k_norm, swa_sink, odd_w_out):
    depth = ffn_norm.shape[0]
    assert x_prompt.shape[1] == x_sample.shape[1]
    lp = x_prompt.shape[1] + BLOCK
    layers = []
    for layer in range(depth):
        i = layer // 2
        ffn = [_prep_ffn(ffn_norm[layer, s], ffn_w_gate[layer, s], ffn_w_up[layer, s], ffn_w_down[layer, s])
               for s in range(2)]
        if layer % 2 == 0:
            mix = _prep_even(lp, mix_norm[layer], even_w_in[i], ret_decay_f[i], ret_decay_b[i],
                             ret_out_norm[i], mla_q_norm[i], mla_w_qb[i], mla_kv_norm[i], mla_w_kvb[i],
                             mla_qk_norm_q[i], mla_qk_norm_k[i], even_w_out[i])
        else:
            mix = _prep_odd(lp, mix_norm[layer], odd_w_in[i], swa_q_norm[i], swa_k_norm[i],
                            swa_sink[i], odd_w_out[i])
        layers.append(dict(even=layer % 2 == 0, ffn0=ffn[0], ffn1=ffn[1], mix=mix))
    return (_trunk(x_prompt, meta_tokens, layers), _trunk(x_sample, meta_tokens, layers))
```
